```python
import math
import jax, jax.numpy as jnp
from jax import lax
import numpy as np

D_MODEL = 1024
BATCH = 32
SEQ = 256
DEPTH = 2
DEC_BATCH = 4
DEC_SEQ = 4096
PAST_LEN = 256

GRID_W = 64
N_HEADS_A = 4
HEAD_DIM_A = 64
V_DIM_A = 2 * HEAD_DIM_A
QK_WIDTH = N_HEADS_A * 2 * HEAD_DIM_A
WIDTH_A = N_HEADS_A * V_DIM_A
N_GROUPS_B = 4
CHUNK = 128
GROUP_DIM_B = 128
WIDTH_B = N_GROUPS_B * GROUP_DIM_B
IN_WIDTH_0 = 2 * QK_WIDTH + WIDTH_A + 2 * WIDTH_B
MIX_WIDTH_0 = WIDTH_A + WIDTH_B
WIDTH_C = D_MODEL
CONV_W = 3
D_FF = 4 * D_MODEL
ROPE_BASE = 10000.0
ROPE_PAIRS = HEAD_DIM_A // 4
LN_EPS = 1e-5
Q_BLOCK = 128
ALPHA = (2 * DEPTH) ** 0.25
BETA = (8 * DEPTH) ** -0.25
LAMBDA_INIT_0 = 0.8 - 0.6 * math.exp(-0.3 * 0)

kernel_name = 'hybrid_diffattn_sgu_shortconv_deepnorm_step'


def layer_norm(x, g, b):
    xf = x.astype(jnp.float32)
    mu = jnp.mean(xf, -1, keepdims=True)
    var = jnp.mean(jnp.square(xf - mu), -1, keepdims=True)
    return ((xf - mu) * lax.rsqrt(var + LN_EPS)).astype(x.dtype) * g + b


def layer_norm_plain(x):
    xf = x.astype(jnp.float32)
    mu = jnp.mean(xf, -1, keepdims=True)
    var = jnp.mean(jnp.square(xf - mu), -1, keepdims=True)
    return ((xf - mu) * lax.rsqrt(var + LN_EPS)).astype(x.dtype)


def rms_norm(x, g):
    xf = x.astype(jnp.float32)
    return (xf * lax.rsqrt(jnp.mean(jnp.square(xf), -1, keepdims=True) + LN_EPS)).astype(x.dtype) * g


def adaln(cvec, w_mod, b_mod):
    m = jax.nn.silu(cvec) @ w_mod + b_mod
    return [t[:, None, :] for t in jnp.split(m, 6, axis=-1)]


def grid_angles(n):
    rows = n // GRID_W
    row = jnp.repeat(jnp.arange(rows, dtype=jnp.float32), GRID_W)
    col = jnp.tile(jnp.arange(GRID_W, dtype=jnp.float32), rows)
    inv = 1.0 / (ROPE_BASE ** (jnp.arange(ROPE_PAIRS, dtype=jnp.float32) / ROPE_PAIRS))
    return row[:, None] * inv, col[:, None] * inv


def rotate(x, ang):
    cos = jnp.cos(ang)[:, None, None, :].astype(x.dtype)
    sin = jnp.sin(ang)[:, None, None, :].astype(x.dtype)
    x1, x2 = jnp.split(x, 2, axis=-1)
    return jnp.concatenate([x1 * cos - x2 * sin, x1 * sin + x2 * cos], axis=-1)


def axial_rope(x, ang_r, ang_c):
    half = HEAD_DIM_A // 2
    return jnp.concatenate([rotate(x[..., :half], ang_r), rotate(x[..., half:], ang_c)], axis=-1)


def diff_lambda(lq1, lk1, lq2, lk2, lambda_init):
    f = jnp.float32
    return (jnp.exp(jnp.sum(lq1.astype(f) * lk1.astype(f)))
            - jnp.exp(jnp.sum(lq2.astype(f) * lk2.astype(f))) + lambda_init)


def diff_attend(q, k, v, lam):
    s = jnp.einsum('bqhid,bkhid->bhiqk', q, k).astype(jnp.float32) * (HEAD_DIM_A ** -0.5)
    p = jax.nn.softmax(s, axis=-1)
    a = p[:, :, 0] - lam * p[:, :, 1]
    return jnp.einsum('bhqk,bkhv->bqhv', a.astype(v.dtype), v)


def blocked_diff_attend(q, k, v, lam):
    b, n = q.shape[:2]
    nb = n // Q_BLOCK
    qb = jnp.moveaxis(q.reshape(b, nb, Q_BLOCK, N_HEADS_A, 2, HEAD_DIM_A), 1, 0)
    ob = lax.map(lambda qi: diff_attend(qi, k, v, lam), qb)
    return jnp.moveaxis(ob, 0, 1).reshape(b, n, N_HEADS_A, V_DIM_A)


def ab_project(h, w_in):
    b, n, _ = h.shape
    q, k, v, u, g = jnp.split(h @ w_in, [QK_WIDTH, 2 * QK_WIDTH, 2 * QK_WIDTH + WIDTH_A,
                                         2 * QK_WIDTH + WIDTH_A + WIDTH_B], axis=-1)
    return (q.reshape(b, n, N_HEADS_A, 2, HEAD_DIM_A), k.reshape(b, n, N_HEADS_A, 2, HEAD_DIM_A),
            v.reshape(b, n, N_HEADS_A, V_DIM_A), u.reshape(b, n, N_GROUPS_B, GROUP_DIM_B),
            g.reshape(b, n, N_GROUPS_B, GROUP_DIM_B))


def chunk_sgu(u, v, sgu_w, sgu_b):
    b, n = u.shape[:2]
    vc = layer_norm_plain(v).reshape(b, n // CHUNK, CHUNK, N_GROUPS_B, GROUP_DIM_B)
    mixed = jnp.einsum('gpq,bcqgd->bcpgd', sgu_w, vc) + sgu_b.T[None, None, :, :, None]
    return u * mixed.reshape(b, n, N_GROUPS_B, GROUP_DIM_B)


def ab_output(attn, u, g, subln_g, sgu_w, sgu_b, w_out):
    b, n = attn.shape[:2]
    a = rms_norm(attn, subln_g) * (1.0 - LAMBDA_INIT_0)
    s = chunk_sgu(u, g, sgu_w, sgu_b)
    return jnp.concatenate([a.reshape(b, n, WIDTH_A), s.reshape(b, n, WIDTH_B)], axis=-1) @ w_out


def short_conv_mixer(h, w_in, conv_w, w_out):
    bg, cg, xt = jnp.split(h @ w_in, 3, axis=-1)
    z = cg * xt
    n = h.shape[1]
    pad = CONV_W // 2
    zp = jnp.pad(z, ((0, 0), (pad, pad), (0, 0)))
    y = zp[:, 0:n] * conv_w[0]
    for j in range(1, CONV_W):
        y = y + zp[:, j:j + n] * conv_w[j]
    return (bg * y) @ w_out


def sq_relu_mlp(h, w1, w2):
    return jnp.square(jax.nn.relu(h @ w1)) @ w2


def setup_inputs(seed: int = 0) -> dict:
    key = jax.random.key(seed)
    ks = iter(jax.random.split(key, 64))
    f = jnp.float32
    d = D_MODEL

    def nrm(shape, scale):
        return jax.random.normal(next(ks), shape, f) * scale

    return {
        'x_prompt': nrm((BATCH, SEQ, d), 1.0),
        'x_sample': nrm((DEC_BATCH, DEC_SEQ, d), 1.0),
        'cache_k0': nrm((DEC_BATCH, PAST_LEN, N_HEADS_A, 2, HEAD_DIM_A), 1.0),
        'cache_v0': nrm((DEC_BATCH, PAST_LEN, N_HEADS_A, V_DIM_A), 1.0),
        'c': nrm((DEC_BATCH, d), 1.0),
        'c_ctx': nrm((d,), 1.0),
        'w_mod0': nrm((d, 6 * d), d ** -0.5),
        'b_mod0': nrm((6 * d,), 0.02),
        'w_in0': nrm((d, IN_WIDTH_0), d ** -0.5),
        'lambda_q1_0': nrm((HEAD_DIM_A,), 0.1),
        'lambda_k1_0': nrm((HEAD_DIM_A,), 0.1),
        'lambda_q2_0': nrm((HEAD_DIM_A,), 0.1),
        'lambda_k2_0': nrm((HEAD_DIM_A,), 0.1),
        'subln_g0': 1.0 + nrm((V_DIM_A,), 0.02),
        'sgu_w0': nrm((N_GROUPS_B, CHUNK, CHUNK), CHUNK ** -0.5),
        'sgu_b0': 1.0 + nrm((N_GROUPS_B, CHUNK), 0.02),
        'w_out0': nrm((MIX_WIDTH_0, d), BETA * MIX_WIDTH_0 ** -0.5),
        'ln_mix_g0': 1.0 + nrm((d,), 0.02),
        'ln_mix_b0': nrm((d,), 0.02),
        'w_ff1_0': nrm((d, D_FF), d ** -0.5),
        'w_ff2_0': nrm((D_FF, d), BETA * D_FF ** -0.5),
        'ln_ff_g0': 1.0 + nrm((d,), 0.02),
        'ln_ff_b0': nrm((d,), 0.02),
        'w_mod1': nrm((d, 6 * d), d ** -0.5),
        'b_mod1': nrm((6 * d,), 0.02),
        'w_in1': nrm((d, 3 * WIDTH_C), d ** -0.5),
        'conv_w1': nrm((CONV_W, WIDTH_C), CONV_W ** -0.5),
        'w_out1': nrm((WIDTH_C, d), BETA * WIDTH_C ** -0.5),
        'ln_mix_g1': 1.0 + nrm((d,), 0.02),
        'ln_mix_b1': nrm((d,), 0.02),
        'w_ff1_1': nrm((d, D_FF), d ** -0.5),
        'w_ff2_1': nrm((D_FF, d), BETA * D_FF ** -0.5),
        'ln_ff_g1': 1.0 + nrm((d,), 0.02),
        'ln_ff_b1': nrm((d,), 0.02),
    }


def reference(x_prompt, x_sample, cache_k0, cache_v0, c, c_ctx,
              w_mod0, b_mod0, w_in0, lambda_q1_0, lambda_k1_0, lambda_q2_0, lambda_k2_0,
              subln_g0, sgu_w0, sgu_b0, w_out0, ln_mix_g0, ln_mix_b0, w_ff1_0, w_ff2_0,
              ln_ff_g0, ln_ff_b0,
              w_mod1, b_mod1, w_in1, conv_w1, w_out1, ln_mix_g1, ln_mix_b1, w_ff1_1, w_ff2_1,
              ln_ff_g1, ln_ff_b1):
    shared = ((w_mod0, b_mod0, ln_mix_g0, ln_mix_b0, w_ff1_0, w_ff2_0, ln_ff_g0, ln_ff_b0),
              (w_mod1, b_mod1, ln_mix_g1, ln_mix_b1, w_ff1_1, w_ff2_1, ln_ff_g1, ln_ff_b1))
    ang_r, ang_c = grid_angles(x_sample.shape[1])
    xp, xs = x_prompt, x_sample
    new_k0 = new_v0 = None
    for layer in range(DEPTH):
        w_mod, b_mod, g_mix, bt_mix, w_ff1, w_ff2, g_ff, bt_ff = shared[layer]
        sh_p, sc_p, ga_p, shf_p, scf_p, gaf_p = adaln(c_ctx[None, :], w_mod, b_mod)
        sh_s, sc_s, ga_s, shf_s, scf_s, gaf_s = adaln(c, w_mod, b_mod)
        hp = xp * (1.0 + sc_p) + sh_p
        hs = xs * (1.0 + sc_s) + sh_s
        if layer % 2 == 0:
            lam = diff_lambda(lambda_q1_0, lambda_k1_0, lambda_q2_0, lambda_k2_0, LAMBDA_INIT_0)
            qp, kp, vp, up, gp = ab_project(hp, w_in0)
            out_p = ab_output(diff_attend(qp, kp, vp, lam), up, gp, subln_g0, sgu_w0, sgu_b0, w_out0)
            new_k0, new_v0 = kp, vp
            qs, ks_, vs, us, gs = ab_project(hs, w_in0)
            qs = axial_rope(qs, ang_r, ang_c)
            k_all = jnp.concatenate([cache_k0, axial_rope(ks_, ang_r, ang_c)], axis=1)
            v_all = jnp.concatenate([cache_v0, vs], axis=1)
            out_s = ab_output(blocked_diff_attend(qs, k_all, v_all, lam), us, gs,
                              subln_g0, sgu_w0, sgu_b0, w_out0)
        else:
            out_p = short_conv_mixer(hp, w_in1, conv_w1, w_out1)
            out_s = short_conv_mixer(hs, w_in1, conv_w1, w_out1)
        xp = layer_norm(ALPHA * xp + ga_p * out_p, g_mix, bt_mix)
        xs = layer_norm(ALPHA * xs + ga_s * out_s, g_mix, bt_mix)
        fp = sq_relu_mlp(xp * (1.0 + scf_p) + shf_p, w_ff1, w_ff2)
        fs = sq_relu_mlp(xs * (1.0 + scf_s) + shf_s, w_ff1, w_ff2)
        xp = layer_norm(ALPHA * xp + gaf_p * fp, g_ff, bt_ff)
        xs = layer_norm(ALPHA * xs + gaf_s * fs, g_ff, bt_ff)
    return (xp, xs, new_k0, new_v0)
```

```python
import functools
import math

import jax
import jax.numpy as jnp
from jax import lax
from jax.experimental import pallas as pl
from jax.experimental.pallas import tpu as pltpu

F32 = jnp.float32
BF16 = jnp.bfloat16

D_MODEL = 1024
DEPTH = 2
SEQ = 256
DEC_SEQ = 4096
PAST_LEN = 256
GRID_W = 64
N_HEADS_A = 4
HEAD_DIM_A = 64
V_DIM_A = 2 * HEAD_DIM_A
QK_WIDTH = N_HEADS_A * 2 * HEAD_DIM_A
WIDTH_A = N_HEADS_A * V_DIM_A
N_GROUPS_B = 4
CHUNK = 128
GROUP_DIM_B = 128
WIDTH_B = N_GROUPS_B * GROUP_DIM_B
IN_WIDTH_0 = 2 * QK_WIDTH + WIDTH_A + 2 * WIDTH_B
WIDTH_C = D_MODEL
D_FF = 4 * D_MODEL
ROPE_BASE = 10000.0
ROPE_PAIRS = HEAD_DIM_A // 4
LN_EPS = 1e-5
ALPHA = (2 * DEPTH) ** 0.25
LAMBDA_INIT_0 = 0.8 - 0.6 * math.exp(-0.3 * 0)

LANES = 128
SUBLANES = 8
VMEM_LIMIT = 56 * 2 ** 20
N_MOD = 6 * D_MODEL
MOD_ROWS = 8
TOK_TILE = 512
ATT_Q_BLOCK = 128
FF_CHUNK = 1024
MOD_TILE = 512


def _const_spec(shape):
    zeros = (0,) * len(shape)
    return pl.BlockSpec(shape, lambda *_: zeros, pipeline_mode=pl.Buffered(1))


def _params(*sem):
    return pltpu.CompilerParams(dimension_semantics=sem, vmem_limit_bytes=VMEM_LIMIT)


def _layer_norm(x, g, b):
    mu = jnp.mean(x, axis=-1, keepdims=True)
    xc = x - mu
    var = jnp.mean(xc * xc, axis=-1, keepdims=True)
    return xc * lax.rsqrt(var + LN_EPS) * g + b


def _mod(mod_ref, idx):
    return mod_ref[:, idx * D_MODEL:(idx + 1) * D_MODEL]


def _mod_kernel(ct_ref, w0_ref, b0_ref, w1_ref, b1_ref, o0_ref, o1_ref, *, rows):
    ct = ct_ref[...]
    s = ct * jax.nn.sigmoid(ct)
    for w_ref, b_ref, o_ref in ((w0_ref, b0_ref, o0_ref), (w1_ref, b1_ref, o1_ref)):
        w = w_ref[...]
        b = b_ref[...]
        for r in range(rows):
            o_ref[r:r + 1, :] = jnp.sum(w * s[:, r:r + 1], axis=0, keepdims=True) + b
        o_ref[rows:, :] = jnp.zeros((MOD_ROWS - rows, w.shape[1]), F32)


def _modulations(ct, w0, b0, w1, b1, rows):
    w_spec = pl.BlockSpec((D_MODEL, MOD_TILE), lambda j: (0, j))
    v_spec = pl.BlockSpec((1, MOD_TILE), lambda j: (0, j))
    o_spec = pl.BlockSpec((MOD_ROWS, MOD_TILE), lambda j: (0, j))
    out = jax.ShapeDtypeStruct((MOD_ROWS, N_MOD), F32)
    return pl.pallas_call(
        functools.partial(_mod_kernel, rows=rows),
        grid=(N_MOD // MOD_TILE,),
        in_specs=[_const_spec((D_MODEL, MOD_ROWS)), w_spec, v_spec, w_spec, v_spec],
        out_specs=[o_spec, o_spec],
        out_shape=[out, out],
        compiler_params=_params("parallel"),
        name="adaln_mod",
    )(ct, w0, b0, w1, b1)


def _mod_spec(row_fn):
    return pl.BlockSpec((None, 1, N_MOD), lambda i: (row_fn(i), 0, 0))


def _in0_kernel(*refs, rope):
    if rope:
        (x_ref, mod_ref, w_ref, cos_ref, sa_ref, sb_ref,
         q_ref, k_ref, v_ref, u_ref, vc_ref) = refs
    else:
        x_ref, mod_ref, w_ref, q_ref, k_ref, v_ref, u_ref, vc_ref = refs
    h = (x_ref[...] * (1.0 + _mod(mod_ref, 1)) + _mod(mod_ref, 0)).astype(BF16)
    y = jnp.dot(h, w_ref[...], preferred_element_type=F32)
    if rope:
        cos, sa, sb = cos_ref[...], sa_ref[...], sb_ref[...]
    for j in range(QK_WIDTH // LANES):
        lo, hi = j * LANES, (j + 1) * LANES
        qj = y[:, lo:hi]
        kj = y[:, QK_WIDTH + lo:QK_WIDTH + hi]
        if rope:
            qj = (qj * cos + pltpu.roll(qj, LANES - ROPE_PAIRS, 1) * sa
                  + pltpu.roll(qj, ROPE_PAIRS, 1) * sb)
            kj = (kj * cos + pltpu.roll(kj, LANES - ROPE_PAIRS, 1) * sa
                  + pltpu.roll(kj, ROPE_PAIRS, 1) * sb)
        q_ref[:, lo:hi] = (qj * (HEAD_DIM_A ** -0.5)).astype(q_ref.dtype)
        k_ref[:, lo:hi] = kj.astype(k_ref.dtype)
    off = 2 * QK_WIDTH
    v_ref[...] = y[:, off:off + WIDTH_A].astype(v_ref.dtype)
    off += WIDTH_A
    u_ref[...] = y[:, off:off + WIDTH_B]
    off += WIDTH_B
    for g in range(N_GROUPS_B):
        gg = y[:, off + g * GROUP_DIM_B:off + (g + 1) * GROUP_DIM_B]
        mu = jnp.mean(gg, axis=-1, keepdims=True)
        gc = gg - mu
        var = jnp.mean(gc * gc, axis=-1, keepdims=True)
        vc_ref[:, g * GROUP_DIM_B:(g + 1) * GROUP_DIM_B] = (
            gc * lax.rsqrt(var + LN_EPS)).astype(vc_ref.dtype)


def _in_proj0(x, mods, w_in, row_fn, rope_tabs, kv_dtype):
    n = x.shape[0]
    t = TOK_TILE
    tok = lambda width: pl.BlockSpec((t, width), lambda i: (i, 0))
    in_specs = [tok(D_MODEL), _mod_spec(row_fn), _const_spec((D_MODEL, IN_WIDTH_0))]
    args = [x, mods, w_in]
    if rope_tabs is not None:
        blocks_per_seq = DEC_SEQ // t
        tab = pl.BlockSpec((t, LANES), lambda i: (i % blocks_per_seq, 0))
        in_specs += [tab, tab, tab]
        args += list(rope_tabs)
    out_shape = [jax.ShapeDtypeStruct((n, QK_WIDTH), BF16),
                 jax.ShapeDtypeStruct((n, QK_WIDTH), kv_dtype),
                 jax.ShapeDtypeStruct((n, WIDTH_A), kv_dtype),
                 jax.ShapeDtypeStruct((n, WIDTH_B), F32),
                 jax.ShapeDtypeStruct((n, WIDTH_B), BF16)]
    return pl.pallas_call(
        functools.partial(_in0_kernel, rope=rope_tabs is not None),
        grid=(n // t,),
        in_specs=in_specs,
        out_specs=[tok(QK_WIDTH), tok(QK_WIDTH), tok(WIDTH_A), tok(WIDTH_B), tok(WIDTH_B)],
        out_shape=out_shape,
        compiler_params=_params("parallel"),
        name="in_proj0",
    )(*args)


def _lambda(lp_ref):
    lp = lp_ref[...]
    a = jnp.sum(lp[0:1] * lp[1:2], axis=1, keepdims=True)
    b = jnp.sum(lp[2:3] * lp[3:4], axis=1, keepdims=True)
    return jnp.exp(a) - jnp.exp(b) + LAMBDA_INIT_0


def _stack_halves(qh):
    lane = lax.broadcasted_iota(jnp.int32, qh.shape, 1)
    zero = jnp.zeros_like(qh)
    return jnp.concatenate([jnp.where(lane < HEAD_DIM_A, qh, zero),
                            jnp.where(lane >= HEAD_DIM_A, qh, zero)], axis=0)


def _scores(qq, k):
    return lax.dot_general(qq, k, (((1,), (1,)), ((), ())), preferred_element_type=F32)


def _head_out(pv, l, lam, gain, rows):
    o = pv[:rows] / l[:rows] - lam * (pv[rows:] / l[rows:])
    ms = jnp.mean(o * o, axis=-1, keepdims=True)
    return o * lax.rsqrt(ms + LN_EPS) * gain * (1.0 - LAMBDA_INIT_0)


def _attn_ctx_kernel(q_ref, k_ref, v_ref, lp_ref, gain_ref, o_ref):
    lam = _lambda(lp_ref)
    gain = gain_ref[...]
    rows = q_ref.shape[0]
    for h in range(N_HEADS_A):
        lo, hi = h * LANES, (h + 1) * LANES
        qq = _stack_halves(q_ref[:, lo:hi])
        s = _scores(qq, k_ref[:, lo:hi].astype(BF16))
        p = jnp.exp(s - jnp.max(s, axis=-1, keepdims=True))
        l = jnp.sum(p, axis=-1, keepdims=True)
        pv = jnp.dot(p.astype(BF16), v_ref[:, lo:hi].astype(BF16),
                     preferred_element_type=F32)
        o_ref[:, lo:hi] = _head_out(pv, l, lam, gain, rows).astype(o_ref.dtype)


def _attn_ctx(q, k, v, lam_params, gain):
    n = q.shape[0]
    blk = lambda: pl.BlockSpec((SEQ, QK_WIDTH), lambda b: (b, 0))
    return pl.pallas_call(
        _attn_ctx_kernel,
        grid=(n // SEQ,),
        in_specs=[blk(), blk(), blk(), _const_spec((4, HEAD_DIM_A)), _const_spec((1, V_DIM_A))],
        out_specs=blk(),
        out_shape=jax.ShapeDtypeStruct((n, WIDTH_A), BF16),
        compiler_params=_params("parallel"),
        name="attn_ctx",
    )(q, k, v, lam_params, gain)


def _attn_lat_kernel(q_ref, k_ref, v_ref, ck_ref, cv_ref, lp_ref, gain_ref, o_ref):
    lam = _lambda(lp_ref)
    gain = gain_ref[...]
    rows = q_ref.shape[0]
    for h in range(N_HEADS_A):
        lo, hi = h * LANES, (h + 1) * LANES
        qq = _stack_halves(q_ref[:, lo:hi])
        s_c = _scores(qq, ck_ref[:, lo:hi].astype(BF16))
        s_n = _scores(qq, k_ref[:, lo:hi])
        m = jnp.maximum(jnp.max(s_c, axis=-1, keepdims=True),
                        jnp.max(s_n, axis=-1, keepdims=True))
        p_c = jnp.exp(s_c - m)
        p_n = jnp.exp(s_n - m)
        l = jnp.sum(p_c, axis=-1, keepdims=True) + jnp.sum(p_n, axis=-1, keepdims=True)
        pv = (jnp.dot(p_c.astype(BF16), cv_ref[:, lo:hi].astype(BF16),
                      preferred_element_type=F32)
              + jnp.dot(p_n.astype(BF16), v_ref[:, lo:hi], preferred_element_type=F32))
        o_ref[:, lo:hi] = _head_out(pv, l, lam, gain, rows).astype(o_ref.dtype)


def _attn_lat(q, k, v, cache_k, cache_v, lam_params, gain):
    n = q.shape[0]
    nb = n // DEC_SEQ
    qb = DEC_SEQ // ATT_Q_BLOCK
    q_spec = pl.BlockSpec((ATT_Q_BLOCK, QK_WIDTH), lambda b, j: (b * qb + j, 0))
    kv_spec = pl.BlockSpec((DEC_SEQ, QK_WIDTH), lambda b, j: (b, 0))
    c_spec = pl.BlockSpec((PAST_LEN, QK_WIDTH), lambda b, j: (b, 0))
    return pl.pallas_call(
        _attn_lat_kernel,
        grid=(nb, qb),
        in_specs=[q_spec, kv_spec, kv_spec, c_spec, c_spec,
                  _const_spec((4, HEAD_DIM_A)), _const_spec((1, V_DIM_A))],
        out_specs=q_spec,
        out_shape=jax.ShapeDtypeStruct((n, WIDTH_A), BF16),
        compiler_params=_params("parallel", "arbitrary"),
        name="attn_lat",
    )(q, k, v, cache_k, cache_v, lam_params, gain)


def _mix0_kernel(a_ref, u_ref, vc_ref, x_ref, mod_ref, sw_ref, sb_ref, wo_ref,
                 g_ref, b_ref, o_ref, s_scr):
    t = x_ref.shape[0]
    for g in range(N_GROUPS_B):
        lo, hi = g * GROUP_DIM_B, (g + 1) * GROUP_DIM_B
        w = sw_ref[g]
        bias = jnp.broadcast_to(sb_ref[:, g:g + 1], (CHUNK, GROUP_DIM_B))
        for c in range(t // CHUNK):
            r0, r1 = c * CHUNK, (c + 1) * CHUNK
            mixed = jnp.dot(w, vc_ref[r0:r1, lo:hi], preferred_element_type=F32) + bias
            s_scr[r0:r1, lo:hi] = (u_ref[r0:r1, lo:hi] * mixed).astype(BF16)
    out = (jnp.dot(a_ref[...], wo_ref[0:WIDTH_A, :], preferred_element_type=F32)
           + jnp.dot(s_scr[...], wo_ref[WIDTH_A:, :], preferred_element_type=F32))
    o_ref[...] = _layer_norm(ALPHA * x_ref[...] + _mod(mod_ref, 2) * out,
                             g_ref[...], b_ref[...])


def _mix0(attn, u, vc, x, mods, row_fn, sgu_w, sgu_bt, w_out, g, b):
    n = x.shape[0]
    t = TOK_TILE
    tok = lambda width: pl.BlockSpec((t, width), lambda i: (i, 0))
    return pl.pallas_call(
        _mix0_kernel,
        grid=(n // t,),
        in_specs=[tok(WIDTH_A), tok(WIDTH_B), tok(WIDTH_B), tok(D_MODEL), _mod_spec(row_fn),
                  _const_spec((N_GROUPS_B, CHUNK, CHUNK)), _const_spec((CHUNK, N_GROUPS_B)),
                  _const_spec((WIDTH_A + WIDTH_B, D_MODEL)),
                  _const_spec((1, D_MODEL)), _const_spec((1, D_MODEL))],
        out_specs=tok(D_MODEL),
        out_shape=jax.ShapeDtypeStruct((n, D_MODEL), F32),
        scratch_shapes=[pltpu.VMEM((t, WIDTH_B), BF16)],
        compiler_params=_params("parallel"),
        name="mix0",
    )(attn, u, vc, x, mods, sgu_w, sgu_bt, w_out, g, b)


def _mix1_kernel(xp_ref, x_ref, xn_ref, mod_ref, wi_ref, cw_ref, wo_ref, g_ref, b_ref,
                 o_ref, z_scr, *, seq_len):
    t = x_ref.shape[0]
    halo = SUBLANES
    xin = jnp.concatenate([xp_ref[...], x_ref[...], xn_ref[...]], axis=0)
    h = (xin * (1.0 + _mod(mod_ref, 1)) + _mod(mod_ref, 0)).astype(BF16)
    y = jnp.dot(h, wi_ref[...], preferred_element_type=F32)
    z_scr[...] = y[:, WIDTH_C:2 * WIDTH_C] * y[:, 2 * WIDTH_C:]
    pos = (pl.program_id(0) * t + lax.broadcasted_iota(jnp.int32, (t, 1), 0)) % seq_len
    z_prev = jnp.where(pos != 0, z_scr[halo - 1:halo - 1 + t, :], 0.0)
    z_next = jnp.where(pos != seq_len - 1, z_scr[halo + 1:halo + 1 + t, :], 0.0)
    cw = cw_ref[...]
    conv = z_prev * cw[0:1] + z_scr[halo:halo + t, :] * cw[1:2] + z_next * cw[2:3]
    gated = (y[halo:halo + t, 0:WIDTH_C] * conv).astype(BF16)
    out = jnp.dot(gated, wo_ref[...], preferred_element_type=F32)
    o_ref[...] = _layer_norm(ALPHA * x_ref[...] + _mod(mod_ref, 2) * out,
                             g_ref[...], b_ref[...])


def _mix1(x, mods, row_fn, w_in, conv_w, w_out, g, b, seq_len):
    n = x.shape[0]
    t = TOK_TILE
    tiles_per_block = t // SUBLANES
    last = n // SUBLANES - 1
    tok = pl.BlockSpec((t, D_MODEL), lambda i: (i, 0))
    prev = pl.BlockSpec((SUBLANES, D_MODEL),
                        lambda i: (jnp.maximum(i * tiles_per_block - 1, 0), 0))
    nxt = pl.BlockSpec((SUBLANES, D_MODEL),
                       lambda i: (jnp.minimum((i + 1) * tiles_per_block, last), 0))
    return pl.pallas_call(
        functools.partial(_mix1_kernel, seq_len=seq_len),
        grid=(n // t,),
        in_specs=[prev, tok, nxt, _mod_spec(row_fn),
                  _const_spec((D_MODEL, 3 * WIDTH_C)), _const_spec((3, WIDTH_C)),
                  _const_spec((WIDTH_C, D_MODEL)),
                  _const_spec((1, D_MODEL)), _const_spec((1, D_MODEL))],
        out_specs=tok,
        out_shape=jax.ShapeDtypeStruct((n, D_MODEL), F32),
        scratch_shapes=[pltpu.VMEM((t + 2 * SUBLANES, WIDTH_C), F32)],
        compiler_params=_params("parallel"),
        name="mix1",
    )(x, x, x, mods, w_in, conv_w, w_out, g, b)


def _ffn_kernel(x_ref, mod_ref, w1_ref, w2_ref, g_ref, b_ref, o_ref):
    x = x_ref[...]
    h = (x * (1.0 + _mod(mod_ref, 4)) + _mod(mod_ref, 3)).astype(BF16)
    f = None
    for j in range(D_FF // FF_CHUNK):
        lo, hi = j * FF_CHUNK, (j + 1) * FF_CHUNK
        hid = jnp.dot(h, w1_ref[:, lo:hi], preferred_element_type=F32)
        hid = jnp.square(jnp.maximum(hid, 0.0)).astype(BF16)
        part = jnp.dot(hid, w2_ref[lo:hi, :], preferred_element_type=F32)
        f = part if f is None else f + part
    o_ref[...] = _layer_norm(ALPHA * x + _mod(mod_ref, 5) * f, g_ref[...], b_ref[...])


def _ffn(x, mods, row_fn, w1, w2, g, b):
    n = x.shape[0]
    t = TOK_TILE
    tok = pl.BlockSpec((t, D_MODEL), lambda i: (i, 0))
    return pl.pallas_call(
        _ffn_kernel,
        grid=(n // t,),
        in_specs=[tok, _mod_spec(row_fn), _const_spec((D_MODEL, D_FF)),
                  _const_spec((D_FF, D_MODEL)), _const_spec((1, D_MODEL)),
                  _const_spec((1, D_MODEL))],
        out_specs=tok,
        out_shape=jax.ShapeDtypeStruct((n, D_MODEL), F32),
        compiler_params=_params("parallel"),
        name="ffn",
    )(x, mods, w1, w2, g, b)


def _rope_tables(n):
    rows = n // GRID_W
    row = jnp.repeat(jnp.arange(rows, dtype=F32), GRID_W)
    col = jnp.tile(jnp.arange(GRID_W, dtype=F32), rows)
    inv = 1.0 / (ROPE_BASE ** (jnp.arange(ROPE_PAIRS, dtype=F32) / ROPE_PAIRS))
    ang_r, ang_c = row[:, None] * inv, col[:, None] * inv
    zero = jnp.zeros_like(ang_r)
    reps = LANES // HEAD_DIM_A
    cos = jnp.tile(jnp.concatenate([jnp.cos(ang_r)] * 2 + [jnp.cos(ang_c)] * 2, axis=1), (1, reps))
    sa = jnp.tile(jnp.concatenate([-jnp.sin(ang_r), zero, -jnp.sin(ang_c), zero], axis=1), (1, reps))
    sb = jnp.tile(jnp.concatenate([zero, jnp.sin(ang_r), zero, jnp.sin(ang_c)], axis=1), (1, reps))
    return cos, sa, sb


def kernel(x_prompt, x_sample, cache_k0, cache_v0, c, c_ctx, w_mod0, b_mod0, w_in0, lambda_q1_0, lambda_k1_0, lambda_q2_0, lambda_k2_0, subln_g0, sgu_w0, sgu_b0, w_out0, ln_mix_g0, ln_mix_b0, w_ff1_0, w_ff2_0, ln_ff_g0, ln_ff_b0, w_mod1, b_mod1, w_in1, conv_w1, w_out1, ln_mix_g1, ln_mix_b1, w_ff1_1, w_ff2_1, ln_ff_g1, ln_ff_b1):
    batch, seq, d = x_prompt.shape
    dec_batch, dec_seq, _ = x_sample.shape
    assert (seq, dec_seq, d) == (SEQ, DEC_SEQ, D_MODEL) and 1 + dec_batch <= MOD_ROWS
    row = lambda v: v.reshape(1, -1)

    cvec = jnp.concatenate([c_ctx[None, :], c, jnp.zeros((MOD_ROWS - 1 - dec_batch, d), F32)], axis=0)
    mods0, mods1 = _modulations(cvec.T, w_mod0, row(b_mod0), w_mod1, row(b_mod1), 1 + dec_batch)
    mods0 = mods0.reshape(MOD_ROWS, 1, N_MOD)
    mods1 = mods1.reshape(MOD_ROWS, 1, N_MOD)
    ctx_row = lambda i: 0
    lat_row = lambda i: 1 + (i * TOK_TILE) // DEC_SEQ

    xp = x_prompt.reshape(batch * seq, d)
    xs = x_sample.reshape(dec_batch * dec_seq, d)

    w_in0_b = w_in0.astype(BF16)
    lam_params = jnp.stack([lambda_q1_0, lambda_k1_0, lambda_q2_0, lambda_k2_0])
    gain = row(subln_g0)
    qp, kp, vp, up, vcp = _in_proj0(xp, mods0, w_in0_b, ctx_row, None, F32)
    qs, ks, vs, us, vcs = _in_proj0(xs, mods0, w_in0_b, lat_row, _rope_tables(dec_seq), BF16)
    ap = _attn_ctx(qp, kp, vp, lam_params, gain)
    a_s = _attn_lat(qs, ks, vs, cache_k0.reshape(dec_batch * PAST_LEN, QK_WIDTH),
                    cache_v0.reshape(dec_batch * PAST_LEN, WIDTH_A), lam_params, gain)
    mix0_w = (sgu_w0.astype(BF16), sgu_b0.T, w_out0.astype(BF16), row(ln_mix_g0), row(ln_mix_b0))
    xp = _mix0(ap, up, vcp, xp, mods0, ctx_row, *mix0_w)
    xs = _mix0(a_s, us, vcs, xs, mods0, lat_row, *mix0_w)
    ffn0_w = (w_ff1_0.astype(BF16), w_ff2_0.astype(BF16), row(ln_ff_g0), row(ln_ff_b0))
    xp = _ffn(xp, mods0, ctx_row, *ffn0_w)
    xs = _ffn(xs, mods0, lat_row, *ffn0_w)

    mix1_w = (w_in1.astype(BF16), conv_w1, w_out1.astype(BF16), row(ln_mix_g1), row(ln_mix_b1))
    xp = _mix1(xp, mods1, ctx_row, *mix1_w, seq_len=seq)
    xs = _mix1(xs, mods1, lat_row, *mix1_w, seq_len=dec_seq)
    ffn1_w = (w_ff1_1.astype(BF16), w_ff2_1.astype(BF16), row(ln_ff_g1), row(ln_ff_b1))
    xp = _ffn(xp, mods1, ctx_row, *ffn1_w)
    xs = _ffn(xs, mods1, lat_row, *ffn1_w)

    return (xp.reshape(batch, seq, d), xs.reshape(dec_batch, dec_seq, d),
            kp.reshape(batch, seq, N_HEADS_A, 2, HEAD_DIM_A),
            vp.reshape(batch, seq, N_HEADS_A, V_DIM_A))
```

```python
import functools
import math

import jax
import jax.numpy as jnp
from jax import lax
from jax.experimental import pallas as pl
from jax.experimental.pallas import tpu as pltpu

F32 = jnp.float32
BF16 = jnp.bfloat16

D_MODEL = 1024
DEPTH = 2
SEQ = 256
DEC_SEQ = 4096
PAST_LEN = 256
GRID_W = 64
N_HEADS_A = 4
HEAD_DIM_A = 64
V_DIM_A = 2 * HEAD_DIM_A
QK_WIDTH = N_HEADS_A * 2 * HEAD_DIM_A
WIDTH_A = N_HEADS_A * V_DIM_A
N_GROUPS_B = 4
CHUNK = 128
GROUP_DIM_B = 128
WIDTH_B = N_GROUPS_B * GROUP_DIM_B
IN_WIDTH_0 = 2 * QK_WIDTH + WIDTH_A + 2 * WIDTH_B
WIDTH_C = D_MODEL
D_FF = 4 * D_MODEL
ROPE_BASE = 10000.0
ROPE_PAIRS = HEAD_DIM_A // 4
LN_EPS = 1e-5
ALPHA = (2 * DEPTH) ** 0.25
LAMBDA_INIT_0 = 0.8 - 0.6 * math.exp(-0.3 * 0)

LANES = 128
SUBLANES = 8
VMEM_LIMIT = 56 * 2 ** 20
N_MOD = 6 * D_MODEL
MOD_ROWS = 8
TOK_TILE = 512
ATT_Q_BLOCK = 128
ATT_KEY_CHUNK = 256
FF_CHUNK = 1024
MOD_TILE = 512


def _const_spec(shape):
    zeros = (0,) * len(shape)
    return pl.BlockSpec(shape, lambda *_: zeros, pipeline_mode=pl.Buffered(1))


def _params(*sem):
    return pltpu.CompilerParams(dimension_semantics=sem, vmem_limit_bytes=VMEM_LIMIT)


def _layer_norm(x, g, b):
    mu = jnp.mean(x, axis=-1, keepdims=True)
    xc = x - mu
    var = jnp.mean(xc * xc, axis=-1, keepdims=True)
    return xc * lax.rsqrt(var + LN_EPS) * g + b


def _mod(mod_ref, idx):
    return mod_ref[:, idx * D_MODEL:(idx + 1) * D_MODEL]


def _mod_kernel(ct_ref, w0_ref, b0_ref, w1_ref, b1_ref, o0_ref, o1_ref, *, rows):
    ct = ct_ref[...]
    s = ct * jax.nn.sigmoid(ct)
    for w_ref, b_ref, o_ref in ((w0_ref, b0_ref, o0_ref), (w1_ref, b1_ref, o1_ref)):
        w = w_ref[...]
        b = b_ref[...]
        for r in range(rows):
            o_ref[r:r + 1, :] = jnp.sum(w * s[:, r:r + 1], axis=0, keepdims=True) + b
        o_ref[rows:, :] = jnp.zeros((MOD_ROWS - rows, w.shape[1]), F32)


def _modulations(ct, w0, b0, w1, b1, rows):
    w_spec = pl.BlockSpec((D_MODEL, MOD_TILE), lambda j: (0, j))
    v_spec = pl.BlockSpec((1, MOD_TILE), lambda j: (0, j))
    o_spec = pl.BlockSpec((MOD_ROWS, MOD_TILE), lambda j: (0, j))
    out = jax.ShapeDtypeStruct((MOD_ROWS, N_MOD), F32)
    return pl.pallas_call(
        functools.partial(_mod_kernel, rows=rows),
        grid=(N_MOD // MOD_TILE,),
        in_specs=[_const_spec((D_MODEL, MOD_ROWS)), w_spec, v_spec, w_spec, v_spec],
        out_specs=[o_spec, o_spec],
        out_shape=[out, out],
        compiler_params=_params("parallel"),
        name="adaln_mod",
    )(ct, w0, b0, w1, b1)


def _mod_spec(row_fn):
    return pl.BlockSpec((None, 1, N_MOD), lambda i: (row_fn(i), 0, 0))


def _in0_kernel(*refs, rope):
    if rope:
        (x_ref, mod_ref, w_ref, cos_ref, sa_ref, sb_ref,
         q_ref, k_ref, v_ref, u_ref, vc_ref) = refs
    else:
        x_ref, mod_ref, w_ref, q_ref, k_ref, v_ref, u_ref, vc_ref = refs
    h = (x_ref[...] * (1.0 + _mod(mod_ref, 1)) + _mod(mod_ref, 0)).astype(BF16)
    y = jnp.dot(h, w_ref[...], preferred_element_type=F32)
    if rope:
        cos, sa, sb = cos_ref[...], sa_ref[...], sb_ref[...]
    for j in range(QK_WIDTH // LANES):
        lo, hi = j * LANES, (j + 1) * LANES
        qj = y[:, lo:hi]
        kj = y[:, QK_WIDTH + lo:QK_WIDTH + hi]
        if rope:
            qj = (qj * cos + pltpu.roll(qj, LANES - ROPE_PAIRS, 1) * sa
                  + pltpu.roll(qj, ROPE_PAIRS, 1) * sb)
            kj = (kj * cos + pltpu.roll(kj, LANES - ROPE_PAIRS, 1) * sa
                  + pltpu.roll(kj, ROPE_PAIRS, 1) * sb)
        q_ref[:, lo:hi] = (qj * (HEAD_DIM_A ** -0.5)).astype(q_ref.dtype)
        k_ref[:, lo:hi] = kj.astype(k_ref.dtype)
    off = 2 * QK_WIDTH
    v = y[:, off:off + WIDTH_A]
    v_ref[...] = (v.T if rope else v).astype(v_ref.dtype)
    off += WIDTH_A
    u_ref[...] = y[:, off:off + WIDTH_B]
    off += WIDTH_B
    for g in range(N_GROUPS_B):
        gg = y[:, off + g * GROUP_DIM_B:off + (g + 1) * GROUP_DIM_B]
        mu = jnp.mean(gg, axis=-1, keepdims=True)
        gc = gg - mu
        var = jnp.mean(gc * gc, axis=-1, keepdims=True)
        vc_ref[:, g * GROUP_DIM_B:(g + 1) * GROUP_DIM_B] = (
            gc * lax.rsqrt(var + LN_EPS)).astype(vc_ref.dtype)


def _in_proj0(x, mods, w_in, row_fn, rope_tabs, kv_dtype):
    n = x.shape[0]
    t = TOK_TILE
    tok = lambda width: pl.BlockSpec((t, width), lambda i: (i, 0))
    in_specs = [tok(D_MODEL), _mod_spec(row_fn), _const_spec((D_MODEL, IN_WIDTH_0))]
    args = [x, mods, w_in]
    v_spec = tok(WIDTH_A)
    v_shape = (n, WIDTH_A)
    if rope_tabs is not None:
        blocks_per_seq = DEC_SEQ // t
        tab = pl.BlockSpec((t, LANES), lambda i: (i % blocks_per_seq, 0))
        in_specs += [tab, tab, tab]
        args += list(rope_tabs)
        v_spec = pl.BlockSpec((WIDTH_A, t), lambda i: (i // blocks_per_seq, i % blocks_per_seq))
        v_shape = (n // DEC_SEQ * WIDTH_A, DEC_SEQ)
    out_shape = [jax.ShapeDtypeStruct((n, QK_WIDTH), BF16),
                 jax.ShapeDtypeStruct((n, QK_WIDTH), kv_dtype),
                 jax.ShapeDtypeStruct(v_shape, kv_dtype),
                 jax.ShapeDtypeStruct((n, WIDTH_B), F32),
                 jax.ShapeDtypeStruct((n, WIDTH_B), BF16)]
    return pl.pallas_call(
        functools.partial(_in0_kernel, rope=rope_tabs is not None),
        grid=(n // t,),
        in_specs=in_specs,
        out_specs=[tok(QK_WIDTH), tok(QK_WIDTH), v_spec, tok(WIDTH_B), tok(WIDTH_B)],
        out_shape=out_shape,
        compiler_params=_params("parallel"),
        name="in_proj0",
    )(*args)


def _lambda(lp_ref):
    lp = lp_ref[...]
    a = jnp.sum(lp[0:1] * lp[1:2], axis=1, keepdims=True)
    b = jnp.sum(lp[2:3] * lp[3:4], axis=1, keepdims=True)
    return jnp.exp(a) - jnp.exp(b) + LAMBDA_INIT_0


def _stack_halves(qh):
    lane = lax.broadcasted_iota(jnp.int32, qh.shape, 1)
    zero = jnp.zeros_like(qh)
    return jnp.concatenate([jnp.where(lane < HEAD_DIM_A, qh, zero),
                            jnp.where(lane >= HEAD_DIM_A, qh, zero)], axis=0)


def _scores(qq, k):
    return lax.dot_general(qq, k, (((1,), (1,)), ((), ())), preferred_element_type=F32)


def _head_out(pv, l, lam, gain, rows):
    o = pv[:rows] / l[:rows] - lam * (pv[rows:] / l[rows:])
    ms = jnp.mean(o * o, axis=-1, keepdims=True)
    return o * lax.rsqrt(ms + LN_EPS) * gain * (1.0 - LAMBDA_INIT_0)


def _attn_ctx_kernel(q_ref, k_ref, v_ref, lp_ref, gain_ref, o_ref):
    lam = _lambda(lp_ref)
    gain = gain_ref[...]
    rows = q_ref.shape[0]
    for h in range(N_HEADS_A):
        lo, hi = h * LANES, (h + 1) * LANES
        qq = _stack_halves(q_ref[:, lo:hi])
        s = _scores(qq, k_ref[:, lo:hi].astype(BF16))
        p = jnp.exp(s - jnp.max(s, axis=-1, keepdims=True))
        l = jnp.sum(p, axis=-1, keepdims=True)
        pv = jnp.dot(p.astype(BF16), v_ref[:, lo:hi].astype(BF16),
                     preferred_element_type=F32)
        o_ref[:, lo:hi] = _head_out(pv, l, lam, gain, rows).astype(o_ref.dtype)


def _attn_ctx(q, k, v, lam_params, gain):
    n = q.shape[0]
    blk = lambda: pl.BlockSpec((SEQ, QK_WIDTH), lambda b: (b, 0))
    return pl.pallas_call(
        _attn_ctx_kernel,
        grid=(n // SEQ,),
        in_specs=[blk(), blk(), blk(), _const_spec((4, HEAD_DIM_A)), _const_spec((1, V_DIM_A))],
        out_specs=blk(),
        out_shape=jax.ShapeDtypeStruct((n, WIDTH_A), BF16),
        compiler_params=_params("parallel"),
        name="attn_ctx",
    )(q, k, v, lam_params, gain)


def _attn_lat_kernel(q_ref, k_ref, vt_ref, ck_ref, cvt_ref, lp_ref, gain_ref, o_ref):
    lam = _lambda(lp_ref)
    gain = gain_ref[...]
    rows = q_ref.shape[0]
    chunks = [(None, PAST_LEN)] + [(s, ATT_KEY_CHUNK) for s in range(0, DEC_SEQ, ATT_KEY_CHUNK)]

    def fold(x, op):
        return op(x.reshape(x.shape[0] // SUBLANES, SUBLANES, x.shape[1]), axis=0)

    def score_steps(h, state):
        lo, hi = h * LANES, (h + 1) * LANES
        qq = _stack_halves(q_ref[:, lo:hi])
        state["st"] = []
        m = None
        for start, size in chunks:
            keys = (ck_ref[:, lo:hi].astype(BF16) if start is None
                    else k_ref[start:start + size, lo:hi])
            st = _scores(keys, qq)
            state["st"].append(st)
            cm = fold(st, jnp.max)
            m = cm if m is None else jnp.maximum(m, cm)
            state["m"] = m
            yield

    def value_steps(h, state):
        lo, hi = h * LANES, (h + 1) * LANES
        m = jnp.max(state["m"], axis=0, keepdims=True)
        l = None
        pvt = None
        for (start, size), st in zip(chunks, state["st"]):
            pt = jnp.exp(st - m)
            cl = fold(pt, jnp.sum)
            l = cl if l is None else l + cl
            vals_t = cvt_ref[lo:hi, :] if start is None else vt_ref[lo:hi, start:start + size]
            part = jnp.dot(vals_t, pt.astype(BF16), preferred_element_type=F32)
            pvt = part if pvt is None else pvt + part
            yield
        l = jnp.sum(l, axis=0, keepdims=True)
        ot = pvt[:, :rows] / l[:, :rows] - lam * (pvt[:, rows:] / l[:, rows:])
        ms = jnp.mean(ot * ot, axis=0, keepdims=True)
        on = ot * lax.rsqrt(ms + LN_EPS) * gain * (1.0 - LAMBDA_INIT_0)
        o_ref[:, lo:hi] = on.T.astype(o_ref.dtype)
        yield

    states = [dict() for _ in range(N_HEADS_A)]
    for _ in score_steps(0, states[0]):
        pass
    for h in range(N_HEADS_A):
        nxt = score_steps(h + 1, states[h + 1]) if h + 1 < N_HEADS_A else iter(())
        for _ in value_steps(h, states[h]):
            next(nxt, None)
        for _ in nxt:
            pass


def _attn_lat(q, k, vt, cache_k, cache_vt, lam_params, gain_col):
    n = q.shape[0]
    nb = n // DEC_SEQ
    qb = DEC_SEQ // ATT_Q_BLOCK
    q_spec = pl.BlockSpec((ATT_Q_BLOCK, QK_WIDTH), lambda b, j: (b * qb + j, 0))
    k_spec = pl.BlockSpec((DEC_SEQ, QK_WIDTH), lambda b, j: (b, 0))
    vt_spec = pl.BlockSpec((WIDTH_A, DEC_SEQ), lambda b, j: (b, 0))
    ck_spec = pl.BlockSpec((PAST_LEN, QK_WIDTH), lambda b, j: (b, 0))
    cvt_spec = pl.BlockSpec((WIDTH_A, PAST_LEN), lambda b, j: (b, 0))
    return pl.pallas_call(
        _attn_lat_kernel,
        grid=(nb, qb),
        in_specs=[q_spec, k_spec, vt_spec, ck_spec, cvt_spec,
                  _const_spec((4, HEAD_DIM_A)), _const_spec((V_DIM_A, 1))],
        out_specs=q_spec,
        out_shape=jax.ShapeDtypeStruct((n, WIDTH_A), BF16),
        compiler_params=_params("parallel", "arbitrary"),
        name="attn_lat",
    )(q, k, vt, cache_k, cache_vt, lam_params, gain_col)


def _mix0_kernel(a_ref, u_ref, vc_ref, x_ref, mod_ref, sw_ref, sb_ref, wo_ref,
                 g_ref, b_ref, o_ref, s_scr):
    t = x_ref.shape[0]
    for g in range(N_GROUPS_B):
        lo, hi = g * GROUP_DIM_B, (g + 1) * GROUP_DIM_B
        w = sw_ref[g]
        bias = jnp.broadcast_to(sb_ref[:, g:g + 1], (CHUNK, GROUP_DIM_B))
        for c in range(t // CHUNK):
            r0, r1 = c * CHUNK, (c + 1) * CHUNK
            mixed = jnp.dot(w, vc_ref[r0:r1, lo:hi], preferred_element_type=F32) + bias
            s_scr[r0:r1, lo:hi] = (u_ref[r0:r1, lo:hi] * mixed).astype(BF16)
    out = (jnp.dot(a_ref[...], wo_ref[0:WIDTH_A, :], preferred_element_type=F32)
           + jnp.dot(s_scr[...], wo_ref[WIDTH_A:, :], preferred_element_type=F32))
    o_ref[...] = _layer_norm(ALPHA * x_ref[...] + _mod(mod_ref, 2) * out,
                             g_ref[...], b_ref[...])


def _mix0(attn, u, vc, x, mods, row_fn, sgu_w, sgu_bt, w_out, g, b):
    n = x.shape[0]
    t = TOK_TILE
    tok = lambda width: pl.BlockSpec((t, width), lambda i: (i, 0))
    return pl.pallas_call(
        _mix0_kernel,
        grid=(n // t,),
        in_specs=[tok(WIDTH_A), tok(WIDTH_B), tok(WIDTH_B), tok(D_MODEL), _mod_spec(row_fn),
                  _const_spec((N_GROUPS_B, CHUNK, CHUNK)), _const_spec((CHUNK, N_GROUPS_B)),
                  _const_spec((WIDTH_A + WIDTH_B, D_MODEL)),
                  _const_spec((1, D_MODEL)), _const_spec((1, D_MODEL))],
        out_specs=tok(D_MODEL),
        out_shape=jax.ShapeDtypeStruct((n, D_MODEL), F32),
        scratch_shapes=[pltpu.VMEM((t, WIDTH_B), BF16)],
        compiler_params=_params("parallel"),
        name="mix0",
    )(attn, u, vc, x, mods, sgu_w, sgu_bt, w_out, g, b)


def _mix1_kernel(xp_ref, x_ref, xn_ref, mod_ref, wi_ref, cw_ref, wo_ref, g_ref, b_ref,
                 o_ref, z_scr, *, seq_len):
    t = x_ref.shape[0]
    halo = SUBLANES
    xin = jnp.concatenate([xp_ref[...], x_ref[...], xn_ref[...]], axis=0)
    h = (xin * (1.0 + _mod(mod_ref, 1)) + _mod(mod_ref, 0)).astype(BF16)
    y = jnp.dot(h, wi_ref[...], preferred_element_type=F32)
    z_scr[...] = y[:, WIDTH_C:2 * WIDTH_C] * y[:, 2 * WIDTH_C:]
    pos = (pl.program_id(0) * t + lax.broadcasted_iota(jnp.int32, (t, 1), 0)) % seq_len
    z_prev = jnp.where(pos != 0, z_scr[halo - 1:halo - 1 + t, :], 0.0)
    z_next = jnp.where(pos != seq_len - 1, z_scr[halo + 1:halo + 1 + t, :], 0.0)
    cw = cw_ref[...]
    conv = z_prev * cw[0:1] + z_scr[halo:halo + t, :] * cw[1:2] + z_next * cw[2:3]
    gated = (y[halo:halo + t, 0:WIDTH_C] * conv).astype(BF16)
    out = jnp.dot(gated, wo_ref[...], preferred_element_type=F32)
    o_ref[...] = _layer_norm(ALPHA * x_ref[...] + _mod(mod_ref, 2) * out,
                             g_ref[...], b_ref[...])


def _mix1(x, mods, row_fn, w_in, conv_w, w_out, g, b, seq_len):
    n = x.shape[0]
    t = TOK_TILE
    tiles_per_block = t // SUBLANES
    last = n // SUBLANES - 1
    tok = pl.BlockSpec((t, D_MODEL), lambda i: (i, 0))
    prev = pl.BlockSpec((SUBLANES, D_MODEL),
                        lambda i: (jnp.maximum(i * tiles_per_block - 1, 0), 0))
    nxt = pl.BlockSpec((SUBLANES, D_MODEL),
                       lambda i: (jnp.minimum((i + 1) * tiles_per_block, last), 0))
    return pl.pallas_call(
        functools.partial(_mix1_kernel, seq_len=seq_len),
        grid=(n // t,),
        in_specs=[prev, tok, nxt, _mod_spec(row_fn),
                  _const_spec((D_MODEL, 3 * WIDTH_C)), _const_spec((3, WIDTH_C)),
                  _const_spec((WIDTH_C, D_MODEL)),
                  _const_spec((1, D_MODEL)), _const_spec((1, D_MODEL))],
        out_specs=tok,
        out_shape=jax.ShapeDtypeStruct((n, D_MODEL), F32),
        scratch_shapes=[pltpu.VMEM((t + 2 * SUBLANES, WIDTH_C), F32)],
        compiler_params=_params("parallel"),
        name="mix1",
    )(x, x, x, mods, w_in, conv_w, w_out, g, b)


def _ffn_kernel(x_ref, mod_ref, w1_ref, w2_ref, g_ref, b_ref, o_ref):
    x = x_ref[...]
    h = (x * (1.0 + _mod(mod_ref, 4)) + _mod(mod_ref, 3)).astype(BF16)
    f = None
    for j in range(D_FF // FF_CHUNK):
        lo, hi = j * FF_CHUNK, (j + 1) * FF_CHUNK
        hid = jnp.dot(h, w1_ref[:, lo:hi], preferred_element_type=F32)
        hid = jnp.square(jnp.maximum(hid, 0.0)).astype(BF16)
        part = jnp.dot(hid, w2_ref[lo:hi, :], preferred_element_type=F32)
        f = part if f is None else f + part
    o_ref[...] = _layer_norm(ALPHA * x + _mod(mod_ref, 5) * f, g_ref[...], b_ref[...])


def _ffn(x, mods, row_fn, w1, w2, g, b):
    n = x.shape[0]
    t = TOK_TILE
    tok = pl.BlockSpec((t, D_MODEL), lambda i: (i, 0))
    return pl.pallas_call(
        _ffn_kernel,
        grid=(n // t,),
        in_specs=[tok, _mod_spec(row_fn), _const_spec((D_MODEL, D_FF)),
                  _const_spec((D_FF, D_MODEL)), _const_spec((1, D_MODEL)),
                  _const_spec((1, D_MODEL))],
        out_specs=tok,
        out_shape=jax.ShapeDtypeStruct((n, D_MODEL), F32),
        compiler_params=_params("parallel"),
        name="ffn",
    )(x, mods, w1, w2, g, b)


def _rope_tables(n):
    rows = n // GRID_W
    row = jnp.repeat(jnp.arange(rows, dtype=F32), GRID_W)
    col = jnp.tile(jnp.arange(GRID_W, dtype=F32), rows)
    inv = 1.0 / (ROPE_BASE ** (jnp.arange(ROPE_PAIRS, dtype=F32) / ROPE_PAIRS))
    ang_r, ang_c = row[:, None] * inv, col[:, None] * inv
    zero = jnp.zeros_like(ang_r)
    reps = LANES // HEAD_DIM_A
    cos = jnp.tile(jnp.concatenate([jnp.cos(ang_r)] * 2 + [jnp.cos(ang_c)] * 2, axis=1), (1, reps))
    sa = jnp.tile(jnp.concatenate([-jnp.sin(ang_r), zero, -jnp.sin(ang_c), zero], axis=1), (1, reps))
    sb = jnp.tile(jnp.concatenate([zero, jnp.sin(ang_r), zero, jnp.sin(ang_c)], axis=1), (1, reps))
    return cos, sa, sb


def kernel(x_prompt, x_sample, cache_k0, cache_v0, c, c_ctx, w_mod0, b_mod0, w_in0, lambda_q1_0, lambda_k1_0, lambda_q2_0, lambda_k2_0, subln_g0, sgu_w0, sgu_b0, w_out0, ln_mix_g0, ln_mix_b0, w_ff1_0, w_ff2_0, ln_ff_g0, ln_ff_b0, w_mod1, b_mod1, w_in1, conv_w1, w_out1, ln_mix_g1, ln_mix_b1, w_ff1_1, w_ff2_1, ln_ff_g1, ln_ff_b1):
    batch, seq, d = x_prompt.shape
    dec_batch, dec_seq, _ = x_sample.shape
    assert (seq, dec_seq, d) == (SEQ, DEC_SEQ, D_MODEL) and 1 + dec_batch <= MOD_ROWS
    row = lambda v: v.reshape(1, -1)

    cvec = jnp.concatenate([c_ctx[None, :], c, jnp.zeros((MOD_ROWS - 1 - dec_batch, d), F32)], axis=0)
    mods0, mods1 = _modulations(cvec.T, w_mod0, row(b_mod0), w_mod1, row(b_mod1), 1 + dec_batch)
    mods0 = mods0.reshape(MOD_ROWS, 1, N_MOD)
    mods1 = mods1.reshape(MOD_ROWS, 1, N_MOD)
    ctx_row = lambda i: 0
    lat_row = lambda i: 1 + (i * TOK_TILE) // DEC_SEQ

    xp = x_prompt.reshape(batch * seq, d)
    xs = x_sample.reshape(dec_batch * dec_seq, d)

    w_in0_b = w_in0.astype(BF16)
    lam_params = jnp.stack([lambda_q1_0, lambda_k1_0, lambda_q2_0, lambda_k2_0])
    gain = row(subln_g0)
    qp, kp, vp, up, vcp = _in_proj0(xp, mods0, w_in0_b, ctx_row, None, F32)
    qs, ks, vs, us, vcs = _in_proj0(xs, mods0, w_in0_b, lat_row, _rope_tables(dec_seq), BF16)
    ap = _attn_ctx(qp, kp, vp, lam_params, gain)
    cache_vt = jnp.swapaxes(cache_v0.reshape(dec_batch, PAST_LEN, WIDTH_A), 1, 2)
    a_s = _attn_lat(qs, ks, vs, cache_k0.reshape(dec_batch * PAST_LEN, QK_WIDTH),
                    cache_vt.reshape(dec_batch * WIDTH_A, PAST_LEN).astype(BF16),
                    lam_params, gain.reshape(V_DIM_A, 1))
    mix0_w = (sgu_w0.astype(BF16), sgu_b0.T, w_out0.astype(BF16), row(ln_mix_g0), row(ln_mix_b0))
    xp = _mix0(ap, up, vcp, xp, mods0, ctx_row, *mix0_w)
    xs = _mix0(a_s, us, vcs, xs, mods0, lat_row, *mix0_w)
    ffn0_w = (w_ff1_0.astype(BF16), w_ff2_0.astype(BF16), row(ln_ff_g0), row(ln_ff_b0))
    xp = _ffn(xp, mods0, ctx_row, *ffn0_w)
    xs = _ffn(xs, mods0, lat_row, *ffn0_w)

    mix1_w = (w_in1.astype(BF16), conv_w1, w_out1.astype(BF16), row(ln_mix_g1), row(ln_mix_b1))
    xp = _mix1(xp, mods1, ctx_row, *mix1_w, seq_len=seq)
    xs = _mix1(xs, mods1, lat_row, *mix1_w, seq_len=dec_seq)
    ffn1_w = (w_ff1_1.astype(BF16), w_ff2_1.astype(BF16), row(ln_ff_g1), row(ln_ff_b1))
    xp = _ffn(xp, mods1, ctx_row, *ffn1_w)
    xs = _ffn(xs, mods1, lat_row, *ffn1_w)

    return (xp.reshape(batch, seq, d), xs.reshape(dec_batch, dec_seq, d),
            kp.reshape(batch, seq, N_HEADS_A, 2, HEAD_DIM_A),
            vp.reshape(batch, seq, N_HEADS_A, V_DIM_A))
```

```python
import functools
import math

import jax
import jax.numpy as jnp
from jax import lax
from jax.experimental import pallas as pl
from jax.experimental.pallas import tpu as pltpu

F32 = jnp.float32
BF16 = jnp.bfloat16

D_MODEL = 1024
DEPTH = 2
SEQ = 256
DEC_SEQ = 4096
PAST_LEN = 256
GRID_W = 64
N_HEADS_A = 4
HEAD_DIM_A = 64
V_DIM_A = 2 * HEAD_DIM_A
QK_WIDTH = N_HEADS_A * 2 * HEAD_DIM_A
WIDTH_A = N_HEADS_A * V_DIM_A
N_GROUPS_B = 4
CHUNK = 128
GROUP_DIM_B = 128
WIDTH_B = N_GROUPS_B * GROUP_DIM_B
IN_WIDTH_0 = 2 * QK_WIDTH + WIDTH_A + 2 * WIDTH_B
WIDTH_C = D_MODEL
D_FF = 4 * D_MODEL
ROPE_BASE = 10000.0
ROPE_PAIRS = HEAD_DIM_A // 4
LN_EPS = 1e-5
ALPHA = (2 * DEPTH) ** 0.25
LAMBDA_INIT_0 = 0.8 - 0.6 * math.exp(-0.3 * 0)
Q_SCALE = HEAD_DIM_A ** -0.5 * math.log2(math.e)

LANES = 128
SUBLANES = 8
BF16_SUBLANES = 16
VMEM_LIMIT = 56 * 2 ** 20
N_MOD = 6 * D_MODEL
MOD_ROWS = 8
TOK_TILE = 512
ATT_Q_BLOCK = 256
ATT_SUB_BLOCK = 128
ATT_KEY_CHUNK = 256
ATT_LOOKAHEAD = 5
FF_CHUNK = 1024
MOD_TILE = 512


def _const_spec(shape):
    zeros = (0,) * len(shape)
    return pl.BlockSpec(shape, lambda *_: zeros, pipeline_mode=pl.Buffered(1))


def _params(*sem):
    return pltpu.CompilerParams(dimension_semantics=sem, vmem_limit_bytes=VMEM_LIMIT)


def _layer_norm(x, g, b):
    mu = jnp.mean(x, axis=-1, keepdims=True)
    xc = x - mu
    var = jnp.mean(xc * xc, axis=-1, keepdims=True)
    return xc * lax.rsqrt(var + LN_EPS) * g + b


def _mod(mod_ref, idx):
    return mod_ref[:, idx * D_MODEL:(idx + 1) * D_MODEL]


def _mod_kernel(ct_ref, w0_ref, b0_ref, w1_ref, b1_ref, o0_ref, o1_ref, *, rows):
    ct = ct_ref[...]
    s = ct * jax.nn.sigmoid(ct)
    for w_ref, b_ref, o_ref in ((w0_ref, b0_ref, o0_ref), (w1_ref, b1_ref, o1_ref)):
        w = w_ref[...]
        b = b_ref[...]
        for r in range(rows):
            o_ref[r:r + 1, :] = jnp.sum(w * s[:, r:r + 1], axis=0, keepdims=True) + b
        o_ref[rows:, :] = jnp.zeros((MOD_ROWS - rows, w.shape[1]), F32)


def _modulations(ct, w0, b0, w1, b1, rows):
    w_spec = pl.BlockSpec((D_MODEL, MOD_TILE), lambda j: (0, j))
    v_spec = pl.BlockSpec((1, MOD_TILE), lambda j: (0, j))
    o_spec = pl.BlockSpec((MOD_ROWS, MOD_TILE), lambda j: (0, j))
    out = jax.ShapeDtypeStruct((MOD_ROWS, N_MOD), F32)
    return pl.pallas_call(
        functools.partial(_mod_kernel, rows=rows),
        grid=(N_MOD // MOD_TILE,),
        in_specs=[_const_spec((D_MODEL, MOD_ROWS)), w_spec, v_spec, w_spec, v_spec],
        out_specs=[o_spec, o_spec],
        out_shape=[out, out],
        compiler_params=_params("parallel"),
        name="adaln_mod",
    )(ct, w0, b0, w1, b1)


def _mod_spec(row_fn):
    return pl.BlockSpec((None, 1, N_MOD), lambda i: (row_fn(i), 0, 0))


def _in0_kernel(*refs, rope):
    if rope:
        (x_ref, mod_ref, w_ref, cos_ref, sa_ref, sb_ref,
         q_ref, k_ref, v_ref, u_ref, vc_ref) = refs
    else:
        x_ref, mod_ref, w_ref, q_ref, k_ref, v_ref, u_ref, vc_ref = refs
    h = (x_ref[...] * (1.0 + _mod(mod_ref, 1)) + _mod(mod_ref, 0)).astype(BF16)
    y = jnp.dot(h, w_ref[...], preferred_element_type=F32)
    if rope:
        cos, sa, sb = cos_ref[...], sa_ref[...], sb_ref[...]
    for j in range(QK_WIDTH // LANES):
        lo, hi = j * LANES, (j + 1) * LANES
        qj = y[:, lo:hi]
        kj = y[:, QK_WIDTH + lo:QK_WIDTH + hi]
        if rope:
            qj = (qj * cos + pltpu.roll(qj, LANES - ROPE_PAIRS, 1) * sa
                  + pltpu.roll(qj, ROPE_PAIRS, 1) * sb)
            kj = (kj * cos + pltpu.roll(kj, LANES - ROPE_PAIRS, 1) * sa
                  + pltpu.roll(kj, ROPE_PAIRS, 1) * sb)
        q_ref[:, lo:hi] = (qj * Q_SCALE).astype(q_ref.dtype)
        k_ref[:, lo:hi] = kj.astype(k_ref.dtype)
    off = 2 * QK_WIDTH
    v = y[:, off:off + WIDTH_A]
    v_ref[...] = (v.T if rope else v).astype(v_ref.dtype)
    off += WIDTH_A
    u_ref[...] = y[:, off:off + WIDTH_B]
    off += WIDTH_B
    for g in range(N_GROUPS_B):
        gg = y[:, off + g * GROUP_DIM_B:off + (g + 1) * GROUP_DIM_B]
        mu = jnp.mean(gg, axis=-1, keepdims=True)
        gc = gg - mu
        var = jnp.mean(gc * gc, axis=-1, keepdims=True)
        vc_ref[:, g * GROUP_DIM_B:(g + 1) * GROUP_DIM_B] = (
            gc * lax.rsqrt(var + LN_EPS)).astype(vc_ref.dtype)


def _in_proj0(x, mods, w_in, row_fn, rope_tabs, kv_dtype):
    n = x.shape[0]
    t = TOK_TILE
    tok = lambda width: pl.BlockSpec((t, width), lambda i: (i, 0))
    in_specs = [tok(D_MODEL), _mod_spec(row_fn), _const_spec((D_MODEL, IN_WIDTH_0))]
    args = [x, mods, w_in]
    v_spec = tok(WIDTH_A)
    v_shape = (n, WIDTH_A)
    if rope_tabs is not None:
        blocks_per_seq = DEC_SEQ // t
        tab = pl.BlockSpec((t, LANES), lambda i: (i % blocks_per_seq, 0))
        in_specs += [tab, tab, tab]
        args += list(rope_tabs)
        v_spec = pl.BlockSpec((WIDTH_A, t), lambda i: (i // blocks_per_seq, i % blocks_per_seq))
        v_shape = (n // DEC_SEQ * WIDTH_A, DEC_SEQ)
    out_shape = [jax.ShapeDtypeStruct((n, QK_WIDTH), BF16),
                 jax.ShapeDtypeStruct((n, QK_WIDTH), kv_dtype),
                 jax.ShapeDtypeStruct(v_shape, kv_dtype),
                 jax.ShapeDtypeStruct((n, WIDTH_B), F32),
                 jax.ShapeDtypeStruct((n, WIDTH_B), BF16)]
    return pl.pallas_call(
        functools.partial(_in0_kernel, rope=rope_tabs is not None),
        grid=(n // t,),
        in_specs=in_specs,
        out_specs=[tok(QK_WIDTH), tok(QK_WIDTH), v_spec, tok(WIDTH_B), tok(WIDTH_B)],
        out_shape=out_shape,
        compiler_params=_params("parallel"),
        name="in_proj0",
    )(*args)


def _lambda(lp_ref):
    lp = lp_ref[...]
    a = jnp.sum(lp[0:1] * lp[1:2], axis=1, keepdims=True)
    b = jnp.sum(lp[2:3] * lp[3:4], axis=1, keepdims=True)
    return jnp.exp(a) - jnp.exp(b) + LAMBDA_INIT_0


def _stack_halves(qh):
    lane = lax.broadcasted_iota(jnp.int32, qh.shape, 1)
    zero = jnp.zeros_like(qh)
    return jnp.concatenate([jnp.where(lane < HEAD_DIM_A, qh, zero),
                            jnp.where(lane >= HEAD_DIM_A, qh, zero)], axis=0)


def _scores(qq, k):
    return lax.dot_general(qq, k, (((1,), (1,)), ((), ())), preferred_element_type=F32)


def _head_out(pv, l, lam, gain, rows):
    o = pv[:rows] / l[:rows] - lam * (pv[rows:] / l[rows:])
    ms = jnp.mean(o * o, axis=-1, keepdims=True)
    return o * lax.rsqrt(ms + LN_EPS) * gain * (1.0 - LAMBDA_INIT_0)


def _attn_ctx_kernel(q_ref, k_ref, v_ref, lp_ref, gain_ref, o_ref):
    lam = _lambda(lp_ref)
    gain = gain_ref[...]
    rows = q_ref.shape[0]
    for h in range(N_HEADS_A):
        lo, hi = h * LANES, (h + 1) * LANES
        qq = _stack_halves(q_ref[:, lo:hi])
        s = _scores(qq, k_ref[:, lo:hi].astype(BF16))
        p = jnp.exp2(s - jnp.max(s, axis=-1, keepdims=True))
        l = jnp.sum(p, axis=-1, keepdims=True)
        pv = jnp.dot(p.astype(BF16), v_ref[:, lo:hi].astype(BF16),
                     preferred_element_type=F32)
        o_ref[:, lo:hi] = _head_out(pv, l, lam, gain, rows).astype(o_ref.dtype)


def _attn_ctx(q, k, v, lam_params, gain):
    n = q.shape[0]
    blk = lambda: pl.BlockSpec((SEQ, QK_WIDTH), lambda b: (b, 0))
    return pl.pallas_call(
        _attn_ctx_kernel,
        grid=(n // SEQ,),
        in_specs=[blk(), blk(), blk(), _const_spec((4, HEAD_DIM_A)), _const_spec((1, V_DIM_A))],
        out_specs=blk(),
        out_shape=jax.ShapeDtypeStruct((n, WIDTH_A), BF16),
        compiler_params=_params("parallel"),
        name="attn_ctx",
    )(q, k, v, lam_params, gain)


def _attn_lat_kernel(q_ref, k_ref, vt_ref, ck_ref, cvt_ref, lp_ref, gain_ref, o_ref):
    lam = _lambda(lp_ref)
    gain = gain_ref[...]
    rows = ATT_SUB_BLOCK
    chunks = [(None, PAST_LEN)] + [(s, ATT_KEY_CHUNK) for s in range(0, DEC_SEQ, ATT_KEY_CHUNK)]
    ones = jnp.ones((BF16_SUBLANES, ATT_KEY_CHUNK), BF16)
    units = [(sub, h, c) for sub in range(q_ref.shape[0] // rows)
             for h in range(N_HEADS_A) for c in range(len(chunks))]
    qq_of = {}
    scores = {}

    def issue_scores(unit):
        sub, h, c = unit
        lo, hi = h * LANES, (h + 1) * LANES
        if c == 0:
            qq_of[sub, h] = _stack_halves(q_ref[sub * rows:(sub + 1) * rows, lo:hi])
        start, size = chunks[c]
        keys = ck_ref[:, lo:hi].astype(BF16) if start is None else k_ref[start:start + size, lo:hi]
        scores[unit] = _scores(keys, qq_of[sub, h])

    for unit in units[:ATT_LOOKAHEAD]:
        issue_scores(unit)
    m = acc = None
    for i, unit in enumerate(units):
        if i + ATT_LOOKAHEAD < len(units):
            issue_scores(units[i + ATT_LOOKAHEAD])
        sub, h, c = unit
        lo, hi = h * LANES, (h + 1) * LANES
        start, size = chunks[c]
        vals_t = cvt_ref[lo:hi, :] if start is None else vt_ref[lo:hi, start:start + size]
        st = scores.pop(unit)
        cm = jnp.max(st, axis=0, keepdims=True)
        m_new = cm if c == 0 else jnp.maximum(m, cm)
        pt = jnp.exp2(st - m_new).astype(BF16)
        part = jnp.dot(jnp.concatenate([vals_t, ones[:, :size]], axis=0), pt,
                       preferred_element_type=F32)
        acc = part if c == 0 else acc * jnp.exp2(m - m_new) + part
        m = m_new
        if c == len(chunks) - 1:
            pvt = acc[:V_DIM_A]
            l = acc[V_DIM_A:V_DIM_A + 1]
            ot = pvt[:, :rows] / l[:, :rows] - lam * (pvt[:, rows:] / l[:, rows:])
            ms = jnp.mean(ot * ot, axis=0, keepdims=True)
            on = ot * lax.rsqrt(ms + LN_EPS) * gain * (1.0 - LAMBDA_INIT_0)
            o_ref[sub * rows:(sub + 1) * rows, lo:hi] = on.T.astype(o_ref.dtype)


def _attn_lat(q, k, vt, cache_k, cache_vt, lam_params, gain_col):
    n = q.shape[0]
    nb = n // DEC_SEQ
    qb = DEC_SEQ // ATT_Q_BLOCK
    q_spec = pl.BlockSpec((ATT_Q_BLOCK, QK_WIDTH), lambda b, j: (b * qb + j, 0))
    k_spec = pl.BlockSpec((DEC_SEQ, QK_WIDTH), lambda b, j: (b, 0))
    vt_spec = pl.BlockSpec((WIDTH_A, DEC_SEQ), lambda b, j: (b, 0))
    ck_spec = pl.BlockSpec((PAST_LEN, QK_WIDTH), lambda b, j: (b, 0))
    cvt_spec = pl.BlockSpec((WIDTH_A, PAST_LEN), lambda b, j: (b, 0))
    return pl.pallas_call(
        _attn_lat_kernel,
        grid=(nb, qb),
        in_specs=[q_spec, k_spec, vt_spec, ck_spec, cvt_spec,
                  _const_spec((4, HEAD_DIM_A)), _const_spec((V_DIM_A, 1))],
        out_specs=q_spec,
        out_shape=jax.ShapeDtypeStruct((n, WIDTH_A), BF16),
        compiler_params=_params("parallel", "arbitrary"),
        name="attn_lat",
    )(q, k, vt, cache_k, cache_vt, lam_params, gain_col)


def _mix0_kernel(a_ref, u_ref, vc_ref, x_ref, mod_ref, sw_ref, sb_ref, wo_ref,
                 g_ref, b_ref, o_ref, s_scr):
    t = x_ref.shape[0]
    for g in range(N_GROUPS_B):
        lo, hi = g * GROUP_DIM_B, (g + 1) * GROUP_DIM_B
        w = sw_ref[g]
        bias = jnp.broadcast_to(sb_ref[:, g:g + 1], (CHUNK, GROUP_DIM_B))
        for c in range(t // CHUNK):
            r0, r1 = c * CHUNK, (c + 1) * CHUNK
            mixed = jnp.dot(w, vc_ref[r0:r1, lo:hi], preferred_element_type=F32) + bias
            s_scr[r0:r1, lo:hi] = (u_ref[r0:r1, lo:hi] * mixed).astype(BF16)
    out = (jnp.dot(a_ref[...], wo_ref[0:WIDTH_A, :], preferred_element_type=F32)
           + jnp.dot(s_scr[...], wo_ref[WIDTH_A:, :], preferred_element_type=F32))
    o_ref[...] = _layer_norm(ALPHA * x_ref[...] + _mod(mod_ref, 2) * out,
                             g_ref[...], b_ref[...])


def _mix0(attn, u, vc, x, mods, row_fn, sgu_w, sgu_bt, w_out, g, b):
    n = x.shape[0]
    t = TOK_TILE
    tok = lambda width: pl.BlockSpec((t, width), lambda i: (i, 0))
    return pl.pallas_call(
        _mix0_kernel,
        grid=(n // t,),
        in_specs=[tok(WIDTH_A), tok(WIDTH_B), tok(WIDTH_B), tok(D_MODEL), _mod_spec(row_fn),
                  _const_spec((N_GROUPS_B, CHUNK, CHUNK)), _const_spec((CHUNK, N_GROUPS_B)),
                  _const_spec((WIDTH_A + WIDTH_B, D_MODEL)),
                  _const_spec((1, D_MODEL)), _const_spec((1, D_MODEL))],
        out_specs=tok(D_MODEL),
        out_shape=jax.ShapeDtypeStruct((n, D_MODEL), F32),
        scratch_shapes=[pltpu.VMEM((t, WIDTH_B), BF16)],
        compiler_params=_params("parallel"),
        name="mix0",
    )(attn, u, vc, x, mods, sgu_w, sgu_bt, w_out, g, b)


def _mix1_kernel(xp_ref, x_ref, xn_ref, mod_ref, wi_ref, cw_ref, wo_ref, g_ref, b_ref,
                 o_ref, z_scr, *, seq_len):
    t = x_ref.shape[0]
    halo = SUBLANES
    xin = jnp.concatenate([xp_ref[...], x_ref[...], xn_ref[...]], axis=0)
    h = (xin * (1.0 + _mod(mod_ref, 1)) + _mod(mod_ref, 0)).astype(BF16)
    y = jnp.dot(h, wi_ref[...], preferred_element_type=F32)
    z_scr[...] = y[:, WIDTH_C:2 * WIDTH_C] * y[:, 2 * WIDTH_C:]
    pos = (pl.program_id(0) * t + lax.broadcasted_iota(jnp.int32, (t, 1), 0)) % seq_len
    z_prev = jnp.where(pos != 0, z_scr[halo - 1:halo - 1 + t, :], 0.0)
    z_next = jnp.where(pos != seq_len - 1, z_scr[halo + 1:halo + 1 + t, :], 0.0)
    cw = cw_ref[...]
    conv = z_prev * cw[0:1] + z_scr[halo:halo + t, :] * cw[1:2] + z_next * cw[2:3]
    gated = (y[halo:halo + t, 0:WIDTH_C] * conv).astype(BF16)
    out = jnp.dot(gated, wo_ref[...], preferred_element_type=F32)
    o_ref[...] = _layer_norm(ALPHA * x_ref[...] + _mod(mod_ref, 2) * out,
                             g_ref[...], b_ref[...])


def _mix1(x, mods, row_fn, w_in, conv_w, w_out, g, b, seq_len):
    n = x.shape[0]
    t = TOK_TILE
    tiles_per_block = t // SUBLANES
    last = n // SUBLANES - 1
    tok = pl.BlockSpec((t, D_MODEL), lambda i: (i, 0))
    prev = pl.BlockSpec((SUBLANES, D_MODEL),
                        lambda i: (jnp.maximum(i * tiles_per_block - 1, 0), 0))
    nxt = pl.BlockSpec((SUBLANES, D_MODEL),
                       lambda i: (jnp.minimum((i + 1) * tiles_per_block, last), 0))
    return pl.pallas_call(
        functools.partial(_mix1_kernel, seq_len=seq_len),
        grid=(n // t,),
        in_specs=[prev, tok, nxt, _mod_spec(row_fn),
                  _const_spec((D_MODEL, 3 * WIDTH_C)), _const_spec((3, WIDTH_C)),
                  _const_spec((WIDTH_C, D_MODEL)),
                  _const_spec((1, D_MODEL)), _const_spec((1, D_MODEL))],
        out_specs=tok,
        out_shape=jax.ShapeDtypeStruct((n, D_MODEL), F32),
        scratch_shapes=[pltpu.VMEM((t + 2 * SUBLANES, WIDTH_C), F32)],
        compiler_params=_params("parallel"),
        name="mix1",
    )(x, x, x, mods, w_in, conv_w, w_out, g, b)


def _ffn_kernel(x_ref, mod_ref, w1_ref, w2_ref, g_ref, b_ref, o_ref):
    x = x_ref[...]
    h = (x * (1.0 + _mod(mod_ref, 4)) + _mod(mod_ref, 3)).astype(BF16)
    f = None
    for j in range(D_FF // FF_CHUNK):
        lo, hi = j * FF_CHUNK, (j + 1) * FF_CHUNK
        hid = jnp.dot(h, w1_ref[:, lo:hi], preferred_element_type=F32)
        hid = jnp.square(jnp.maximum(hid, 0.0)).astype(BF16)
        part = jnp.dot(hid, w2_ref[lo:hi, :], preferred_element_type=F32)
        f = part if f is None else f + part
    o_ref[...] = _layer_norm(ALPHA * x + _mod(mod_ref, 5) * f, g_ref[...], b_ref[...])


def _ffn(x, mods, row_fn, w1, w2, g, b):
    n = x.shape[0]
    t = TOK_TILE
    tok = pl.BlockSpec((t, D_MODEL), lambda i: (i, 0))
    return pl.pallas_call(
        _ffn_kernel,
        grid=(n // t,),
        in_specs=[tok, _mod_spec(row_fn), _const_spec((D_MODEL, D_FF)),
                  _const_spec((D_FF, D_MODEL)), _const_spec((1, D_MODEL)),
                  _const_spec((1, D_MODEL))],
        out_specs=tok,
        out_shape=jax.ShapeDtypeStruct((n, D_MODEL), F32),
        compiler_params=_params("parallel"),
        name="ffn",
    )(x, mods, w1, w2, g, b)


def _rope_tables(n):
    rows = n // GRID_W
    row = jnp.repeat(jnp.arange(rows, dtype=F32), GRID_W)
    col = jnp.tile(jnp.arange(GRID_W, dtype=F32), rows)
    inv = 1.0 / (ROPE_BASE ** (jnp.arange(ROPE_PAIRS, dtype=F32) / ROPE_PAIRS))
    ang_r, ang_c = row[:, None] * inv, col[:, None] * inv
    zero = jnp.zeros_like(ang_r)
    reps = LANES // HEAD_DIM_A
    cos = jnp.tile(jnp.concatenate([jnp.cos(ang_r)] * 2 + [jnp.cos(ang_c)] * 2, axis=1), (1, reps))
    sa = jnp.tile(jnp.concatenate([-jnp.sin(ang_r), zero, -jnp.sin(ang_c), zero], axis=1), (1, reps))
    sb = jnp.tile(jnp.concatenate([zero, jnp.sin(ang_r), zero, jnp.sin(ang_c)], axis=1), (1, reps))
    return cos, sa, sb


def kernel(x_prompt, x_sample, cache_k0, cache_v0, c, c_ctx, w_mod0, b_mod0, w_in0, lambda_q1_0, lambda_k1_0, lambda_q2_0, lambda_k2_0, subln_g0, sgu_w0, sgu_b0, w_out0, ln_mix_g0, ln_mix_b0, w_ff1_0, w_ff2_0, ln_ff_g0, ln_ff_b0, w_mod1, b_mod1, w_in1, conv_w1, w_out1, ln_mix_g1, ln_mix_b1, w_ff1_1, w_ff2_1, ln_ff_g1, ln_ff_b1):
    batch, seq, d = x_prompt.shape
    dec_batch, dec_seq, _ = x_sample.shape
    assert (seq, dec_seq, d) == (SEQ, DEC_SEQ, D_MODEL) and 1 + dec_batch <= MOD_ROWS
    row = lambda v: v.reshape(1, -1)

    cvec = jnp.concatenate([c_ctx[None, :], c, jnp.zeros((MOD_ROWS - 1 - dec_batch, d), F32)], axis=0)
    mods0, mods1 = _modulations(cvec.T, w_mod0, row(b_mod0), w_mod1, row(b_mod1), 1 + dec_batch)
    mods0 = mods0.reshape(MOD_ROWS, 1, N_MOD)
    mods1 = mods1.reshape(MOD_ROWS, 1, N_MOD)
    ctx_row = lambda i: 0
    lat_row = lambda i: 1 + (i * TOK_TILE) // DEC_SEQ

    xp = x_prompt.reshape(batch * seq, d)
    xs = x_sample.reshape(dec_batch * dec_seq, d)

    w_in0_b = w_in0.astype(BF16)
    lam_params = jnp.stack([lambda_q1_0, lambda_k1_0, lambda_q2_0, lambda_k2_0])
    gain = row(subln_g0)
    qp, kp, vp, up, vcp = _in_proj0(xp, mods0, w_in0_b, ctx_row, None, F32)
    qs, ks, vs, us, vcs = _in_proj0(xs, mods0, w_in0_b, lat_row, _rope_tables(dec_seq), BF16)
    ap = _attn_ctx(qp, kp, vp, lam_params, gain)
    cache_vt = jnp.swapaxes(cache_v0.reshape(dec_batch, PAST_LEN, WIDTH_A), 1, 2)
    a_s = _attn_lat(qs, ks, vs, cache_k0.reshape(dec_batch * PAST_LEN, QK_WIDTH),
                    cache_vt.reshape(dec_batch * WIDTH_A, PAST_LEN).astype(BF16),
                    lam_params, gain.reshape(V_DIM_A, 1))
    mix0_w = (sgu_w0.astype(BF16), sgu_b0.T, w_out0.astype(BF16), row(ln_mix_g0), row(ln_mix_b0))
    xp = _mix0(ap, up, vcp, xp, mods0, ctx_row, *mix0_w)
    xs = _mix0(a_s, us, vcs, xs, mods0, lat_row, *mix0_w)
    ffn0_w = (w_ff1_0.astype(BF16), w_ff2_0.astype(BF16), row(ln_ff_g0), row(ln_ff_b0))
    xp = _ffn(xp, mods0, ctx_row, *ffn0_w)
    xs = _ffn(xs, mods0, lat_row, *ffn0_w)

    mix1_w = (w_in1.astype(BF16), conv_w1, w_out1.astype(BF16), row(ln_mix_g1), row(ln_mix_b1))
    xp = _mix1(xp, mods1, ctx_row, *mix1_w, seq_len=seq)
    xs = _mix1(xs, mods1, lat_row, *mix1_w, seq_len=dec_seq)
    ffn1_w = (w_ff1_1.astype(BF16), w_ff2_1.astype(BF16), row(ln_ff_g1), row(ln_ff_b1))
    xp = _ffn(xp, mods1, ctx_row, *ffn1_w)
    xs = _ffn(xs, mods1, lat_row, *ffn1_w)

    return (xp.reshape(batch, seq, d), xs.reshape(dec_batch, dec_seq, d),
            kp.reshape(batch, seq, N_HEADS_A, 2, HEAD_DIM_A),
            vp.reshape(batch, seq, N_HEADS_A, V_DIM_A))
```

```python
import functools
import math

import jax
import jax.numpy as jnp
from jax import lax
from jax.experimental import pallas as pl
from jax.experimental.pallas import tpu as pltpu

F32 = jnp.float32
BF16 = jnp.bfloat16

D_MODEL = 1024
DEPTH = 2
SEQ = 256
DEC_SEQ = 4096
PAST_LEN = 256
GRID_W = 64
N_HEADS_A = 4
HEAD_DIM_A = 64
V_DIM_A = 2 * HEAD_DIM_A
QK_WIDTH = N_HEADS_A * 2 * HEAD_DIM_A
WIDTH_A = N_HEADS_A * V_DIM_A
N_GROUPS_B = 4
CHUNK = 128
GROUP_DIM_B = 128
WIDTH_B = N_GROUPS_B * GROUP_DIM_B
IN_WIDTH_0 = 2 * QK_WIDTH + WIDTH_A + 2 * WIDTH_B
WIDTH_C = D_MODEL
D_FF = 4 * D_MODEL
ROPE_BASE = 10000.0
ROPE_PAIRS = HEAD_DIM_A // 4
LN_EPS = 1e-5
ALPHA = (2 * DEPTH) ** 0.25
LAMBDA_INIT_0 = 0.8 - 0.6 * math.exp(-0.3 * 0)
Q_SCALE = HEAD_DIM_A ** -0.5 * math.log2(math.e)

LANES = 128
SUBLANES = 8
BF16_SUBLANES = 16
VMEM_LIMIT = 56 * 2 ** 20
N_MOD = 6 * D_MODEL
MOD_ROWS = 8
TOK_TILE = 512
ATT_Q_BLOCK = 256
ATT_SUB_BLOCK = 128
ATT_KEY_CHUNK = 256
ATT_LOOKAHEAD = 5
FF_CHUNK = 1024
MOD_TILE = 512


def _const_spec(shape):
    zeros = (0,) * len(shape)
    return pl.BlockSpec(shape, lambda *_: zeros, pipeline_mode=pl.Buffered(1))


def _params(*sem):
    return pltpu.CompilerParams(dimension_semantics=sem, vmem_limit_bytes=VMEM_LIMIT)


def _layer_norm(x, g, b):
    mu = jnp.mean(x, axis=-1, keepdims=True)
    xc = x - mu
    var = jnp.mean(xc * xc, axis=-1, keepdims=True)
    return xc * lax.rsqrt(var + LN_EPS) * g + b


def _mod(mod_ref, idx):
    return mod_ref[:, idx * D_MODEL:(idx + 1) * D_MODEL]


def _mod_kernel(ct_ref, w0_ref, b0_ref, w1_ref, b1_ref, o0_ref, o1_ref, *, rows):
    ct = ct_ref[...]
    s = ct * jax.nn.sigmoid(ct)
    for w_ref, b_ref, o_ref in ((w0_ref, b0_ref, o0_ref), (w1_ref, b1_ref, o1_ref)):
        w = w_ref[...]
        b = b_ref[...]
        for r in range(rows):
            o_ref[r:r + 1, :] = jnp.sum(w * s[:, r:r + 1], axis=0, keepdims=True) + b
        o_ref[rows:, :] = jnp.zeros((MOD_ROWS - rows, w.shape[1]), F32)


def _modulations(ct, w0, b0, w1, b1, rows):
    w_spec = pl.BlockSpec((D_MODEL, MOD_TILE), lambda j: (0, j))
    v_spec = pl.BlockSpec((1, MOD_TILE), lambda j: (0, j))
    o_spec = pl.BlockSpec((MOD_ROWS, MOD_TILE), lambda j: (0, j))
    out = jax.ShapeDtypeStruct((MOD_ROWS, N_MOD), F32)
    return pl.pallas_call(
        functools.partial(_mod_kernel, rows=rows),
        grid=(N_MOD // MOD_TILE,),
        in_specs=[_const_spec((D_MODEL, MOD_ROWS)), w_spec, v_spec, w_spec, v_spec],
        out_specs=[o_spec, o_spec],
        out_shape=[out, out],
        compiler_params=_params("parallel"),
        name="adaln_mod",
    )(ct, w0, b0, w1, b1)


def _mod_spec(row_fn):
    return pl.BlockSpec((None, 1, N_MOD), lambda i: (row_fn(i), 0, 0))


def _in0_kernel(*refs, rope):
    if rope:
        (x_ref, mod_ref, w_ref, cos_ref, sa_ref, sb_ref,
         q_ref, k_ref, v_ref, u_ref, vc_ref) = refs
    else:
        x_ref, mod_ref, w_ref, q_ref, k_ref, v_ref, u_ref, vc_ref, k_out_ref, v_out_ref = refs
    h = (x_ref[...] * (1.0 + _mod(mod_ref, 1)) + _mod(mod_ref, 0)).astype(BF16)
    y = jnp.dot(h, w_ref[...], preferred_element_type=F32)
    if rope:
        cos, sa, sb = cos_ref[...], sa_ref[...], sb_ref[...]
    for j in range(QK_WIDTH // LANES):
        lo, hi = j * LANES, (j + 1) * LANES
        qj = y[:, lo:hi]
        kj = y[:, QK_WIDTH + lo:QK_WIDTH + hi]
        if rope:
            qj = (qj * cos + pltpu.roll(qj, LANES - ROPE_PAIRS, 1) * sa
                  + pltpu.roll(qj, ROPE_PAIRS, 1) * sb)
            kj = (kj * cos + pltpu.roll(kj, LANES - ROPE_PAIRS, 1) * sa
                  + pltpu.roll(kj, ROPE_PAIRS, 1) * sb)
        q_ref[:, lo:hi] = (qj * Q_SCALE).astype(q_ref.dtype)
        k_ref[:, lo:hi] = kj.astype(k_ref.dtype)
        if not rope:
            kt = kj.T
            for b in range(kj.shape[0] // SEQ):
                for i in range(2):
                    k_out_ref[b, j, i] = kt[i * HEAD_DIM_A:(i + 1) * HEAD_DIM_A, b * SEQ:(b + 1) * SEQ]
    off = 2 * QK_WIDTH
    v = y[:, off:off + WIDTH_A]
    v_ref[...] = (v.T if rope else v).astype(v_ref.dtype)
    if not rope:
        for j in range(N_HEADS_A):
            v_out_ref[:, j, :] = v[:, j * V_DIM_A:(j + 1) * V_DIM_A]
    off += WIDTH_A
    u_ref[...] = y[:, off:off + WIDTH_B]
    off += WIDTH_B
    for g in range(N_GROUPS_B):
        gg = y[:, off + g * GROUP_DIM_B:off + (g + 1) * GROUP_DIM_B]
        mu = jnp.mean(gg, axis=-1, keepdims=True)
        gc = gg - mu
        var = jnp.mean(gc * gc, axis=-1, keepdims=True)
        vc_ref[:, g * GROUP_DIM_B:(g + 1) * GROUP_DIM_B] = (
            gc * lax.rsqrt(var + LN_EPS)).astype(vc_ref.dtype)


def _in_proj0(x, mods, w_in, row_fn, rope_tabs):
    n = x.shape[0]
    t = TOK_TILE
    tok = lambda width: pl.BlockSpec((t, width), lambda i: (i, 0))
    in_specs = [tok(D_MODEL), _mod_spec(row_fn), _const_spec((D_MODEL, IN_WIDTH_0))]
    args = [x, mods, w_in]
    v_spec = tok(WIDTH_A)
    v_shape = (n, WIDTH_A)
    if rope_tabs is not None:
        blocks_per_seq = DEC_SEQ // t
        tab = pl.BlockSpec((t, LANES), lambda i: (i % blocks_per_seq, 0))
        in_specs += [tab, tab, tab]
        args += list(rope_tabs)
        v_spec = pl.BlockSpec((WIDTH_A, t), lambda i: (i // blocks_per_seq, i % blocks_per_seq))
        v_shape = (n // DEC_SEQ * WIDTH_A, DEC_SEQ)
    out_shape = [jax.ShapeDtypeStruct((n, QK_WIDTH), BF16),
                 jax.ShapeDtypeStruct((n, QK_WIDTH), BF16),
                 jax.ShapeDtypeStruct(v_shape, BF16),
                 jax.ShapeDtypeStruct((n, WIDTH_B), F32),
                 jax.ShapeDtypeStruct((n, WIDTH_B), BF16)]
    out_specs = [tok(QK_WIDTH), tok(QK_WIDTH), v_spec, tok(WIDTH_B), tok(WIDTH_B)]
    if rope_tabs is None:
        out_shape += [jax.ShapeDtypeStruct((n // SEQ, N_HEADS_A, 2, HEAD_DIM_A, SEQ), F32),
                      jax.ShapeDtypeStruct((n, N_HEADS_A, V_DIM_A), F32)]
        out_specs += [pl.BlockSpec((t // SEQ, N_HEADS_A, 2, HEAD_DIM_A, SEQ),
                                   lambda i: (i, 0, 0, 0, 0)),
                      pl.BlockSpec((t, N_HEADS_A, V_DIM_A), lambda i: (i, 0, 0))]
    return pl.pallas_call(
        functools.partial(_in0_kernel, rope=rope_tabs is not None),
        grid=(n // t,),
        in_specs=in_specs,
        out_specs=out_specs,
        out_shape=out_shape,
        compiler_params=_params("parallel"),
        name="in_proj0",
    )(*args)


def _lambda(lp_ref):
    lp = lp_ref[...]
    a = jnp.sum(lp[0:1] * lp[1:2], axis=1, keepdims=True)
    b = jnp.sum(lp[2:3] * lp[3:4], axis=1, keepdims=True)
    return jnp.exp(a) - jnp.exp(b) + LAMBDA_INIT_0


def _stack_halves(qh):
    lane = lax.broadcasted_iota(jnp.int32, qh.shape, 1)
    zero = jnp.zeros_like(qh)
    return jnp.concatenate([jnp.where(lane < HEAD_DIM_A, qh, zero),
                            jnp.where(lane >= HEAD_DIM_A, qh, zero)], axis=0)


def _scores(qq, k):
    return lax.dot_general(qq, k, (((1,), (1,)), ((), ())), preferred_element_type=F32)


def _head_out(pv, l, lam, gain, rows):
    o = pv[:rows] / l[:rows] - lam * (pv[rows:] / l[rows:])
    ms = jnp.mean(o * o, axis=-1, keepdims=True)
    return o * lax.rsqrt(ms + LN_EPS) * gain * (1.0 - LAMBDA_INIT_0)


def _attn_ctx_kernel(q_ref, k_ref, v_ref, lp_ref, gain_ref, o_ref):
    lam = _lambda(lp_ref)
    gain = gain_ref[...]
    rows = q_ref.shape[0]
    for h in range(N_HEADS_A):
        lo, hi = h * LANES, (h + 1) * LANES
        qq = _stack_halves(q_ref[:, lo:hi])
        s = _scores(qq, k_ref[:, lo:hi])
        p = jnp.exp2(s - jnp.max(s, axis=-1, keepdims=True))
        l = jnp.sum(p, axis=-1, keepdims=True)
        pv = jnp.dot(p.astype(BF16), v_ref[:, lo:hi], preferred_element_type=F32)
        o_ref[:, lo:hi] = _head_out(pv, l, lam, gain, rows).astype(o_ref.dtype)


def _attn_ctx(q, k, v, lam_params, gain):
    n = q.shape[0]
    blk = lambda: pl.BlockSpec((SEQ, QK_WIDTH), lambda b: (b, 0))
    return pl.pallas_call(
        _attn_ctx_kernel,
        grid=(n // SEQ,),
        in_specs=[blk(), blk(), blk(), _const_spec((4, HEAD_DIM_A)), _const_spec((1, V_DIM_A))],
        out_specs=blk(),
        out_shape=jax.ShapeDtypeStruct((n, WIDTH_A), BF16),
        compiler_params=_params("parallel"),
        name="attn_ctx",
    )(q, k, v, lam_params, gain)


def _attn_lat_kernel(q_ref, k_ref, vt_ref, ck_ref, cvt_ref, lp_ref, gain_ref, o_ref):
    lam = _lambda(lp_ref)
    gain = gain_ref[...]
    rows = ATT_SUB_BLOCK
    chunks = [(None, PAST_LEN)] + [(s, ATT_KEY_CHUNK) for s in range(0, DEC_SEQ, ATT_KEY_CHUNK)]
    ones = jnp.ones((BF16_SUBLANES, ATT_KEY_CHUNK), BF16)
    units = [(sub, h, c) for sub in range(q_ref.shape[0] // rows)
             for h in range(N_HEADS_A) for c in range(len(chunks))]
    qq_of = {}
    scores = {}

    def issue_scores(unit):
        sub, h, c = unit
        lo, hi = h * LANES, (h + 1) * LANES
        if c == 0:
            qq_of[sub, h] = _stack_halves(q_ref[sub * rows:(sub + 1) * rows, lo:hi])
        start, size = chunks[c]
        keys = ck_ref[:, lo:hi].astype(BF16) if start is None else k_ref[start:start + size, lo:hi]
        scores[unit] = _scores(keys, qq_of[sub, h])

    for unit in units[:ATT_LOOKAHEAD]:
        issue_scores(unit)
    m = acc = None
    for i, unit in enumerate(units):
        if i + ATT_LOOKAHEAD < len(units):
            issue_scores(units[i + ATT_LOOKAHEAD])
        sub, h, c = unit
        lo, hi = h * LANES, (h + 1) * LANES
        start, size = chunks[c]
        vals_t = cvt_ref[lo:hi, :] if start is None else vt_ref[lo:hi, start:start + size]
        st = scores.pop(unit)
        cm = jnp.max(st, axis=0, keepdims=True)
        m_new = cm if c == 0 else jnp.maximum(m, cm)
        pt = jnp.exp2(st - m_new).astype(BF16)
        part = jnp.dot(jnp.concatenate([vals_t, ones[:, :size]], axis=0), pt,
                       preferred_element_type=F32)
        acc = part if c == 0 else acc * jnp.exp2(m - m_new) + part
        m = m_new
        if c == len(chunks) - 1:
            pvt = acc[:V_DIM_A]
            l = acc[V_DIM_A:V_DIM_A + 1]
            ot = pvt[:, :rows] / l[:, :rows] - lam * (pvt[:, rows:] / l[:, rows:])
            ms = jnp.mean(ot * ot, axis=0, keepdims=True)
            on = ot * lax.rsqrt(ms + LN_EPS) * gain * (1.0 - LAMBDA_INIT_0)
            o_ref[sub * rows:(sub + 1) * rows, lo:hi] = on.T.astype(o_ref.dtype)


def _attn_lat(q, k, vt, cache_k, cache_vt, lam_params, gain_col):
    n = q.shape[0]
    nb = n // DEC_SEQ
    qb = DEC_SEQ // ATT_Q_BLOCK
    q_spec = pl.BlockSpec((ATT_Q_BLOCK, QK_WIDTH), lambda b, j: (b * qb + j, 0))
    k_spec = pl.BlockSpec((DEC_SEQ, QK_WIDTH), lambda b, j: (b, 0))
    vt_spec = pl.BlockSpec((WIDTH_A, DEC_SEQ), lambda b, j: (b, 0))
    ck_spec = pl.BlockSpec((PAST_LEN, QK_WIDTH), lambda b, j: (b, 0))
    cvt_spec = pl.BlockSpec((WIDTH_A, PAST_LEN), lambda b, j: (b, 0))
    return pl.pallas_call(
        _attn_lat_kernel,
        grid=(nb, qb),
        in_specs=[q_spec, k_spec, vt_spec, ck_spec, cvt_spec,
                  _const_spec((4, HEAD_DIM_A)), _const_spec((V_DIM_A, 1))],
        out_specs=q_spec,
        out_shape=jax.ShapeDtypeStruct((n, WIDTH_A), BF16),
        compiler_params=_params("parallel", "arbitrary"),
        name="attn_lat",
    )(q, k, vt, cache_k, cache_vt, lam_params, gain_col)


def _mix0_kernel(a_ref, u_ref, vc_ref, x_ref, mod_ref, sw_ref, sb_ref, wo_ref,
                 g_ref, b_ref, o_ref, s_scr):
    t = x_ref.shape[0]
    for g in range(N_GROUPS_B):
        lo, hi = g * GROUP_DIM_B, (g + 1) * GROUP_DIM_B
        w = sw_ref[g]
        bias = jnp.broadcast_to(sb_ref[:, g:g + 1], (CHUNK, GROUP_DIM_B))
        for c in range(t // CHUNK):
            r0, r1 = c * CHUNK, (c + 1) * CHUNK
            mixed = jnp.dot(w, vc_ref[r0:r1, lo:hi], preferred_element_type=F32) + bias
            s_scr[r0:r1, lo:hi] = (u_ref[r0:r1, lo:hi] * mixed).astype(BF16)
    out = (jnp.dot(a_ref[...], wo_ref[0:WIDTH_A, :], preferred_element_type=F32)
           + jnp.dot(s_scr[...], wo_ref[WIDTH_A:, :], preferred_element_type=F32))
    o_ref[...] = _layer_norm(ALPHA * x_ref[...] + _mod(mod_ref, 2) * out,
                             g_ref[...], b_ref[...])


def _mix0(attn, u, vc, x, mods, row_fn, sgu_w, sgu_bt, w_out, g, b):
    n = x.shape[0]
    t = TOK_TILE
    tok = lambda width: pl.BlockSpec((t, width), lambda i: (i, 0))
    return pl.pallas_call(
        _mix0_kernel,
        grid=(n // t,),
        in_specs=[tok(WIDTH_A), tok(WIDTH_B), tok(WIDTH_B), tok(D_MODEL), _mod_spec(row_fn),
                  _const_spec((N_GROUPS_B, CHUNK, CHUNK)), _const_spec((CHUNK, N_GROUPS_B)),
                  _const_spec((WIDTH_A + WIDTH_B, D_MODEL)),
                  _const_spec((1, D_MODEL)), _const_spec((1, D_MODEL))],
        out_specs=tok(D_MODEL),
        out_shape=jax.ShapeDtypeStruct((n, D_MODEL), F32),
        scratch_shapes=[pltpu.VMEM((t, WIDTH_B), BF16)],
        compiler_params=_params("parallel"),
        name="mix0",
    )(attn, u, vc, x, mods, sgu_w, sgu_bt, w_out, g, b)


def _mix1_kernel(xp_ref, x_ref, xn_ref, mod_ref, wi_ref, cw_ref, wo_ref, g_ref, b_ref,
                 o_ref, z_scr, *, seq_len):
    t = x_ref.shape[0]
    halo = SUBLANES
    xin = jnp.concatenate([xp_ref[...], x_ref[...], xn_ref[...]], axis=0)
    h = (xin * (1.0 + _mod(mod_ref, 1)) + _mod(mod_ref, 0)).astype(BF16)
    y = jnp.dot(h, wi_ref[...], preferred_element_type=F32)
    z_scr[...] = y[:, WIDTH_C:2 * WIDTH_C] * y[:, 2 * WIDTH_C:]
    pos = (pl.program_id(0) * t + lax.broadcasted_iota(jnp.int32, (t, 1), 0)) % seq_len
    z_prev = jnp.where(pos != 0, z_scr[halo - 1:halo - 1 + t, :], 0.0)
    z_next = jnp.where(pos != seq_len - 1, z_scr[halo + 1:halo + 1 + t, :], 0.0)
    cw = cw_ref[...]
    conv = z_prev * cw[0:1] + z_scr[halo:halo + t, :] * cw[1:2] + z_next * cw[2:3]
    gated = (y[halo:halo + t, 0:WIDTH_C] * conv).astype(BF16)
    out = jnp.dot(gated, wo_ref[...], preferred_element_type=F32)
    o_ref[...] = _layer_norm(ALPHA * x_ref[...] + _mod(mod_ref, 2) * out,
                             g_ref[...], b_ref[...])


def _mix1(x, mods, row_fn, w_in, conv_w, w_out, g, b, seq_len):
    n = x.shape[0]
    t = TOK_TILE
    tiles_per_block = t // SUBLANES
    last = n // SUBLANES - 1
    tok = pl.BlockSpec((t, D_MODEL), lambda i: (i, 0))
    prev = pl.BlockSpec((SUBLANES, D_MODEL),
                        lambda i: (jnp.maximum(i * tiles_per_block - 1, 0), 0))
    nxt = pl.BlockSpec((SUBLANES, D_MODEL),
                       lambda i: (jnp.minimum((i + 1) * tiles_per_block, last), 0))
    return pl.pallas_call(
        functools.partial(_mix1_kernel, seq_len=seq_len),
        grid=(n // t,),
        in_specs=[prev, tok, nxt, _mod_spec(row_fn),
                  _const_spec((D_MODEL, 3 * WIDTH_C)), _const_spec((3, WIDTH_C)),
                  _const_spec((WIDTH_C, D_MODEL)),
                  _const_spec((1, D_MODEL)), _const_spec((1, D_MODEL))],
        out_specs=tok,
        out_shape=jax.ShapeDtypeStruct((n, D_MODEL), F32),
        scratch_shapes=[pltpu.VMEM((t + 2 * SUBLANES, WIDTH_C), F32)],
        compiler_params=_params("parallel"),
        name="mix1",
    )(x, x, x, mods, w_in, conv_w, w_out, g, b)


def _ffn_kernel(x_ref, mod_ref, w1_ref, w2_ref, g_ref, b_ref, o_ref):
    x = x_ref[...]
    h = (x * (1.0 + _mod(mod_ref, 4)) + _mod(mod_ref, 3)).astype(BF16)
    f = None
    for j in range(D_FF // FF_CHUNK):
        lo, hi = j * FF_CHUNK, (j + 1) * FF_CHUNK
        hid = jnp.dot(h, w1_ref[:, lo:hi], preferred_element_type=F32)
        hid = jnp.square(jnp.maximum(hid, 0.0)).astype(BF16)
        part = jnp.dot(hid, w2_ref[lo:hi, :], preferred_element_type=F32)
        f = part if f is None else f + part
    o_ref[...] = _layer_norm(ALPHA * x + _mod(mod_ref, 5) * f, g_ref[...], b_ref[...])


def _ffn(x, mods, row_fn, w1, w2, g, b):
    n = x.shape[0]
    t = TOK_TILE
    tok = pl.BlockSpec((t, D_MODEL), lambda i: (i, 0))
    return pl.pallas_call(
        _ffn_kernel,
        grid=(n // t,),
        in_specs=[tok, _mod_spec(row_fn), _const_spec((D_MODEL, D_FF)),
                  _const_spec((D_FF, D_MODEL)), _const_spec((1, D_MODEL)),
                  _const_spec((1, D_MODEL))],
        out_specs=tok,
        out_shape=jax.ShapeDtypeStruct((n, D_MODEL), F32),
        compiler_params=_params("parallel"),
        name="ffn",
    )(x, mods, w1, w2, g, b)


def _rope_tables(n):
    rows = n // GRID_W
    row = jnp.repeat(jnp.arange(rows, dtype=F32), GRID_W)
    col = jnp.tile(jnp.arange(GRID_W, dtype=F32), rows)
    inv = 1.0 / (ROPE_BASE ** (jnp.arange(ROPE_PAIRS, dtype=F32) / ROPE_PAIRS))
    ang_r, ang_c = row[:, None] * inv, col[:, None] * inv
    zero = jnp.zeros_like(ang_r)
    reps = LANES // HEAD_DIM_A
    cos = jnp.tile(jnp.concatenate([jnp.cos(ang_r)] * 2 + [jnp.cos(ang_c)] * 2, axis=1), (1, reps))
    sa = jnp.tile(jnp.concatenate([-jnp.sin(ang_r), zero, -jnp.sin(ang_c), zero], axis=1), (1, reps))
    sb = jnp.tile(jnp.concatenate([zero, jnp.sin(ang_r), zero, jnp.sin(ang_c)], axis=1), (1, reps))
    return cos, sa, sb


def kernel(x_prompt, x_sample, cache_k0, cache_v0, c, c_ctx, w_mod0, b_mod0, w_in0, lambda_q1_0, lambda_k1_0, lambda_q2_0, lambda_k2_0, subln_g0, sgu_w0, sgu_b0, w_out0, ln_mix_g0, ln_mix_b0, w_ff1_0, w_ff2_0, ln_ff_g0, ln_ff_b0, w_mod1, b_mod1, w_in1, conv_w1, w_out1, ln_mix_g1, ln_mix_b1, w_ff1_1, w_ff2_1, ln_ff_g1, ln_ff_b1):
    batch, seq, d = x_prompt.shape
    dec_batch, dec_seq, _ = x_sample.shape
    assert (seq, dec_seq, d) == (SEQ, DEC_SEQ, D_MODEL) and 1 + dec_batch <= MOD_ROWS
    row = lambda v: v.reshape(1, -1)

    cvec = jnp.concatenate([c_ctx[None, :], c, jnp.zeros((MOD_ROWS - 1 - dec_batch, d), F32)], axis=0)
    mods0, mods1 = _modulations(cvec.T, w_mod0, row(b_mod0), w_mod1, row(b_mod1), 1 + dec_batch)
    mods0 = mods0.reshape(MOD_ROWS, 1, N_MOD)
    mods1 = mods1.reshape(MOD_ROWS, 1, N_MOD)
    ctx_row = lambda i: 0
    lat_row = lambda i: 1 + (i * TOK_TILE) // DEC_SEQ

    xp = x_prompt.reshape(batch * seq, d)
    xs = x_sample.reshape(dec_batch * dec_seq, d)

    w_in0_b = w_in0.astype(BF16)
    lam_params = jnp.stack([lambda_q1_0, lambda_k1_0, lambda_q2_0, lambda_k2_0])
    gain = row(subln_g0)
    qp, kp, vp, up, vcp, new_k0, new_v0 = _in_proj0(xp, mods0, w_in0_b, ctx_row, None)
    qs, ks, vs, us, vcs = _in_proj0(xs, mods0, w_in0_b, lat_row, _rope_tables(dec_seq))
    ap = _attn_ctx(qp, kp, vp, lam_params, gain)
    cache_vt = jnp.swapaxes(cache_v0.reshape(dec_batch, PAST_LEN, WIDTH_A), 1, 2)
    a_s = _attn_lat(qs, ks, vs, cache_k0.reshape(dec_batch * PAST_LEN, QK_WIDTH),
                    cache_vt.reshape(dec_batch * WIDTH_A, PAST_LEN).astype(BF16),
                    lam_params, gain.reshape(V_DIM_A, 1))
    mix0_w = (sgu_w0.astype(BF16), sgu_b0.T, w_out0.astype(BF16), row(ln_mix_g0), row(ln_mix_b0))
    xp = _mix0(ap, up, vcp, xp, mods0, ctx_row, *mix0_w)
    xs = _mix0(a_s, us, vcs, xs, mods0, lat_row, *mix0_w)
    ffn0_w = (w_ff1_0.astype(BF16), w_ff2_0.astype(BF16), row(ln_ff_g0), row(ln_ff_b0))
    xp = _ffn(xp, mods0, ctx_row, *ffn0_w)
    xs = _ffn(xs, mods0, lat_row, *ffn0_w)

    mix1_w = (w_in1.astype(BF16), conv_w1, w_out1.astype(BF16), row(ln_mix_g1), row(ln_mix_b1))
    xp = _mix1(xp, mods1, ctx_row, *mix1_w, seq_len=seq)
    xs = _mix1(xs, mods1, lat_row, *mix1_w, seq_len=dec_seq)
    ffn1_w = (w_ff1_1.astype(BF16), w_ff2_1.astype(BF16), row(ln_ff_g1), row(ln_ff_b1))
    xp = _ffn(xp, mods1, ctx_row, *ffn1_w)
    xs = _ffn(xs, mods1, lat_row, *ffn1_w)

    return (xp.reshape(batch, seq, d), xs.reshape(dec_batch, dec_seq, d),
            jnp.transpose(new_k0, (0, 4, 1, 2, 3)),
            new_v0.reshape(batch, seq, N_HEADS_A, V_DIM_A))
```

```python
import functools
import math

import jax
import jax.numpy as jnp
from jax import lax
from jax.experimental import pallas as pl
from jax.experimental.pallas import tpu as pltpu

F32 = jnp.float32
BF16 = jnp.bfloat16

D_MODEL = 1024
DEPTH = 2
SEQ = 256
DEC_SEQ = 4096
PAST_LEN = 256
GRID_W = 64
N_HEADS_A = 4
HEAD_DIM_A = 64
V_DIM_A = 2 * HEAD_DIM_A
QK_WIDTH = N_HEADS_A * 2 * HEAD_DIM_A
WIDTH_A = N_HEADS_A * V_DIM_A
N_GROUPS_B = 4
CHUNK = 128
GROUP_DIM_B = 128
WIDTH_B = N_GROUPS_B * GROUP_DIM_B
IN_WIDTH_0 = 2 * QK_WIDTH + WIDTH_A + 2 * WIDTH_B
WIDTH_C = D_MODEL
D_FF = 4 * D_MODEL
ROPE_BASE = 10000.0
ROPE_PAIRS = HEAD_DIM_A // 4
LN_EPS = 1e-5
ALPHA = (2 * DEPTH) ** 0.25
LAMBDA_INIT_0 = 0.8 - 0.6 * math.exp(-0.3 * 0)
Q_SCALE = HEAD_DIM_A ** -0.5 * math.log2(math.e)

LANES = 128
SUBLANES = 8
BF16_SUBLANES = 16
VMEM_LIMIT = 56 * 2 ** 20
N_MOD = 6 * D_MODEL
MOD_ROWS = 8
TOK_TILE = 1024
SUB_TILE = 512
ATT_Q_BLOCK = 256
ATT_SUB_BLOCK = 128
ATT_KEY_CHUNK = 256
ATT_LOOKAHEAD = 5
FF_CHUNK = 1024
MOD_TILE = 512


def _const_spec(shape):
    zeros = (0,) * len(shape)
    return pl.BlockSpec(shape, lambda *_: zeros, pipeline_mode=pl.Buffered(1))


def _params(*sem):
    return pltpu.CompilerParams(dimension_semantics=sem, vmem_limit_bytes=VMEM_LIMIT)


def _layer_norm(x, g, b):
    mu = jnp.mean(x, axis=-1, keepdims=True)
    xc = x - mu
    var = jnp.mean(xc * xc, axis=-1, keepdims=True)
    return xc * lax.rsqrt(var + LN_EPS) * g + b


def _mod(mod_ref, idx):
    return mod_ref[:, idx * D_MODEL:(idx + 1) * D_MODEL]


def _sub_tiles(ref):
    return [slice(i * SUB_TILE, (i + 1) * SUB_TILE) for i in range(ref.shape[0] // SUB_TILE)]


def _mod_kernel(ct_ref, w0_ref, b0_ref, w1_ref, b1_ref, o0_ref, o1_ref, *, rows):
    ct = ct_ref[...]
    s = ct * jax.nn.sigmoid(ct)
    for w_ref, b_ref, o_ref in ((w0_ref, b0_ref, o0_ref), (w1_ref, b1_ref, o1_ref)):
        w = w_ref[...]
        b = b_ref[...]
        for r in range(rows):
            o_ref[r:r + 1, :] = jnp.sum(w * s[:, r:r + 1], axis=0, keepdims=True) + b
        o_ref[rows:, :] = jnp.zeros((MOD_ROWS - rows, w.shape[1]), F32)


def _modulations(ct, w0, b0, w1, b1, rows):
    w_spec = pl.BlockSpec((D_MODEL, MOD_TILE), lambda j: (0, j))
    v_spec = pl.BlockSpec((1, MOD_TILE), lambda j: (0, j))
    o_spec = pl.BlockSpec((MOD_ROWS, MOD_TILE), lambda j: (0, j))
    out = jax.ShapeDtypeStruct((MOD_ROWS, N_MOD), F32)
    return pl.pallas_call(
        functools.partial(_mod_kernel, rows=rows),
        grid=(N_MOD // MOD_TILE,),
        in_specs=[_const_spec((D_MODEL, MOD_ROWS)), w_spec, v_spec, w_spec, v_spec],
        out_specs=[o_spec, o_spec],
        out_shape=[out, out],
        compiler_params=_params("parallel"),
        name="adaln_mod",
    )(ct, w0, b0, w1, b1)


def _mod_spec(row_fn, tile):
    return pl.BlockSpec((None, 1, N_MOD), lambda i: (row_fn(i * tile), 0, 0))


def _in0_kernel(*refs, rope):
    if rope:
        (x_ref, mod_ref, w_ref, sw_ref, sbias_ref, cos_ref, sa_ref, sb_ref,
         q_ref, k_ref, v_ref, s_ref) = refs
    else:
        (x_ref, mod_ref, w_ref, sw_ref, sbias_ref,
         q_ref, k_ref, v_ref, s_ref, k_out_ref, v_out_ref) = refs
    subs = _sub_tiles(x_ref)
    ys = []
    for rows in subs:
        h = (x_ref[rows, :] * (1.0 + _mod(mod_ref, 1)) + _mod(mod_ref, 0)).astype(BF16)
        ys.append(jnp.dot(h, w_ref[...], preferred_element_type=F32))
    for si, (rows, y) in enumerate(zip(subs, ys)):
        if rope:
            cos, sa, sb = cos_ref[rows, :], sa_ref[rows, :], sb_ref[rows, :]
        for j in range(QK_WIDTH // LANES):
            lo, hi = j * LANES, (j + 1) * LANES
            qj = y[:, lo:hi]
            kj = y[:, QK_WIDTH + lo:QK_WIDTH + hi]
            if rope:
                qj = (qj * cos + pltpu.roll(qj, LANES - ROPE_PAIRS, 1) * sa
                      + pltpu.roll(qj, ROPE_PAIRS, 1) * sb)
                kj = (kj * cos + pltpu.roll(kj, LANES - ROPE_PAIRS, 1) * sa
                      + pltpu.roll(kj, ROPE_PAIRS, 1) * sb)
            q_ref[rows, lo:hi] = (qj * Q_SCALE).astype(q_ref.dtype)
            k_ref[rows, lo:hi] = kj.astype(k_ref.dtype)
            if not rope:
                kt = kj.T
                seqs = SUB_TILE // SEQ
                for b in range(seqs):
                    for i in range(2):
                        k_out_ref[si * seqs + b, j, i] = kt[i * HEAD_DIM_A:(i + 1) * HEAD_DIM_A,
                                                            b * SEQ:(b + 1) * SEQ]
        off = 2 * QK_WIDTH
        v = y[:, off:off + WIDTH_A]
        if rope:
            v_ref[:, rows] = v.T.astype(v_ref.dtype)
        else:
            v_ref[rows, :] = v.astype(v_ref.dtype)
            for j in range(N_HEADS_A):
                v_out_ref[rows, j, :] = v[:, j * V_DIM_A:(j + 1) * V_DIM_A]
        u_off = off + WIDTH_A
        g_off = u_off + WIDTH_B
        for g in range(N_GROUPS_B):
            lo, hi = g * GROUP_DIM_B, (g + 1) * GROUP_DIM_B
            gg = y[:, g_off + lo:g_off + hi]
            mu = jnp.mean(gg, axis=-1, keepdims=True)
            gc = gg - mu
            var = jnp.mean(gc * gc, axis=-1, keepdims=True)
            vc = (gc * lax.rsqrt(var + LN_EPS)).astype(BF16)
            w = sw_ref[g]
            bias = jnp.broadcast_to(sbias_ref[:, g:g + 1], (CHUNK, GROUP_DIM_B))
            for c in range(SUB_TILE // CHUNK):
                r0, r1 = c * CHUNK, (c + 1) * CHUNK
                mixed = jnp.dot(w, vc[r0:r1], preferred_element_type=F32) + bias
                s_ref[rows.start + r0:rows.start + r1, lo:hi] = (
                    y[r0:r1, u_off + lo:u_off + hi] * mixed).astype(s_ref.dtype)


def _in_proj0(x, mods, w_in, sgu_w, sgu_bt, row_fn, rope_tabs):
    n = x.shape[0]
    t = TOK_TILE
    tok = lambda width: pl.BlockSpec((t, width), lambda i: (i, 0))
    in_specs = [tok(D_MODEL), _mod_spec(row_fn, t), _const_spec((D_MODEL, IN_WIDTH_0)),
                _const_spec((N_GROUPS_B, CHUNK, CHUNK)), _const_spec((CHUNK, N_GROUPS_B))]
    args = [x, mods, w_in, sgu_w, sgu_bt]
    v_spec = tok(WIDTH_A)
    v_shape = (n, WIDTH_A)
    if rope_tabs is not None:
        blocks_per_seq = DEC_SEQ // t
        tab = pl.BlockSpec((t, LANES), lambda i: (i % blocks_per_seq, 0))
        in_specs += [tab, tab, tab]
        args += list(rope_tabs)
        v_spec = pl.BlockSpec((WIDTH_A, t), lambda i: (i // blocks_per_seq, i % blocks_per_seq))
        v_shape = (n // DEC_SEQ * WIDTH_A, DEC_SEQ)
    out_shape = [jax.ShapeDtypeStruct((n, QK_WIDTH), BF16),
                 jax.ShapeDtypeStruct((n, QK_WIDTH), BF16),
                 jax.ShapeDtypeStruct(v_shape, BF16),
                 jax.ShapeDtypeStruct((n, WIDTH_B), BF16)]
    out_specs = [tok(QK_WIDTH), tok(QK_WIDTH), v_spec, tok(WIDTH_B)]
    if rope_tabs is None:
        out_shape += [jax.ShapeDtypeStruct((n // SEQ, N_HEADS_A, 2, HEAD_DIM_A, SEQ), F32),
                      jax.ShapeDtypeStruct((n, N_HEADS_A, V_DIM_A), F32)]
        out_specs += [pl.BlockSpec((t // SEQ, N_HEADS_A, 2, HEAD_DIM_A, SEQ),
                                   lambda i: (i, 0, 0, 0, 0)),
                      pl.BlockSpec((t, N_HEADS_A, V_DIM_A), lambda i: (i, 0, 0))]
    return pl.pallas_call(
        functools.partial(_in0_kernel, rope=rope_tabs is not None),
        grid=(n // t,),
        in_specs=in_specs,
        out_specs=out_specs,
        out_shape=out_shape,
        compiler_params=_params("parallel"),
        name="in_proj0",
    )(*args)


def _lambda(lp_ref):
    lp = lp_ref[...]
    a = jnp.sum(lp[0:1] * lp[1:2], axis=1, keepdims=True)
    b = jnp.sum(lp[2:3] * lp[3:4], axis=1, keepdims=True)
    return jnp.exp(a) - jnp.exp(b) + LAMBDA_INIT_0


def _stack_halves(qh):
    lane = lax.broadcasted_iota(jnp.int32, qh.shape, 1)
    zero = jnp.zeros_like(qh)
    return jnp.concatenate([jnp.where(lane < HEAD_DIM_A, qh, zero),
                            jnp.where(lane >= HEAD_DIM_A, qh, zero)], axis=0)


def _scores(qq, k):
    return lax.dot_general(qq, k, (((1,), (1,)), ((), ())), preferred_element_type=F32)


def _head_out(pv, l, lam, gain, rows):
    o = pv[:rows] / l[:rows] - lam * (pv[rows:] / l[rows:])
    ms = jnp.mean(o * o, axis=-1, keepdims=True)
    return o * lax.rsqrt(ms + LN_EPS) * gain * (1.0 - LAMBDA_INIT_0)


def _attn_ctx_kernel(q_ref, k_ref, v_ref, lp_ref, gain_ref, o_ref):
    lam = _lambda(lp_ref)
    gain = gain_ref[...]
    rows = q_ref.shape[0]
    for h in range(N_HEADS_A):
        lo, hi = h * LANES, (h + 1) * LANES
        qq = _stack_halves(q_ref[:, lo:hi])
        s = _scores(qq, k_ref[:, lo:hi])
        p = jnp.exp2(s - jnp.max(s, axis=-1, keepdims=True))
        l = jnp.sum(p, axis=-1, keepdims=True)
        pv = jnp.dot(p.astype(BF16), v_ref[:, lo:hi], preferred_element_type=F32)
        o_ref[:, lo:hi] = _head_out(pv, l, lam, gain, rows).astype(o_ref.dtype)


def _attn_ctx(q, k, v, lam_params, gain):
    n = q.shape[0]
    blk = lambda: pl.BlockSpec((SEQ, QK_WIDTH), lambda b: (b, 0))
    return pl.pallas_call(
        _attn_ctx_kernel,
        grid=(n // SEQ,),
        in_specs=[blk(), blk(), blk(), _const_spec((4, HEAD_DIM_A)), _const_spec((1, V_DIM_A))],
        out_specs=blk(),
        out_shape=jax.ShapeDtypeStruct((n, WIDTH_A), BF16),
        compiler_params=_params("parallel"),
        name="attn_ctx",
    )(q, k, v, lam_params, gain)


def _attn_lat_kernel(q_ref, k_ref, vt_ref, ck_ref, cvt_ref, lp_ref, gain_ref, o_ref):
    lam = _lambda(lp_ref)
    gain = gain_ref[...]
    rows = ATT_SUB_BLOCK
    chunks = [(None, PAST_LEN)] + [(s, ATT_KEY_CHUNK) for s in range(0, DEC_SEQ, ATT_KEY_CHUNK)]
    ones = jnp.ones((BF16_SUBLANES, ATT_KEY_CHUNK), BF16)
    units = [(sub, h, c) for sub in range(q_ref.shape[0] // rows)
             for h in range(N_HEADS_A) for c in range(len(chunks))]
    qq_of = {}
    scores = {}

    def issue_scores(unit):
        sub, h, c = unit
        lo, hi = h * LANES, (h + 1) * LANES
        if c == 0:
            qq_of[sub, h] = _stack_halves(q_ref[sub * rows:(sub + 1) * rows, lo:hi])
        start, size = chunks[c]
        keys = ck_ref[:, lo:hi].astype(BF16) if start is None else k_ref[start:start + size, lo:hi]
        scores[unit] = _scores(keys, qq_of[sub, h])

    for unit in units[:ATT_LOOKAHEAD]:
        issue_scores(unit)
    m = acc = None
    for i, unit in enumerate(units):
        if i + ATT_LOOKAHEAD < len(units):
            issue_scores(units[i + ATT_LOOKAHEAD])
        sub, h, c = unit
        lo, hi = h * LANES, (h + 1) * LANES
        start, size = chunks[c]
        vals_t = cvt_ref[lo:hi, :] if start is None else vt_ref[lo:hi, start:start + size]
        st = scores.pop(unit)
        cm = jnp.max(st, axis=0, keepdims=True)
        m_new = cm if c == 0 else jnp.maximum(m, cm)
        pt = jnp.exp2(st - m_new).astype(BF16)
        part = jnp.dot(jnp.concatenate([vals_t, ones[:, :size]], axis=0), pt,
                       preferred_element_type=F32)
        acc = part if c == 0 else acc * jnp.exp2(m - m_new) + part
        m = m_new
        if c == len(chunks) - 1:
            pvt = acc[:V_DIM_A]
            l = acc[V_DIM_A:V_DIM_A + 1]
            ot = pvt[:, :rows] / l[:, :rows] - lam * (pvt[:, rows:] / l[:, rows:])
            ms = jnp.mean(ot * ot, axis=0, keepdims=True)
            on = ot * lax.rsqrt(ms + LN_EPS) * gain * (1.0 - LAMBDA_INIT_0)
            o_ref[sub * rows:(sub + 1) * rows, lo:hi] = on.T.astype(o_ref.dtype)


def _attn_lat(q, k, vt, cache_k, cache_vt, lam_params, gain_col):
    n = q.shape[0]
    nb = n // DEC_SEQ
    qb = DEC_SEQ // ATT_Q_BLOCK
    q_spec = pl.BlockSpec((ATT_Q_BLOCK, QK_WIDTH), lambda b, j: (b * qb + j, 0))
    k_spec = pl.BlockSpec((DEC_SEQ, QK_WIDTH), lambda b, j: (b, 0))
    vt_spec = pl.BlockSpec((WIDTH_A, DEC_SEQ), lambda b, j: (b, 0))
    ck_spec = pl.BlockSpec((PAST_LEN, QK_WIDTH), lambda b, j: (b, 0))
    cvt_spec = pl.BlockSpec((WIDTH_A, PAST_LEN), lambda b, j: (b, 0))
    return pl.pallas_call(
        _attn_lat_kernel,
        grid=(nb, qb),
        in_specs=[q_spec, k_spec, vt_spec, ck_spec, cvt_spec,
                  _const_spec((4, HEAD_DIM_A)), _const_spec((V_DIM_A, 1))],
        out_specs=q_spec,
        out_shape=jax.ShapeDtypeStruct((n, WIDTH_A), BF16),
        compiler_params=_params("parallel", "arbitrary"),
        name="attn_lat",
    )(q, k, vt, cache_k, cache_vt, lam_params, gain_col)


def _conv1_kernel(xp_ref, x_ref, xn_ref, mod_ref, wi_ref, cw_ref, o_ref, z_scr, *, seq_len):
    t = x_ref.shape[0]
    halo = SUBLANES
    xin = jnp.concatenate([xp_ref[...], x_ref[...], xn_ref[...]], axis=0)
    h = (xin * (1.0 + _mod(mod_ref, 1)) + _mod(mod_ref, 0)).astype(BF16)
    ys = [jnp.dot(h[i * SUB_TILE:(i + 1) * SUB_TILE + 2 * halo], wi_ref[...],
                  preferred_element_type=F32) for i in range(t // SUB_TILE)]
    cw = cw_ref[...]
    for i, y in enumerate(ys):
        z_scr[i] = y[:, WIDTH_C:2 * WIDTH_C] * y[:, 2 * WIDTH_C:]
        first = pl.program_id(0) * t + i * SUB_TILE
        pos = (first + lax.broadcasted_iota(jnp.int32, (SUB_TILE, 1), 0)) % seq_len
        z_prev = jnp.where(pos != 0, z_scr[i, halo - 1:halo - 1 + SUB_TILE, :], 0.0)
        z_next = jnp.where(pos != seq_len - 1, z_scr[i, halo + 1:halo + 1 + SUB_TILE, :], 0.0)
        conv = (z_prev * cw[0:1] + z_scr[i, halo:halo + SUB_TILE, :] * cw[1:2]
                + z_next * cw[2:3])
        o_ref[i * SUB_TILE:(i + 1) * SUB_TILE, :] = (
            y[halo:halo + SUB_TILE, 0:WIDTH_C] * conv).astype(o_ref.dtype)


def _conv1(x, mods, row_fn, w_in, conv_w, seq_len):
    n = x.shape[0]
    t = TOK_TILE
    tiles_per_block = t // SUBLANES
    last = n // SUBLANES - 1
    tok = pl.BlockSpec((t, D_MODEL), lambda i: (i, 0))
    prev = pl.BlockSpec((SUBLANES, D_MODEL),
                        lambda i: (jnp.maximum(i * tiles_per_block - 1, 0), 0))
    nxt = pl.BlockSpec((SUBLANES, D_MODEL),
                       lambda i: (jnp.minimum((i + 1) * tiles_per_block, last), 0))
    return pl.pallas_call(
        functools.partial(_conv1_kernel, seq_len=seq_len),
        grid=(n // t,),
        in_specs=[prev, tok, nxt, _mod_spec(row_fn, t),
                  _const_spec((D_MODEL, 3 * WIDTH_C)), _const_spec((3, WIDTH_C))],
        out_specs=pl.BlockSpec((t, WIDTH_C), lambda i: (i, 0)),
        out_shape=jax.ShapeDtypeStruct((n, WIDTH_C), BF16),
        scratch_shapes=[pltpu.VMEM((t // SUB_TILE, SUB_TILE + 2 * SUBLANES, WIDTH_C), F32)],
        compiler_params=_params("parallel"),
        name="conv1",
    )(x, x, x, mods, w_in, conv_w)


def _post_kernel(*refs, n_pieces):
    pieces = refs[:n_pieces]
    (x_ref, mod_ref, wo_ref, g1_ref, b1_ref, w1_ref, w2_ref, g2_ref, b2_ref, o_ref) = refs[n_pieces:]
    subs = _sub_tiles(x_ref)
    mixed = []
    for rows in subs:
        out = None
        k0 = 0
        for p_ref in pieces:
            k1 = k0 + p_ref.shape[1]
            part = jnp.dot(p_ref[rows, :], wo_ref[k0:k1, :], preferred_element_type=F32)
            out = part if out is None else out + part
            k0 = k1
        mixed.append(_layer_norm(ALPHA * x_ref[rows, :] + _mod(mod_ref, 2) * out,
                                 g1_ref[...], b1_ref[...]))
    for rows, x in zip(subs, mixed):
        h = (x * (1.0 + _mod(mod_ref, 4)) + _mod(mod_ref, 3)).astype(BF16)
        f = None
        for j in range(D_FF // FF_CHUNK):
            lo, hi = j * FF_CHUNK, (j + 1) * FF_CHUNK
            hid = jnp.dot(h, w1_ref[:, lo:hi], preferred_element_type=F32)
            hid = jnp.square(jnp.maximum(hid, 0.0)).astype(BF16)
            part = jnp.dot(hid, w2_ref[lo:hi, :], preferred_element_type=F32)
            f = part if f is None else f + part
        o_ref[rows, :] = _layer_norm(ALPHA * x + _mod(mod_ref, 5) * f, g2_ref[...], b2_ref[...])


def _post(pieces, x, mods, row_fn, w_out, g1, b1, w1, w2, g2, b2):
    n = x.shape[0]
    t = TOK_TILE
    tok = lambda width: pl.BlockSpec((t, width), lambda i: (i, 0))
    vec = _const_spec((1, D_MODEL))
    return pl.pallas_call(
        functools.partial(_post_kernel, n_pieces=len(pieces)),
        grid=(n // t,),
        in_specs=[tok(p.shape[1]) for p in pieces] + [
            tok(D_MODEL), _mod_spec(row_fn, t), _const_spec(w_out.shape), vec, vec,
            _const_spec((D_MODEL, D_FF)), _const_spec((D_FF, D_MODEL)), vec, vec],
        out_specs=tok(D_MODEL),
        out_shape=jax.ShapeDtypeStruct((n, D_MODEL), F32),
        compiler_params=_params("parallel"),
        name="post",
    )(*pieces, x, mods, w_out, g1, b1, w1, w2, g2, b2)


def _rope_tables(n):
    rows = n // GRID_W
    row = jnp.repeat(jnp.arange(rows, dtype=F32), GRID_W)
    col = jnp.tile(jnp.arange(GRID_W, dtype=F32), rows)
    inv = 1.0 / (ROPE_BASE ** (jnp.arange(ROPE_PAIRS, dtype=F32) / ROPE_PAIRS))
    ang_r, ang_c = row[:, None] * inv, col[:, None] * inv
    zero = jnp.zeros_like(ang_r)
    reps = LANES // HEAD_DIM_A
    cos = jnp.tile(jnp.concatenate([jnp.cos(ang_r)] * 2 + [jnp.cos(ang_c)] * 2, axis=1), (1, reps))
    sa = jnp.tile(jnp.concatenate([-jnp.sin(ang_r), zero, -jnp.sin(ang_c), zero], axis=1), (1, reps))
    sb = jnp.tile(jnp.concatenate([zero, jnp.sin(ang_r), zero, jnp.sin(ang_c)], axis=1), (1, reps))
    return cos, sa, sb


def kernel(x_prompt, x_sample, cache_k0, cache_v0, c, c_ctx, w_mod0, b_mod0, w_in0, lambda_q1_0, lambda_k1_0, lambda_q2_0, lambda_k2_0, subln_g0, sgu_w0, sgu_b0, w_out0, ln_mix_g0, ln_mix_b0, w_ff1_0, w_ff2_0, ln_ff_g0, ln_ff_b0, w_mod1, b_mod1, w_in1, conv_w1, w_out1, ln_mix_g1, ln_mix_b1, w_ff1_1, w_ff2_1, ln_ff_g1, ln_ff_b1):
    batch, seq, d = x_prompt.shape
    dec_batch, dec_seq, _ = x_sample.shape
    assert (seq, dec_seq, d) == (SEQ, DEC_SEQ, D_MODEL) and 1 + dec_batch <= MOD_ROWS
    row = lambda v: v.reshape(1, -1)

    cvec = jnp.concatenate([c_ctx[None, :], c, jnp.zeros((MOD_ROWS - 1 - dec_batch, d), F32)], axis=0)
    mods0, mods1 = _modulations(cvec.T, w_mod0, row(b_mod0), w_mod1, row(b_mod1), 1 + dec_batch)
    mods0 = mods0.reshape(MOD_ROWS, 1, N_MOD)
    mods1 = mods1.reshape(MOD_ROWS, 1, N_MOD)
    ctx_row = lambda tok: 0
    lat_row = lambda tok: 1 + tok // DEC_SEQ

    xp = x_prompt.reshape(batch * seq, d)
    xs = x_sample.reshape(dec_batch * dec_seq, d)

    w_in0_b = w_in0.astype(BF16)
    sgu = (sgu_w0.astype(BF16), sgu_b0.T)
    lam_params = jnp.stack([lambda_q1_0, lambda_k1_0, lambda_q2_0, lambda_k2_0])
    gain = row(subln_g0)
    qp, kp, vp, sp, new_k0, new_v0 = _in_proj0(xp, mods0, w_in0_b, *sgu, ctx_row, None)
    qs, ks, vs, ss = _in_proj0(xs, mods0, w_in0_b, *sgu, lat_row, _rope_tables(dec_seq))
    ap = _attn_ctx(qp, kp, vp, lam_params, gain)
    cache_vt = jnp.swapaxes(cache_v0.reshape(dec_batch, PAST_LEN, WIDTH_A), 1, 2)
    a_s = _attn_lat(qs, ks, vs, cache_k0.reshape(dec_batch * PAST_LEN, QK_WIDTH),
                    cache_vt.reshape(dec_batch * WIDTH_A, PAST_LEN).astype(BF16),
                    lam_params, gain.reshape(V_DIM_A, 1))
    post0_w = (w_out0.astype(BF16), row(ln_mix_g0), row(ln_mix_b0),
               w_ff1_0.astype(BF16), w_ff2_0.astype(BF16), row(ln_ff_g0), row(ln_ff_b0))
    xp = _post([ap, sp], xp, mods0, ctx_row, *post0_w)
    xs = _post([a_s, ss], xs, mods0, lat_row, *post0_w)

    w_in1_b = w_in1.astype(BF16)
    gp = _conv1(xp, mods1, ctx_row, w_in1_b, conv_w1, seq_len=seq)
    gs = _conv1(xs, mods1, lat_row, w_in1_b, conv_w1, seq_len=dec_seq)
    post1_w = (w_out1.astype(BF16), row(ln_mix_g1), row(ln_mix_b1),
               w_ff1_1.astype(BF16), w_ff2_1.astype(BF16), row(ln_ff_g1), row(ln_ff_b1))
    xp = _post([gp], xp, mods1, ctx_row, *post1_w)
    xs = _post([gs], xs, mods1, lat_row, *post1_w)

    return (xp.reshape(batch, seq, d), xs.reshape(dec_batch, dec_seq, d),
            jnp.transpose(new_k0, (0, 4, 1, 2, 3)),
            new_v0.reshape(batch, seq, N_HEADS_A, V_DIM_A))
```

```python
import functools
import math

import jax
import jax.numpy as jnp
from jax import lax
from jax.experimental import pallas as pl
from jax.experimental.pallas import tpu as pltpu

F32 = jnp.float32
BF16 = jnp.bfloat16

D_MODEL = 1024
DEPTH = 2
SEQ = 256
DEC_SEQ = 4096
PAST_LEN = 256
GRID_W = 64
N_HEADS_A = 4
HEAD_DIM_A = 64
V_DIM_A = 2 * HEAD_DIM_A
QK_WIDTH = N_HEADS_A * 2 * HEAD_DIM_A
WIDTH_A = N_HEADS_A * V_DIM_A
N_GROUPS_B = 4
CHUNK = 128
GROUP_DIM_B = 128
WIDTH_B = N_GROUPS_B * GROUP_DIM_B
IN_WIDTH_0 = 2 * QK_WIDTH + WIDTH_A + 2 * WIDTH_B
WIDTH_C = D_MODEL
D_FF = 4 * D_MODEL
ROPE_BASE = 10000.0
ROPE_PAIRS = HEAD_DIM_A // 4
LN_EPS = 1e-5
ALPHA = (2 * DEPTH) ** 0.25
LAMBDA_INIT_0 = 0.8 - 0.6 * math.exp(-0.3 * 0)
Q_SCALE = HEAD_DIM_A ** -0.5 * math.log2(math.e)

LANES = 128
SUBLANES = 8
BF16_SUBLANES = 16
VMEM_LIMIT = 56 * 2 ** 20
N_MOD = 6 * D_MODEL
MOD_ROWS = 8
TOK_TILE = 1024
SUB_TILE = 512
CTX_SEQS = 4
CTX_LOOKAHEAD = 3
ATT_Q_BLOCK = 256
ATT_SUB_BLOCK = 128
ATT_KEY_CHUNK = 256
ATT_LOOKAHEAD = 5
FF_CHUNK = 1024
CONV_CHUNK = 256
MOD_TILE = 512


def _const_spec(shape):
    zeros = (0,) * len(shape)
    return pl.BlockSpec(shape, lambda *_: zeros, pipeline_mode=pl.Buffered(1))


def _params(*sem):
    return pltpu.CompilerParams(dimension_semantics=sem, vmem_limit_bytes=VMEM_LIMIT)


def _layer_norm(x, g, b):
    mu = jnp.mean(x, axis=-1, keepdims=True)
    xc = x - mu
    var = jnp.mean(xc * xc, axis=-1, keepdims=True)
    return xc * lax.rsqrt(var + LN_EPS) * g + b


def _mod(mod_ref, idx):
    return mod_ref[:, idx * D_MODEL:(idx + 1) * D_MODEL]


def _sub_tiles(ref):
    return [slice(i * SUB_TILE, (i + 1) * SUB_TILE) for i in range(ref.shape[0] // SUB_TILE)]


def _mod_kernel(c_ref, w0_ref, b0_ref, w1_ref, b1_ref, o0_ref, o1_ref):
    c = c_ref[...]
    s = (c * jax.nn.sigmoid(c)).astype(BF16)
    for w_ref, b_ref, o_ref in ((w0_ref, b0_ref, o0_ref), (w1_ref, b1_ref, o1_ref)):
        o_ref[...] = jnp.dot(s, w_ref[...].astype(BF16),
                             preferred_element_type=F32) + b_ref[...]


def _modulations(cvec, w0, b0, w1, b1):
    w_spec = pl.BlockSpec((D_MODEL, MOD_TILE), lambda j: (0, j))
    v_spec = pl.BlockSpec((1, MOD_TILE), lambda j: (0, j))
    o_spec = pl.BlockSpec((MOD_ROWS, MOD_TILE), lambda j: (0, j))
    out = jax.ShapeDtypeStruct((MOD_ROWS, N_MOD), F32)
    return pl.pallas_call(
        _mod_kernel,
        grid=(N_MOD // MOD_TILE,),
        in_specs=[_const_spec((MOD_ROWS, D_MODEL)), w_spec, v_spec, w_spec, v_spec],
        out_specs=[o_spec, o_spec],
        out_shape=[out, out],
        compiler_params=_params("parallel"),
        name="adaln_mod",
    )(cvec, w0, b0, w1, b1)


def _mod_spec(row_fn, tile):
    return pl.BlockSpec((None, 1, N_MOD), lambda i: (row_fn(i * tile), 0, 0))


def _in0_kernel(*refs, rope):
    if rope:
        (x_ref, mod_ref, w_ref, sw_ref, sbias_ref, cos_ref, sa_ref, sb_ref,
         q_ref, k_ref, v_ref, s_ref) = refs
    else:
        (x_ref, mod_ref, w_ref, sw_ref, sbias_ref,
         q_ref, k_ref, v_ref, s_ref, k_out_ref, v_out_ref) = refs
    subs = _sub_tiles(x_ref)
    ys = []
    for rows in subs:
        h = (x_ref[rows, :] * (1.0 + _mod(mod_ref, 1)) + _mod(mod_ref, 0)).astype(BF16)
        ys.append(jnp.dot(h, w_ref[...], preferred_element_type=F32))
    for si, (rows, y) in enumerate(zip(subs, ys)):
        if rope:
            cos, sa, sb = cos_ref[rows, :], sa_ref[rows, :], sb_ref[rows, :]
        for j in range(QK_WIDTH // LANES):
            lo, hi = j * LANES, (j + 1) * LANES
            qj = y[:, lo:hi]
            kj = y[:, QK_WIDTH + lo:QK_WIDTH + hi]
            if rope:
                qj = (qj * cos + pltpu.roll(qj, LANES - ROPE_PAIRS, 1) * sa
                      + pltpu.roll(qj, ROPE_PAIRS, 1) * sb)
                kj = (kj * cos + pltpu.roll(kj, LANES - ROPE_PAIRS, 1) * sa
                      + pltpu.roll(kj, ROPE_PAIRS, 1) * sb)
            q_ref[rows, lo:hi] = (qj * Q_SCALE).astype(q_ref.dtype)
            k_ref[rows, lo:hi] = kj.astype(k_ref.dtype)
            if not rope:
                kt = kj.T
                seqs = SUB_TILE // SEQ
                for b in range(seqs):
                    for i in range(2):
                        k_out_ref[si * seqs + b, j, i] = kt[i * HEAD_DIM_A:(i + 1) * HEAD_DIM_A,
                                                            b * SEQ:(b + 1) * SEQ]
        off = 2 * QK_WIDTH
        v = y[:, off:off + WIDTH_A]
        if rope:
            v_ref[:, rows] = v.T.astype(v_ref.dtype)
        else:
            v_ref[rows, :] = v.astype(v_ref.dtype)
            for j in range(N_HEADS_A):
                v_out_ref[rows, j, :] = v[:, j * V_DIM_A:(j + 1) * V_DIM_A]
        u_off = off + WIDTH_A
        g_off = u_off + WIDTH_B
        for g in range(N_GROUPS_B):
            lo, hi = g * GROUP_DIM_B, (g + 1) * GROUP_DIM_B
            gg = y[:, g_off + lo:g_off + hi]
            mu = jnp.mean(gg, axis=-1, keepdims=True)
            gc = gg - mu
            var = jnp.mean(gc * gc, axis=-1, keepdims=True)
            vc = (gc * lax.rsqrt(var + LN_EPS)).astype(BF16)
            w = sw_ref[g]
            bias = jnp.broadcast_to(sbias_ref[:, g:g + 1], (CHUNK, GROUP_DIM_B))
            for c in range(SUB_TILE // CHUNK):
                r0, r1 = c * CHUNK, (c + 1) * CHUNK
                mixed = jnp.dot(w, vc[r0:r1], preferred_element_type=F32) + bias
                s_ref[rows.start + r0:rows.start + r1, lo:hi] = (
                    y[r0:r1, u_off + lo:u_off + hi] * mixed).astype(s_ref.dtype)


def _in_proj0(x, mods, w_in, sgu_w, sgu_bt, row_fn, rope_tabs):
    n = x.shape[0]
    t = TOK_TILE
    tok = lambda width: pl.BlockSpec((t, width), lambda i: (i, 0))
    in_specs = [tok(D_MODEL), _mod_spec(row_fn, t), _const_spec((D_MODEL, IN_WIDTH_0)),
                _const_spec((N_GROUPS_B, CHUNK, CHUNK)), _const_spec((CHUNK, N_GROUPS_B))]
    args = [x, mods, w_in, sgu_w, sgu_bt]
    v_spec = tok(WIDTH_A)
    v_shape = (n, WIDTH_A)
    if rope_tabs is not None:
        blocks_per_seq = DEC_SEQ // t
        tab = pl.BlockSpec((t, LANES), lambda i: (i % blocks_per_seq, 0))
        in_specs += [tab, tab, tab]
        args += list(rope_tabs)
        v_spec = pl.BlockSpec((WIDTH_A, t), lambda i: (i // blocks_per_seq, i % blocks_per_seq))
        v_shape = (n // DEC_SEQ * WIDTH_A, DEC_SEQ)
    out_shape = [jax.ShapeDtypeStruct((n, QK_WIDTH), BF16),
                 jax.ShapeDtypeStruct((n, QK_WIDTH), BF16),
                 jax.ShapeDtypeStruct(v_shape, BF16),
                 jax.ShapeDtypeStruct((n, WIDTH_B), BF16)]
    out_specs = [tok(QK_WIDTH), tok(QK_WIDTH), v_spec, tok(WIDTH_B)]
    if rope_tabs is None:
        out_shape += [jax.ShapeDtypeStruct((n // SEQ, N_HEADS_A, 2, HEAD_DIM_A, SEQ), F32),
                      jax.ShapeDtypeStruct((n, N_HEADS_A, V_DIM_A), F32)]
        out_specs += [pl.BlockSpec((t // SEQ, N_HEADS_A, 2, HEAD_DIM_A, SEQ),
                                   lambda i: (i, 0, 0, 0, 0)),
                      pl.BlockSpec((t, N_HEADS_A, V_DIM_A), lambda i: (i, 0, 0))]
    return pl.pallas_call(
        functools.partial(_in0_kernel, rope=rope_tabs is not None),
        grid=(n // t,),
        in_specs=in_specs,
        out_specs=out_specs,
        out_shape=out_shape,
        compiler_params=_params("parallel"),
        name="in_proj0",
    )(*args)


def _lambda(lp_ref):
    lp = lp_ref[...]
    a = jnp.sum(lp[0:1] * lp[1:2], axis=1, keepdims=True)
    b = jnp.sum(lp[2:3] * lp[3:4], axis=1, keepdims=True)
    return jnp.exp(a) - jnp.exp(b) + LAMBDA_INIT_0


def _stack_halves(qh):
    lane = lax.broadcasted_iota(jnp.int32, qh.shape, 1)
    zero = jnp.zeros_like(qh)
    return jnp.concatenate([jnp.where(lane < HEAD_DIM_A, qh, zero),
                            jnp.where(lane >= HEAD_DIM_A, qh, zero)], axis=0)


def _scores(qq, k):
    return lax.dot_general(qq, k, (((1,), (1,)), ((), ())), preferred_element_type=F32)


def _head_out(pv, l, lam, gain, rows):
    o = pv[:rows] / l[:rows] - lam * (pv[rows:] / l[rows:])
    ms = jnp.mean(o * o, axis=-1, keepdims=True)
    return o * lax.rsqrt(ms + LN_EPS) * gain * (1.0 - LAMBDA_INIT_0)


def _attn_ctx_kernel(q_ref, k_ref, v_ref, lp_ref, gain_ref, o_ref):
    lam = _lambda(lp_ref)
    gain = gain_ref[...]
    units = [(b, h) for b in range(q_ref.shape[0] // SEQ) for h in range(N_HEADS_A)]
    scores = {}

    def issue_scores(unit):
        b, h = unit
        rows = slice(b * SEQ, (b + 1) * SEQ)
        lo, hi = h * LANES, (h + 1) * LANES
        scores[unit] = _scores(_stack_halves(q_ref[rows, lo:hi]), k_ref[rows, lo:hi])

    for unit in units[:CTX_LOOKAHEAD]:
        issue_scores(unit)
    for i, unit in enumerate(units):
        if i + CTX_LOOKAHEAD < len(units):
            issue_scores(units[i + CTX_LOOKAHEAD])
        b, h = unit
        rows = slice(b * SEQ, (b + 1) * SEQ)
        lo, hi = h * LANES, (h + 1) * LANES
        s = scores.pop(unit)
        p = jnp.exp2(s - jnp.max(s, axis=-1, keepdims=True))
        l = jnp.sum(p, axis=-1, keepdims=True)
        pv = jnp.dot(p.astype(BF16), v_ref[rows, lo:hi], preferred_element_type=F32)
        o_ref[rows, lo:hi] = _head_out(pv, l, lam, gain, SEQ).astype(o_ref.dtype)


def _attn_ctx(q, k, v, lam_params, gain):
    n = q.shape[0]
    blk = lambda: pl.BlockSpec((CTX_SEQS * SEQ, QK_WIDTH), lambda b: (b, 0))
    return pl.pallas_call(
        _attn_ctx_kernel,
        grid=(n // (CTX_SEQS * SEQ),),
        in_specs=[blk(), blk(), blk(), _const_spec((4, HEAD_DIM_A)), _const_spec((1, V_DIM_A))],
        out_specs=blk(),
        out_shape=jax.ShapeDtypeStruct((n, WIDTH_A), BF16),
        compiler_params=_params("parallel"),
        name="attn_ctx",
    )(q, k, v, lam_params, gain)


def _attn_lat_kernel(q_ref, k_ref, vt_ref, ck_ref, cvt_ref, lp_ref, gain_ref, o_ref):
    lam = _lambda(lp_ref)
    gain = gain_ref[...]
    rows = ATT_SUB_BLOCK
    chunks = [(None, PAST_LEN)] + [(s, ATT_KEY_CHUNK) for s in range(0, DEC_SEQ, ATT_KEY_CHUNK)]
    ones = jnp.ones((BF16_SUBLANES, ATT_KEY_CHUNK), BF16)
    units = [(sub, h, c) for sub in range(q_ref.shape[0] // rows)
             for h in range(N_HEADS_A) for c in range(len(chunks))]
    qq_of = {}
    scores = {}

    def issue_scores(unit):
        sub, h, c = unit
        lo, hi = h * LANES, (h + 1) * LANES
        if c == 0:
            qq_of[sub, h] = _stack_halves(q_ref[sub * rows:(sub + 1) * rows, lo:hi])
        start, size = chunks[c]
        keys = ck_ref[:, lo:hi].astype(BF16) if start is None else k_ref[start:start + size, lo:hi]
        scores[unit] = _scores(keys, qq_of[sub, h])

    for unit in units[:ATT_LOOKAHEAD]:
        issue_scores(unit)
    m = acc = None
    for i, unit in enumerate(units):
        if i + ATT_LOOKAHEAD < len(units):
            issue_scores(units[i + ATT_LOOKAHEAD])
        sub, h, c = unit
        lo, hi = h * LANES, (h + 1) * LANES
        start, size = chunks[c]
        vals_t = cvt_ref[lo:hi, :] if start is None else vt_ref[lo:hi, start:start + size]
        st = scores.pop(unit)
        cm = jnp.max(st, axis=0, keepdims=True)
        m_new = cm if c == 0 else jnp.maximum(m, cm)
        pt = jnp.exp2(st - m_new).astype(BF16)
        part = jnp.dot(jnp.concatenate([vals_t, ones[:, :size]], axis=0), pt,
                       preferred_element_type=F32)
        acc = part if c == 0 else acc * jnp.exp2(m - m_new) + part
        m = m_new
        if c == len(chunks) - 1:
            pvt = acc[:V_DIM_A]
            l = acc[V_DIM_A:V_DIM_A + 1]
            ot = pvt[:, :rows] / l[:, :rows] - lam * (pvt[:, rows:] / l[:, rows:])
            ms = jnp.mean(ot * ot, axis=0, keepdims=True)
            on = ot * lax.rsqrt(ms + LN_EPS) * gain * (1.0 - LAMBDA_INIT_0)
            o_ref[sub * rows:(sub + 1) * rows, lo:hi] = on.T.astype(o_ref.dtype)


def _attn_lat(q, k, vt, cache_k, cache_vt, lam_params, gain_col):
    n = q.shape[0]
    nb = n // DEC_SEQ
    qb = DEC_SEQ // ATT_Q_BLOCK
    q_spec = pl.BlockSpec((ATT_Q_BLOCK, QK_WIDTH), lambda b, j: (b * qb + j, 0))
    k_spec = pl.BlockSpec((DEC_SEQ, QK_WIDTH), lambda b, j: (b, 0))
    vt_spec = pl.BlockSpec((WIDTH_A, DEC_SEQ), lambda b, j: (b, 0))
    ck_spec = pl.BlockSpec((PAST_LEN, QK_WIDTH), lambda b, j: (b, 0))
    cvt_spec = pl.BlockSpec((WIDTH_A, PAST_LEN), lambda b, j: (b, 0))
    return pl.pallas_call(
        _attn_lat_kernel,
        grid=(nb, qb),
        in_specs=[q_spec, k_spec, vt_spec, ck_spec, cvt_spec,
                  _const_spec((4, HEAD_DIM_A)), _const_spec((V_DIM_A, 1))],
        out_specs=q_spec,
        out_shape=jax.ShapeDtypeStruct((n, WIDTH_A), BF16),
        compiler_params=_params("parallel", "arbitrary"),
        name="attn_lat",
    )(q, k, vt, cache_k, cache_vt, lam_params, gain_col)


def _conv1_kernel(xp_ref, x_ref, xn_ref, mod_ref, wi_ref, cw_ref, o_ref, z_scr, *, seq_len):
    t = x_ref.shape[0]
    halo = SUBLANES
    cc = CONV_CHUNK
    n_chunks = WIDTH_C // cc
    xin = jnp.concatenate([xp_ref[...], x_ref[...], xn_ref[...]], axis=0)
    h = (xin * (1.0 + _mod(mod_ref, 1)) + _mod(mod_ref, 0)).astype(BF16)
    pos = (pl.program_id(0) * t + lax.broadcasted_iota(jnp.int32, (t, 1), 0)) % seq_len
    not_first = pos != 0
    not_last = pos != seq_len - 1

    def project(j):
        return jnp.dot(h, wi_ref[:, 3 * cc * j:3 * cc * (j + 1)], preferred_element_type=F32)

    y_next = project(0)
    for j in range(n_chunks):
        y = y_next
        if j + 1 < n_chunks:
            y_next = project(j + 1)
        z_scr[j] = y[:, cc:2 * cc] * y[:, 2 * cc:]
        cw = cw_ref[:, j * cc:(j + 1) * cc]
        z_prev = jnp.where(not_first, z_scr[j, halo - 1:halo - 1 + t, :], 0.0)
        z_next = jnp.where(not_last, z_scr[j, halo + 1:halo + 1 + t, :], 0.0)
        conv = z_prev * cw[0:1] + z_scr[j, halo:halo + t, :] * cw[1:2] + z_next * cw[2:3]
        o_ref[:, j * cc:(j + 1) * cc] = (y[halo:halo + t, 0:cc] * conv).astype(o_ref.dtype)


def _conv1(x, mods, row_fn, w_in, conv_w, seq_len):
    n = x.shape[0]
    t = TOK_TILE
    tiles_per_block = t // SUBLANES
    last = n // SUBLANES - 1
    tok = pl.BlockSpec((t, D_MODEL), lambda i: (i, 0))
    prev = pl.BlockSpec((SUBLANES, D_MODEL),
                        lambda i: (jnp.maximum(i * tiles_per_block - 1, 0), 0))
    nxt = pl.BlockSpec((SUBLANES, D_MODEL),
                       lambda i: (jnp.minimum((i + 1) * tiles_per_block, last), 0))
    return pl.pallas_call(
        functools.partial(_conv1_kernel, seq_len=seq_len),
        grid=(n // t,),
        in_specs=[prev, tok, nxt, _mod_spec(row_fn, t),
                  _const_spec((D_MODEL, 3 * WIDTH_C)), _const_spec((3, WIDTH_C))],
        out_specs=pl.BlockSpec((t, WIDTH_C), lambda i: (i, 0)),
        out_shape=jax.ShapeDtypeStruct((n, WIDTH_C), BF16),
        scratch_shapes=[pltpu.VMEM((WIDTH_C // CONV_CHUNK, t + 2 * SUBLANES, CONV_CHUNK), F32)],
        compiler_params=_params("parallel"),
        name="conv1",
    )(x, x, x, mods, w_in, conv_w)


def _post_kernel(*refs, n_pieces):
    pieces = refs[:n_pieces]
    (x_ref, mod_ref, wo_ref, g1_ref, b1_ref, w1_ref, w2_ref, g2_ref, b2_ref, o_ref) = refs[n_pieces:]
    subs = _sub_tiles(x_ref)
    mixed = []
    for rows in subs:
        out = None
        k0 = 0
        for p_ref in pieces:
            k1 = k0 + p_ref.shape[1]
            part = jnp.dot(p_ref[rows, :], wo_ref[k0:k1, :], preferred_element_type=F32)
            out = part if out is None else out + part
            k0 = k1
        mixed.append(_layer_norm(ALPHA * x_ref[rows, :] + _mod(mod_ref, 2) * out,
                                 g1_ref[...], b1_ref[...]))
    hs = [(x * (1.0 + _mod(mod_ref, 4)) + _mod(mod_ref, 3)).astype(BF16) for x in mixed]
    fs = [None] * len(subs)
    for j in range(D_FF // FF_CHUNK):
        lo, hi = j * FF_CHUNK, (j + 1) * FF_CHUNK
        hids = [jnp.dot(h, w1_ref[:, lo:hi], preferred_element_type=F32) for h in hs]
        for s, hid in enumerate(hids):
            hid = jnp.square(jnp.maximum(hid, 0.0)).astype(BF16)
            part = jnp.dot(hid, w2_ref[lo:hi, :], preferred_element_type=F32)
            fs[s] = part if fs[s] is None else fs[s] + part
    for rows, x, f in zip(subs, mixed, fs):
        o_ref[rows, :] = _layer_norm(ALPHA * x + _mod(mod_ref, 5) * f, g2_ref[...], b2_ref[...])


def _post(pieces, x, mods, row_fn, w_out, g1, b1, w1, w2, g2, b2):
    n = x.shape[0]
    t = TOK_TILE
    tok = lambda width: pl.BlockSpec((t, width), lambda i: (i, 0))
    vec = _const_spec((1, D_MODEL))
    return pl.pallas_call(
        functools.partial(_post_kernel, n_pieces=len(pieces)),
        grid=(n // t,),
        in_specs=[tok(p.shape[1]) for p in pieces] + [
            tok(D_MODEL), _mod_spec(row_fn, t), _const_spec(w_out.shape), vec, vec,
            _const_spec((D_MODEL, D_FF)), _const_spec((D_FF, D_MODEL)), vec, vec],
        out_specs=tok(D_MODEL),
        out_shape=jax.ShapeDtypeStruct((n, D_MODEL), F32),
        compiler_params=_params("parallel"),
        name="post",
    )(*pieces, x, mods, w_out, g1, b1, w1, w2, g2, b2)


def _rope_tables(n):
    rows = n // GRID_W
    row = jnp.repeat(jnp.arange(rows, dtype=F32), GRID_W)
    col = jnp.tile(jnp.arange(GRID_W, dtype=F32), rows)
    inv = 1.0 / (ROPE_BASE ** (jnp.arange(ROPE_PAIRS, dtype=F32) / ROPE_PAIRS))
    ang_r, ang_c = row[:, None] * inv, col[:, None] * inv
    zero = jnp.zeros_like(ang_r)
    reps = LANES // HEAD_DIM_A
    cos = jnp.tile(jnp.concatenate([jnp.cos(ang_r)] * 2 + [jnp.cos(ang_c)] * 2, axis=1), (1, reps))
    sa = jnp.tile(jnp.concatenate([-jnp.sin(ang_r), zero, -jnp.sin(ang_c), zero], axis=1), (1, reps))
    sb = jnp.tile(jnp.concatenate([zero, jnp.sin(ang_r), zero, jnp.sin(ang_c)], axis=1), (1, reps))
    return cos, sa, sb


def kernel(x_prompt, x_sample, cache_k0, cache_v0, c, c_ctx, w_mod0, b_mod0, w_in0, lambda_q1_0, lambda_k1_0, lambda_q2_0, lambda_k2_0, subln_g0, sgu_w0, sgu_b0, w_out0, ln_mix_g0, ln_mix_b0, w_ff1_0, w_ff2_0, ln_ff_g0, ln_ff_b0, w_mod1, b_mod1, w_in1, conv_w1, w_out1, ln_mix_g1, ln_mix_b1, w_ff1_1, w_ff2_1, ln_ff_g1, ln_ff_b1):
    batch, seq, d = x_prompt.shape
    dec_batch, dec_seq, _ = x_sample.shape
    assert (seq, dec_seq, d) == (SEQ, DEC_SEQ, D_MODEL) and 1 + dec_batch <= MOD_ROWS
    row = lambda v: v.reshape(1, -1)

    cvec = jnp.concatenate([c_ctx[None, :], c, jnp.zeros((MOD_ROWS - 1 - dec_batch, d), F32)], axis=0)
    mods0, mods1 = _modulations(cvec, w_mod0, row(b_mod0), w_mod1, row(b_mod1))
    mods0 = mods0.reshape(MOD_ROWS, 1, N_MOD)
    mods1 = mods1.reshape(MOD_ROWS, 1, N_MOD)
    ctx_row = lambda tok: 0
    lat_row = lambda tok: 1 + tok // DEC_SEQ

    xp = x_prompt.reshape(batch * seq, d)
    xs = x_sample.reshape(dec_batch * dec_seq, d)

    w_in0_b = w_in0.astype(BF16)
    sgu = (sgu_w0.astype(BF16), sgu_b0.T)
    lam_params = jnp.stack([lambda_q1_0, lambda_k1_0, lambda_q2_0, lambda_k2_0])
    gain = row(subln_g0)
    qp, kp, vp, sp, new_k0, new_v0 = _in_proj0(xp, mods0, w_in0_b, *sgu, ctx_row, None)
    qs, ks, vs, ss = _in_proj0(xs, mods0, w_in0_b, *sgu, lat_row, _rope_tables(dec_seq))
    ap = _attn_ctx(qp, kp, vp, lam_params, gain)
    cache_vt = jnp.swapaxes(cache_v0.reshape(dec_batch, PAST_LEN, WIDTH_A), 1, 2)
    a_s = _attn_lat(qs, ks, vs, cache_k0.reshape(dec_batch * PAST_LEN, QK_WIDTH),
                    cache_vt.reshape(dec_batch * WIDTH_A, PAST_LEN).astype(BF16),
                    lam_params, gain.reshape(V_DIM_A, 1))
    post0_w = (w_out0.astype(BF16), row(ln_mix_g0), row(ln_mix_b0),
               w_ff1_0.astype(BF16), w_ff2_0.astype(BF16), row(ln_ff_g0), row(ln_ff_b0))
    xp = _post([ap, sp], xp, mods0, ctx_row, *post0_w)
    xs = _post([a_s, ss], xs, mods0, lat_row, *post0_w)

    n_cc = WIDTH_C // CONV_CHUNK
    w_in1_b = jnp.swapaxes(w_in1.reshape(d, 3, n_cc, CONV_CHUNK), 1, 2).reshape(d, 3 * WIDTH_C)
    w_in1_b = w_in1_b.astype(BF16)
    gp = _conv1(xp, mods1, ctx_row, w_in1_b, conv_w1, seq_len=seq)
    gs = _conv1(xs, mods1, lat_row, w_in1_b, conv_w1, seq_len=dec_seq)
    post1_w = (w_out1.astype(BF16), row(ln_mix_g1), row(ln_mix_b1),
               w_ff1_1.astype(BF16), w_ff2_1.astype(BF16), row(ln_ff_g1), row(ln_ff_b1))
    xp = _post([gp], xp, mods1, ctx_row, *post1_w)
    xs = _post([gs], xs, mods1, lat_row, *post1_w)

    return (xp.reshape(batch, seq, d), xs.reshape(dec_batch, dec_seq, d),
            jnp.transpose(new_k0, (0, 4, 1, 2, 3)),
            new_v0.reshape(batch, seq, N_HEADS_A, V_DIM_A))
```

```python
import functools
import math

import jax
import jax.numpy as jnp
from jax import lax
from jax.experimental import pallas as pl
from jax.experimental.pallas import tpu as pltpu

F32 = jnp.float32
BF16 = jnp.bfloat16

D_MODEL = 1024
DEPTH = 2
SEQ = 256
DEC_SEQ = 4096
PAST_LEN = 256
GRID_W = 64
N_HEADS_A = 4
HEAD_DIM_A = 64
V_DIM_A = 2 * HEAD_DIM_A
QK_WIDTH = N_HEADS_A * 2 * HEAD_DIM_A
WIDTH_A = N_HEADS_A * V_DIM_A
N_GROUPS_B = 4
CHUNK = 128
GROUP_DIM_B = 128
WIDTH_B = N_GROUPS_B * GROUP_DIM_B
IN_WIDTH_0 = 2 * QK_WIDTH + WIDTH_A + 2 * WIDTH_B
WIDTH_C = D_MODEL
D_FF = 4 * D_MODEL
ROPE_BASE = 10000.0
ROPE_PAIRS = HEAD_DIM_A // 4
LN_EPS = 1e-5
ALPHA = (2 * DEPTH) ** 0.25
LAMBDA_INIT_0 = 0.8 - 0.6 * math.exp(-0.3 * 0)
Q_SCALE = HEAD_DIM_A ** -0.5 * math.log2(math.e)

LANES = 128
SUBLANES = 8
BF16_SUBLANES = 16
VMEM_LIMIT = 56 * 2 ** 20
N_MOD = 6 * D_MODEL
MOD_ROWS = 8
TOK_TILE = 1024
SUB_TILE = 512
CTX_SEQS = 4
CTX_LOOKAHEAD = 3
ATT_Q_BLOCK = 512
ATT_SUB_BLOCK = 128
ATT_KEY_CHUNK = 256
ATT_LOOKAHEAD = 5
FF_CHUNK = 1024
CONV_CHUNK = 256
MOD_TILE = 512


def _const_spec(shape):
    zeros = (0,) * len(shape)
    return pl.BlockSpec(shape, lambda *_: zeros, pipeline_mode=pl.Buffered(1))


def _params(*sem):
    return pltpu.CompilerParams(dimension_semantics=sem, vmem_limit_bytes=VMEM_LIMIT)


def _layer_norm(x, g, b):
    mu = jnp.mean(x, axis=-1, keepdims=True)
    xc = x - mu
    var = jnp.mean(xc * xc, axis=-1, keepdims=True)
    return xc * lax.rsqrt(var + LN_EPS) * g + b


def _mod(mod_ref, idx):
    return mod_ref[:, idx * D_MODEL:(idx + 1) * D_MODEL]


def _sub_tiles(ref):
    return [slice(i * SUB_TILE, (i + 1) * SUB_TILE) for i in range(ref.shape[0] // SUB_TILE)]


def _mod_kernel(c_ref, w0_ref, b0_ref, w1_ref, b1_ref, o0_ref, o1_ref):
    c = c_ref[...]
    s = (c * jax.nn.sigmoid(c)).astype(BF16)
    for w_ref, b_ref, o_ref in ((w0_ref, b0_ref, o0_ref), (w1_ref, b1_ref, o1_ref)):
        o_ref[...] = jnp.dot(s, w_ref[...].astype(BF16),
                             preferred_element_type=F32) + b_ref[...]


def _modulations(cvec, w0, b0, w1, b1):
    w_spec = pl.BlockSpec((D_MODEL, MOD_TILE), lambda j: (0, j))
    v_spec = pl.BlockSpec((1, MOD_TILE), lambda j: (0, j))
    o_spec = pl.BlockSpec((MOD_ROWS, MOD_TILE), lambda j: (0, j))
    out = jax.ShapeDtypeStruct((MOD_ROWS, N_MOD), F32)
    return pl.pallas_call(
        _mod_kernel,
        grid=(N_MOD // MOD_TILE,),
        in_specs=[_const_spec((MOD_ROWS, D_MODEL)), w_spec, v_spec, w_spec, v_spec],
        out_specs=[o_spec, o_spec],
        out_shape=[out, out],
        compiler_params=_params("parallel"),
        name="adaln_mod",
    )(cvec, w0, b0, w1, b1)


def _mod_spec(row_fn, tile):
    return pl.BlockSpec((None, 1, N_MOD), lambda i: (row_fn(i * tile), 0, 0))


def _in0_kernel(*refs, rope):
    if rope:
        (x_ref, mod_ref, w_ref, sw_ref, sbias_ref, cos_ref, sa_ref, sb_ref,
         q_ref, k_ref, v_ref, s_ref) = refs
    else:
        (x_ref, mod_ref, w_ref, sw_ref, sbias_ref,
         q_ref, k_ref, v_ref, s_ref, k_out_ref, v_out_ref) = refs
    subs = _sub_tiles(x_ref)
    ys = []
    for rows in subs:
        h = (x_ref[rows, :] * (1.0 + _mod(mod_ref, 1)) + _mod(mod_ref, 0)).astype(BF16)
        ys.append(jnp.dot(h, w_ref[...], preferred_element_type=F32))
    for si, (rows, y) in enumerate(zip(subs, ys)):
        if rope:
            cos, sa, sb = cos_ref[rows, :], sa_ref[rows, :], sb_ref[rows, :]
        for j in range(QK_WIDTH // LANES):
            lo, hi = j * LANES, (j + 1) * LANES
            qj = y[:, lo:hi]
            kj = y[:, QK_WIDTH + lo:QK_WIDTH + hi]
            if rope:
                qj = (qj * cos + pltpu.roll(qj, LANES - ROPE_PAIRS, 1) * sa
                      + pltpu.roll(qj, ROPE_PAIRS, 1) * sb)
                kj = (kj * cos + pltpu.roll(kj, LANES - ROPE_PAIRS, 1) * sa
                      + pltpu.roll(kj, ROPE_PAIRS, 1) * sb)
            q_ref[rows, lo:hi] = (qj * Q_SCALE).astype(q_ref.dtype)
            k_ref[rows, lo:hi] = kj.astype(k_ref.dtype)
            if not rope:
                kt = kj.T
                seqs = SUB_TILE // SEQ
                for b in range(seqs):
                    for i in range(2):
                        k_out_ref[si * seqs + b, j, i] = kt[i * HEAD_DIM_A:(i + 1) * HEAD_DIM_A,
                                                            b * SEQ:(b + 1) * SEQ]
        off = 2 * QK_WIDTH
        v = y[:, off:off + WIDTH_A]
        if rope:
            v_ref[:, rows] = v.T.astype(v_ref.dtype)
        else:
            v_ref[rows, :] = v.astype(v_ref.dtype)
            for j in range(N_HEADS_A):
                v_out_ref[rows, j, :] = v[:, j * V_DIM_A:(j + 1) * V_DIM_A]
        u_off = off + WIDTH_A
        g_off = u_off + WIDTH_B
        for g in range(N_GROUPS_B):
            lo, hi = g * GROUP_DIM_B, (g + 1) * GROUP_DIM_B
            gg = y[:, g_off + lo:g_off + hi]
            mu = jnp.mean(gg, axis=-1, keepdims=True)
            gc = gg - mu
            var = jnp.mean(gc * gc, axis=-1, keepdims=True)
            vc = (gc * lax.rsqrt(var + LN_EPS)).astype(BF16)
            w = sw_ref[g]
            bias = jnp.broadcast_to(sbias_ref[:, g:g + 1], (CHUNK, GROUP_DIM_B))
            for c in range(SUB_TILE // CHUNK):
                r0, r1 = c * CHUNK, (c + 1) * CHUNK
                mixed = jnp.dot(w, vc[r0:r1], preferred_element_type=F32) + bias
                s_ref[rows.start + r0:rows.start + r1, lo:hi] = (
                    y[r0:r1, u_off + lo:u_off + hi] * mixed).astype(s_ref.dtype)


def _in_proj0(x, mods, w_in, sgu_w, sgu_bt, row_fn, rope_tabs):
    n = x.shape[0]
    t = TOK_TILE
    tok = lambda width: pl.BlockSpec((t, width), lambda i: (i, 0))
    in_specs = [tok(D_MODEL), _mod_spec(row_fn, t), _const_spec((D_MODEL, IN_WIDTH_0)),
                _const_spec((N_GROUPS_B, CHUNK, CHUNK)), _const_spec((CHUNK, N_GROUPS_B))]
    args = [x, mods, w_in, sgu_w, sgu_bt]
    v_spec = tok(WIDTH_A)
    v_shape = (n, WIDTH_A)
    if rope_tabs is not None:
        blocks_per_seq = DEC_SEQ // t
        tab = pl.BlockSpec((t, LANES), lambda i: (i % blocks_per_seq, 0))
        in_specs += [tab, tab, tab]
        args += list(rope_tabs)
        v_spec = pl.BlockSpec((WIDTH_A, t), lambda i: (i // blocks_per_seq, i % blocks_per_seq))
        v_shape = (n // DEC_SEQ * WIDTH_A, DEC_SEQ)
    out_shape = [jax.ShapeDtypeStruct((n, QK_WIDTH), BF16),
                 jax.ShapeDtypeStruct((n, QK_WIDTH), BF16),
                 jax.ShapeDtypeStruct(v_shape, BF16),
                 jax.ShapeDtypeStruct((n, WIDTH_B), BF16)]
    out_specs = [tok(QK_WIDTH), tok(QK_WIDTH), v_spec, tok(WIDTH_B)]
    if rope_tabs is None:
        out_shape += [jax.ShapeDtypeStruct((n // SEQ, N_HEADS_A, 2, HEAD_DIM_A, SEQ), F32),
                      jax.ShapeDtypeStruct((n, N_HEADS_A, V_DIM_A), F32)]
        out_specs += [pl.BlockSpec((t // SEQ, N_HEADS_A, 2, HEAD_DIM_A, SEQ),
                                   lambda i: (i, 0, 0, 0, 0)),
                      pl.BlockSpec((t, N_HEADS_A, V_DIM_A), lambda i: (i, 0, 0))]
    return pl.pallas_call(
        functools.partial(_in0_kernel, rope=rope_tabs is not None),
        grid=(n // t,),
        in_specs=in_specs,
        out_specs=out_specs,
        out_shape=out_shape,
        compiler_params=_params("parallel"),
        name="in_proj0",
    )(*args)


def _lambda(lp_ref):
    lp = lp_ref[...]
    a = jnp.sum(lp[0:1] * lp[1:2], axis=1, keepdims=True)
    b = jnp.sum(lp[2:3] * lp[3:4], axis=1, keepdims=True)
    return jnp.exp(a) - jnp.exp(b) + LAMBDA_INIT_0


def _stack_halves(qh):
    lane = lax.broadcasted_iota(jnp.int32, qh.shape, 1)
    zero = jnp.zeros_like(qh)
    return jnp.concatenate([jnp.where(lane < HEAD_DIM_A, qh, zero),
                            jnp.where(lane >= HEAD_DIM_A, qh, zero)], axis=0)


def _scores(qq, k):
    return lax.dot_general(qq, k, (((1,), (1,)), ((), ())), preferred_element_type=F32)


def _head_out(pv, l, lam, gain, rows):
    o = pv[:rows] / l[:rows] - lam * (pv[rows:] / l[rows:])
    ms = jnp.mean(o * o, axis=-1, keepdims=True)
    return o * lax.rsqrt(ms + LN_EPS) * gain * (1.0 - LAMBDA_INIT_0)


def _attn_ctx_kernel(q_ref, k_ref, v_ref, lp_ref, gain_ref, o_ref):
    lam = _lambda(lp_ref)
    gain = gain_ref[...]
    units = [(b, h) for b in range(q_ref.shape[0] // SEQ) for h in range(N_HEADS_A)]
    scores = {}

    def issue_scores(unit):
        b, h = unit
        rows = slice(b * SEQ, (b + 1) * SEQ)
        lo, hi = h * LANES, (h + 1) * LANES
        scores[unit] = _scores(_stack_halves(q_ref[rows, lo:hi]), k_ref[rows, lo:hi])

    for unit in units[:CTX_LOOKAHEAD]:
        issue_scores(unit)
    for i, unit in enumerate(units):
        if i + CTX_LOOKAHEAD < len(units):
            issue_scores(units[i + CTX_LOOKAHEAD])
        b, h = unit
        rows = slice(b * SEQ, (b + 1) * SEQ)
        lo, hi = h * LANES, (h + 1) * LANES
        s = scores.pop(unit)
        p = jnp.exp2(s - jnp.max(s, axis=-1, keepdims=True))
        l = jnp.sum(p, axis=-1, keepdims=True)
        pv = jnp.dot(p.astype(BF16), v_ref[rows, lo:hi], preferred_element_type=F32)
        o_ref[rows, lo:hi] = _head_out(pv, l, lam, gain, SEQ).astype(o_ref.dtype)


def _attn_ctx(q, k, v, lam_params, gain):
    n = q.shape[0]
    blk = lambda: pl.BlockSpec((CTX_SEQS * SEQ, QK_WIDTH), lambda b: (b, 0))
    return pl.pallas_call(
        _attn_ctx_kernel,
        grid=(n // (CTX_SEQS * SEQ),),
        in_specs=[blk(), blk(), blk(), _const_spec((4, HEAD_DIM_A)), _const_spec((1, V_DIM_A))],
        out_specs=blk(),
        out_shape=jax.ShapeDtypeStruct((n, WIDTH_A), BF16),
        compiler_params=_params("parallel"),
        name="attn_ctx",
    )(q, k, v, lam_params, gain)


def _attn_lat_kernel(q_ref, k_ref, vt_ref, ck_ref, cvt_ref, lp_ref, gain_ref, o_ref):
    lam = _lambda(lp_ref)
    gain = gain_ref[...]
    rows = ATT_SUB_BLOCK
    chunks = [(None, PAST_LEN)] + [(s, ATT_KEY_CHUNK) for s in range(0, DEC_SEQ, ATT_KEY_CHUNK)]
    ones = jnp.ones((BF16_SUBLANES, ATT_KEY_CHUNK), BF16)
    units = [(sub, h, c) for sub in range(q_ref.shape[0] // rows)
             for h in range(N_HEADS_A) for c in range(len(chunks))]
    qq_of = {}
    scores = {}

    def issue_scores(unit):
        sub, h, c = unit
        lo, hi = h * LANES, (h + 1) * LANES
        if c == 0:
            qq_of[sub, h] = _stack_halves(q_ref[sub * rows:(sub + 1) * rows, lo:hi])
        start, size = chunks[c]
        keys = ck_ref[:, lo:hi].astype(BF16) if start is None else k_ref[start:start + size, lo:hi]
        scores[unit] = _scores(keys, qq_of[sub, h])

    for unit in units[:ATT_LOOKAHEAD]:
        issue_scores(unit)
    m = acc = None
    for i, unit in enumerate(units):
        if i + ATT_LOOKAHEAD < len(units):
            issue_scores(units[i + ATT_LOOKAHEAD])
        sub, h, c = unit
        lo, hi = h * LANES, (h + 1) * LANES
        start, size = chunks[c]
        vals_t = cvt_ref[lo:hi, :] if start is None else vt_ref[lo:hi, start:start + size]
        st = scores.pop(unit)
        cm = jnp.max(st, axis=0, keepdims=True)
        m_new = cm if c == 0 else jnp.maximum(m, cm)
        pt = jnp.exp2(st - m_new).astype(BF16)
        part = jnp.dot(jnp.concatenate([vals_t, ones[:, :size]], axis=0), pt,
                       preferred_element_type=F32)
        acc = part if c == 0 else acc * jnp.exp2(m - m_new) + part
        m = m_new
        if c == len(chunks) - 1:
            pvt = acc[:V_DIM_A]
            l = acc[V_DIM_A:V_DIM_A + 1]
            ot = pvt[:, :rows] / l[:, :rows] - lam * (pvt[:, rows:] / l[:, rows:])
            ms = jnp.mean(ot * ot, axis=0, keepdims=True)
            on = ot * lax.rsqrt(ms + LN_EPS) * gain * (1.0 - LAMBDA_INIT_0)
            o_ref[sub * rows:(sub + 1) * rows, lo:hi] = on.T.astype(o_ref.dtype)


def _attn_lat(q, k, vt, cache_k, cache_vt, lam_params, gain_col):
    n = q.shape[0]
    nb = n // DEC_SEQ
    qb = DEC_SEQ // ATT_Q_BLOCK
    q_spec = pl.BlockSpec((ATT_Q_BLOCK, QK_WIDTH), lambda b, j: (b * qb + j, 0))
    k_spec = pl.BlockSpec((DEC_SEQ, QK_WIDTH), lambda b, j: (b, 0))
    vt_spec = pl.BlockSpec((WIDTH_A, DEC_SEQ), lambda b, j: (b, 0))
    ck_spec = pl.BlockSpec((PAST_LEN, QK_WIDTH), lambda b, j: (b, 0))
    cvt_spec = pl.BlockSpec((WIDTH_A, PAST_LEN), lambda b, j: (b, 0))
    return pl.pallas_call(
        _attn_lat_kernel,
        grid=(nb, qb),
        in_specs=[q_spec, k_spec, vt_spec, ck_spec, cvt_spec,
                  _const_spec((4, HEAD_DIM_A)), _const_spec((V_DIM_A, 1))],
        out_specs=q_spec,
        out_shape=jax.ShapeDtypeStruct((n, WIDTH_A), BF16),
        compiler_params=_params("parallel", "arbitrary"),
        name="attn_lat",
    )(q, k, vt, cache_k, cache_vt, lam_params, gain_col)


def _conv1_kernel(xp_ref, x_ref, xn_ref, mod_ref, wi_ref, cw_ref, o_ref, z_scr, *, seq_len):
    t = x_ref.shape[0]
    halo = SUBLANES
    cc = CONV_CHUNK
    n_chunks = WIDTH_C // cc
    xin = jnp.concatenate([xp_ref[...], x_ref[...], xn_ref[...]], axis=0)
    h = (xin * (1.0 + _mod(mod_ref, 1)) + _mod(mod_ref, 0)).astype(BF16)
    pos = (pl.program_id(0) * t + lax.broadcasted_iota(jnp.int32, (t, 1), 0)) % seq_len
    not_first = pos != 0
    not_last = pos != seq_len - 1

    def project(j):
        return [jnp.dot(h, wi_ref[:, sec * WIDTH_C + j * cc:sec * WIDTH_C + (j + 1) * cc],
                        preferred_element_type=F32) for sec in range(3)]

    y_next = project(0)
    for j in range(n_chunks):
        gate, conv_gate, conv_in = y_next
        if j + 1 < n_chunks:
            y_next = project(j + 1)
        z_scr[j] = conv_gate * conv_in
        cw = cw_ref[:, j * cc:(j + 1) * cc]
        z_prev = jnp.where(not_first, z_scr[j, halo - 1:halo - 1 + t, :], 0.0)
        z_next = jnp.where(not_last, z_scr[j, halo + 1:halo + 1 + t, :], 0.0)
        conv = z_prev * cw[0:1] + z_scr[j, halo:halo + t, :] * cw[1:2] + z_next * cw[2:3]
        o_ref[:, j * cc:(j + 1) * cc] = (gate[halo:halo + t] * conv).astype(o_ref.dtype)


def _conv1(x, mods, row_fn, w_in, conv_w, seq_len):
    n = x.shape[0]
    t = TOK_TILE
    tiles_per_block = t // SUBLANES
    last = n // SUBLANES - 1
    tok = pl.BlockSpec((t, D_MODEL), lambda i: (i, 0))
    prev = pl.BlockSpec((SUBLANES, D_MODEL),
                        lambda i: (jnp.maximum(i * tiles_per_block - 1, 0), 0))
    nxt = pl.BlockSpec((SUBLANES, D_MODEL),
                       lambda i: (jnp.minimum((i + 1) * tiles_per_block, last), 0))
    return pl.pallas_call(
        functools.partial(_conv1_kernel, seq_len=seq_len),
        grid=(n // t,),
        in_specs=[prev, tok, nxt, _mod_spec(row_fn, t),
                  _const_spec((D_MODEL, 3 * WIDTH_C)), _const_spec((3, WIDTH_C))],
        out_specs=pl.BlockSpec((t, WIDTH_C), lambda i: (i, 0)),
        out_shape=jax.ShapeDtypeStruct((n, WIDTH_C), BF16),
        scratch_shapes=[pltpu.VMEM((WIDTH_C // CONV_CHUNK, t + 2 * SUBLANES, CONV_CHUNK), F32)],
        compiler_params=_params("parallel"),
        name="conv1",
    )(x, x, x, mods, w_in, conv_w)


def _post_kernel(*refs, n_pieces):
    pieces = refs[:n_pieces]
    (x_ref, mod_ref, wo_ref, g1_ref, b1_ref, w1_ref, w2_ref, g2_ref, b2_ref, o_ref) = refs[n_pieces:]
    subs = _sub_tiles(x_ref)
    mixed = []
    for rows in subs:
        out = None
        k0 = 0
        for p_ref in pieces:
            k1 = k0 + p_ref.shape[1]
            part = jnp.dot(p_ref[rows, :], wo_ref[k0:k1, :], preferred_element_type=F32)
            out = part if out is None else out + part
            k0 = k1
        mixed.append(_layer_norm(ALPHA * x_ref[rows, :] + _mod(mod_ref, 2) * out,
                                 g1_ref[...], b1_ref[...]))
    hs = [(x * (1.0 + _mod(mod_ref, 4)) + _mod(mod_ref, 3)).astype(BF16) for x in mixed]
    fs = [None] * len(subs)
    for j in range(D_FF // FF_CHUNK):
        lo, hi = j * FF_CHUNK, (j + 1) * FF_CHUNK
        hids = [jnp.dot(h, w1_ref[:, lo:hi], preferred_element_type=F32) for h in hs]
        for s, hid in enumerate(hids):
            hid = jnp.square(jnp.maximum(hid, 0.0)).astype(BF16)
            part = jnp.dot(hid, w2_ref[lo:hi, :], preferred_element_type=F32)
            fs[s] = part if fs[s] is None else fs[s] + part
    for rows, x, f in zip(subs, mixed, fs):
        o_ref[rows, :] = _layer_norm(ALPHA * x + _mod(mod_ref, 5) * f, g2_ref[...], b2_ref[...])


def _post(pieces, x, mods, row_fn, w_out, g1, b1, w1, w2, g2, b2):
    n = x.shape[0]
    t = TOK_TILE
    tok = lambda width: pl.BlockSpec((t, width), lambda i: (i, 0))
    vec = _const_spec((1, D_MODEL))
    return pl.pallas_call(
        functools.partial(_post_kernel, n_pieces=len(pieces)),
        grid=(n // t,),
        in_specs=[tok(p.shape[1]) for p in pieces] + [
            tok(D_MODEL), _mod_spec(row_fn, t), _const_spec(w_out.shape), vec, vec,
            _const_spec((D_MODEL, D_FF)), _const_spec((D_FF, D_MODEL)), vec, vec],
        out_specs=tok(D_MODEL),
        out_shape=jax.ShapeDtypeStruct((n, D_MODEL), F32),
        compiler_params=_params("parallel"),
        name="post",
    )(*pieces, x, mods, w_out, g1, b1, w1, w2, g2, b2)


def _rope_tables(n):
    rows = n // GRID_W
    row = jnp.repeat(jnp.arange(rows, dtype=F32), GRID_W)
    col = jnp.tile(jnp.arange(GRID_W, dtype=F32), rows)
    inv = 1.0 / (ROPE_BASE ** (jnp.arange(ROPE_PAIRS, dtype=F32) / ROPE_PAIRS))
    ang_r, ang_c = row[:, None] * inv, col[:, None] * inv
    zero = jnp.zeros_like(ang_r)
    reps = LANES // HEAD_DIM_A
    cos = jnp.tile(jnp.concatenate([jnp.cos(ang_r)] * 2 + [jnp.cos(ang_c)] * 2, axis=1), (1, reps))
    sa = jnp.tile(jnp.concatenate([-jnp.sin(ang_r), zero, -jnp.sin(ang_c), zero], axis=1), (1, reps))
    sb = jnp.tile(jnp.concatenate([zero, jnp.sin(ang_r), zero, jnp.sin(ang_c)], axis=1), (1, reps))
    return cos, sa, sb


def kernel(x_prompt, x_sample, cache_k0, cache_v0, c, c_ctx, w_mod0, b_mod0, w_in0, lambda_q1_0, lambda_k1_0, lambda_q2_0, lambda_k2_0, subln_g0, sgu_w0, sgu_b0, w_out0, ln_mix_g0, ln_mix_b0, w_ff1_0, w_ff2_0, ln_ff_g0, ln_ff_b0, w_mod1, b_mod1, w_in1, conv_w1, w_out1, ln_mix_g1, ln_mix_b1, w_ff1_1, w_ff2_1, ln_ff_g1, ln_ff_b1):
    batch, seq, d = x_prompt.shape
    dec_batch, dec_seq, _ = x_sample.shape
    assert (seq, dec_seq, d) == (SEQ, DEC_SEQ, D_MODEL) and 1 + dec_batch <= MOD_ROWS
    row = lambda v: v.reshape(1, -1)

    cvec = jnp.concatenate([c_ctx[None, :], c, jnp.zeros((MOD_ROWS - 1 - dec_batch, d), F32)], axis=0)
    mods0, mods1 = _modulations(cvec, w_mod0, row(b_mod0), w_mod1, row(b_mod1))
    mods0 = mods0.reshape(MOD_ROWS, 1, N_MOD)
    mods1 = mods1.reshape(MOD_ROWS, 1, N_MOD)
    ctx_row = lambda tok: 0
    lat_row = lambda tok: 1 + tok // DEC_SEQ

    xp = x_prompt.reshape(batch * seq, d)
    xs = x_sample.reshape(dec_batch * dec_seq, d)

    w_in0_b = w_in0.astype(BF16)
    sgu = (sgu_w0.astype(BF16), sgu_b0.T)
    lam_params = jnp.stack([lambda_q1_0, lambda_k1_0, lambda_q2_0, lambda_k2_0])
    gain = row(subln_g0)
    qp, kp, vp, sp, new_k0, new_v0 = _in_proj0(xp, mods0, w_in0_b, *sgu, ctx_row, None)
    qs, ks, vs, ss = _in_proj0(xs, mods0, w_in0_b, *sgu, lat_row, _rope_tables(dec_seq))
    ap = _attn_ctx(qp, kp, vp, lam_params, gain)
    cache_vt = jnp.swapaxes(cache_v0.reshape(dec_batch, PAST_LEN, WIDTH_A), 1, 2)
    a_s = _attn_lat(qs, ks, vs, cache_k0.reshape(dec_batch * PAST_LEN, QK_WIDTH),
                    cache_vt.reshape(dec_batch * WIDTH_A, PAST_LEN).astype(BF16),
                    lam_params, gain.reshape(V_DIM_A, 1))
    post0_w = (w_out0.astype(BF16), row(ln_mix_g0), row(ln_mix_b0),
               w_ff1_0.astype(BF16), w_ff2_0.astype(BF16), row(ln_ff_g0), row(ln_ff_b0))
    xp = _post([ap, sp], xp, mods0, ctx_row, *post0_w)
    xs = _post([a_s, ss], xs, mods0, lat_row, *post0_w)

    w_in1_b = w_in1.astype(BF16)
    gp = _conv1(xp, mods1, ctx_row, w_in1_b, conv_w1, seq_len=seq)
    gs = _conv1(xs, mods1, lat_row, w_in1_b, conv_w1, seq_len=dec_seq)
    post1_w = (w_out1.astype(BF16), row(ln_mix_g1), row(ln_mix_b1),
               w_ff1_1.astype(BF16), w_ff2_1.astype(BF16), row(ln_ff_g1), row(ln_ff_b1))
    xp = _post([gp], xp, mods1, ctx_row, *post1_w)
    xs = _post([gs], xs, mods1, lat_row, *post1_w)

    return (xp.reshape(batch, seq, d), xs.reshape(dec_batch, dec_seq, d),
            jnp.transpose(new_k0, (0, 4, 1, 2, 3)),
            new_v0.reshape(batch, seq, N_HEADS_A, V_DIM_A))
```

```python
import functools
import math

import jax
import jax.numpy as jnp
import numpy as np
from jax import lax
from jax.experimental import pallas as pl
from jax.experimental.pallas import tpu as pltpu

F32 = jnp.float32
BF16 = jnp.bfloat16

D_MODEL = 1024
DEPTH = 2
SEQ = 256
DEC_SEQ = 4096
PAST_LEN = 256
GRID_W = 64
N_HEADS_A = 4
HEAD_DIM_A = 64
V_DIM_A = 2 * HEAD_DIM_A
QK_WIDTH = N_HEADS_A * 2 * HEAD_DIM_A
WIDTH_A = N_HEADS_A * V_DIM_A
N_GROUPS_B = 4
CHUNK = 128
GROUP_DIM_B = 128
WIDTH_B = N_GROUPS_B * GROUP_DIM_B
IN_WIDTH_0 = 2 * QK_WIDTH + WIDTH_A + 2 * WIDTH_B
WIDTH_C = D_MODEL
D_FF = 4 * D_MODEL
ROPE_BASE = 10000.0
ROPE_PAIRS = HEAD_DIM_A // 4
LN_EPS = 1e-5
ALPHA = (2 * DEPTH) ** 0.25
LAMBDA_INIT_0 = 0.8 - 0.6 * math.exp(-0.3 * 0)
Q_SCALE = HEAD_DIM_A ** -0.5 * math.log2(math.e)

LANES = 128
SUBLANES = 8
BF16_SUBLANES = 16
VMEM_LIMIT = 56 * 2 ** 20
N_MOD = 6 * D_MODEL
MOD_ROWS = 8
TOK_TILE = 1024
SUB_TILE = 512
CTX_SEQS = 4
CTX_LOOKAHEAD = 3
ATT_Q_BLOCK = 512
ATT_SUB_BLOCK = 128
ATT_KEY_CHUNK = 256
ATT_LOOKAHEAD = 5
FF_CHUNK = 1024
CONV_CHUNK = 256
MOD_TILE = 512


def _const_spec(shape):
    zeros = (0,) * len(shape)
    return pl.BlockSpec(shape, lambda *_: zeros, pipeline_mode=pl.Buffered(1))


def _params(*sem):
    return pltpu.CompilerParams(dimension_semantics=sem, vmem_limit_bytes=VMEM_LIMIT)


def _layer_norm(x, g, b):
    mu = jnp.mean(x, axis=-1, keepdims=True)
    xc = x - mu
    var = jnp.mean(xc * xc, axis=-1, keepdims=True)
    return xc * lax.rsqrt(var + LN_EPS) * g + b


def _mod(mod_ref, idx):
    return mod_ref[:, idx * D_MODEL:(idx + 1) * D_MODEL]


def _sub_tiles(ref):
    return [slice(i * SUB_TILE, (i + 1) * SUB_TILE) for i in range(ref.shape[0] // SUB_TILE)]


def _mod_kernel(c_ref, w0_ref, b0_ref, w1_ref, b1_ref, o0_ref, o1_ref):
    c = c_ref[...]
    s = (c * jax.nn.sigmoid(c)).astype(BF16)
    for w_ref, b_ref, o_ref in ((w0_ref, b0_ref, o0_ref), (w1_ref, b1_ref, o1_ref)):
        o_ref[...] = jnp.dot(s, w_ref[...].astype(BF16),
                             preferred_element_type=F32) + b_ref[...]


def _modulations(cvec, w0, b0, w1, b1):
    w_spec = pl.BlockSpec((D_MODEL, MOD_TILE), lambda j: (0, j))
    v_spec = pl.BlockSpec((1, MOD_TILE), lambda j: (0, j))
    o_spec = pl.BlockSpec((MOD_ROWS, MOD_TILE), lambda j: (0, j))
    out = jax.ShapeDtypeStruct((MOD_ROWS, N_MOD), F32)
    return pl.pallas_call(
        _mod_kernel,
        grid=(N_MOD // MOD_TILE,),
        in_specs=[_const_spec((MOD_ROWS, D_MODEL)), w_spec, v_spec, w_spec, v_spec],
        out_specs=[o_spec, o_spec],
        out_shape=[out, out],
        compiler_params=_params("parallel"),
        name="adaln_mod",
    )(cvec, w0, b0, w1, b1)


def _mod_spec(row_fn, tile):
    return pl.BlockSpec((None, 1, N_MOD), lambda i: (row_fn(i * tile), 0, 0))


def _in0_kernel(*refs, rope):
    if rope:
        (x_ref, mod_ref, w_ref, sw_ref, sbias_ref, cos_ref, sa_ref, sb_ref,
         q_ref, k_ref, v_ref, s_ref) = refs
    else:
        (x_ref, mod_ref, w_ref, sw_ref, sbias_ref,
         q_ref, k_ref, v_ref, s_ref, k_out_ref, v_out_ref) = refs
    subs = _sub_tiles(x_ref)
    ys = []
    for rows in subs:
        h = (x_ref[rows, :] * (1.0 + _mod(mod_ref, 1)) + _mod(mod_ref, 0)).astype(BF16)
        ys.append(jnp.dot(h, w_ref[...], preferred_element_type=F32))
    for si, (rows, y) in enumerate(zip(subs, ys)):
        if rope:
            cos, sa, sb = cos_ref[rows, :], sa_ref[rows, :], sb_ref[rows, :]
        for j in range(QK_WIDTH // LANES):
            lo, hi = j * LANES, (j + 1) * LANES
            qj = y[:, lo:hi]
            kj = y[:, QK_WIDTH + lo:QK_WIDTH + hi]
            if rope:
                qj = (qj * cos + pltpu.roll(qj, LANES - ROPE_PAIRS, 1) * sa
                      + pltpu.roll(qj, ROPE_PAIRS, 1) * sb)
                kj = (kj * cos + pltpu.roll(kj, LANES - ROPE_PAIRS, 1) * sa
                      + pltpu.roll(kj, ROPE_PAIRS, 1) * sb)
            q_ref[rows, lo:hi] = (qj * Q_SCALE).astype(q_ref.dtype)
            k_ref[rows, lo:hi] = kj.astype(k_ref.dtype)
            if not rope:
                kt = kj.T
                seqs = SUB_TILE // SEQ
                for b in range(seqs):
                    for i in range(2):
                        k_out_ref[si * seqs + b, j, i] = kt[i * HEAD_DIM_A:(i + 1) * HEAD_DIM_A,
                                                            b * SEQ:(b + 1) * SEQ]
        off = 2 * QK_WIDTH
        v = y[:, off:off + WIDTH_A]
        if rope:
            v_ref[:, rows] = v.T.astype(v_ref.dtype)
        else:
            v_ref[rows, :] = v.astype(v_ref.dtype)
            for j in range(N_HEADS_A):
                v_out_ref[rows, j, :] = v[:, j * V_DIM_A:(j + 1) * V_DIM_A]
        u_off = off + WIDTH_A
        g_off = u_off + WIDTH_B
        for g in range(N_GROUPS_B):
            lo, hi = g * GROUP_DIM_B, (g + 1) * GROUP_DIM_B
            gg = y[:, g_off + lo:g_off + hi]
            mu = jnp.mean(gg, axis=-1, keepdims=True)
            gc = gg - mu
            var = jnp.mean(gc * gc, axis=-1, keepdims=True)
            vc = (gc * lax.rsqrt(var + LN_EPS)).astype(BF16)
            w = sw_ref[g]
            bias = jnp.broadcast_to(sbias_ref[:, g:g + 1], (CHUNK, GROUP_DIM_B))
            for c in range(SUB_TILE // CHUNK):
                r0, r1 = c * CHUNK, (c + 1) * CHUNK
                mixed = jnp.dot(w, vc[r0:r1], preferred_element_type=F32) + bias
                s_ref[rows.start + r0:rows.start + r1, lo:hi] = (
                    y[r0:r1, u_off + lo:u_off + hi] * mixed).astype(s_ref.dtype)


def _in_proj0(x, mods, w_in, sgu_w, sgu_bt, row_fn, rope_tabs):
    n = x.shape[0]
    t = TOK_TILE
    tok = lambda width: pl.BlockSpec((t, width), lambda i: (i, 0))
    in_specs = [tok(D_MODEL), _mod_spec(row_fn, t), _const_spec((D_MODEL, IN_WIDTH_0)),
                _const_spec((N_GROUPS_B, CHUNK, CHUNK)), _const_spec((CHUNK, N_GROUPS_B))]
    args = [x, mods, w_in, sgu_w, sgu_bt]
    v_spec = tok(WIDTH_A)
    v_shape = (n, WIDTH_A)
    if rope_tabs is not None:
        blocks_per_seq = DEC_SEQ // t
        tab = pl.BlockSpec((t, LANES), lambda i: (i % blocks_per_seq, 0))
        in_specs += [tab, tab, tab]
        args += list(rope_tabs)
        v_spec = pl.BlockSpec((WIDTH_A, t), lambda i: (i // blocks_per_seq, i % blocks_per_seq))
        v_shape = (n // DEC_SEQ * WIDTH_A, DEC_SEQ)
    out_shape = [jax.ShapeDtypeStruct((n, QK_WIDTH), BF16),
                 jax.ShapeDtypeStruct((n, QK_WIDTH), BF16),
                 jax.ShapeDtypeStruct(v_shape, BF16),
                 jax.ShapeDtypeStruct((n, WIDTH_B), BF16)]
    out_specs = [tok(QK_WIDTH), tok(QK_WIDTH), v_spec, tok(WIDTH_B)]
    if rope_tabs is None:
        out_shape += [jax.ShapeDtypeStruct((n // SEQ, N_HEADS_A, 2, HEAD_DIM_A, SEQ), F32),
                      jax.ShapeDtypeStruct((n, N_HEADS_A, V_DIM_A), F32)]
        out_specs += [pl.BlockSpec((t // SEQ, N_HEADS_A, 2, HEAD_DIM_A, SEQ),
                                   lambda i: (i, 0, 0, 0, 0)),
                      pl.BlockSpec((t, N_HEADS_A, V_DIM_A), lambda i: (i, 0, 0))]
    return pl.pallas_call(
        functools.partial(_in0_kernel, rope=rope_tabs is not None),
        grid=(n // t,),
        in_specs=in_specs,
        out_specs=out_specs,
        out_shape=out_shape,
        compiler_params=_params("parallel"),
        name="in_proj0",
    )(*args)


def _lambda(lp_ref):
    lp = lp_ref[...]
    a = jnp.sum(lp[0:1] * lp[1:2], axis=1, keepdims=True)
    b = jnp.sum(lp[2:3] * lp[3:4], axis=1, keepdims=True)
    return jnp.exp(a) - jnp.exp(b) + LAMBDA_INIT_0


def _stack_halves(qh):
    lane = lax.broadcasted_iota(jnp.int32, qh.shape, 1)
    zero = jnp.zeros_like(qh)
    return jnp.concatenate([jnp.where(lane < HEAD_DIM_A, qh, zero),
                            jnp.where(lane >= HEAD_DIM_A, qh, zero)], axis=0)


def _scores(qq, k):
    return lax.dot_general(qq, k, (((1,), (1,)), ((), ())), preferred_element_type=F32)


def _head_out(pv, l, lam, gain, rows):
    o = pv[:rows] / l[:rows] - lam * (pv[rows:] / l[rows:])
    ms = jnp.mean(o * o, axis=-1, keepdims=True)
    return o * lax.rsqrt(ms + LN_EPS) * gain * (1.0 - LAMBDA_INIT_0)


def _attn_ctx_kernel(q_ref, k_ref, v_ref, lp_ref, gain_ref, o_ref):
    lam = _lambda(lp_ref)
    gain = gain_ref[...]
    units = [(b, h) for b in range(q_ref.shape[0] // SEQ) for h in range(N_HEADS_A)]
    scores = {}

    def issue_scores(unit):
        b, h = unit
        rows = slice(b * SEQ, (b + 1) * SEQ)
        lo, hi = h * LANES, (h + 1) * LANES
        scores[unit] = _scores(_stack_halves(q_ref[rows, lo:hi]), k_ref[rows, lo:hi])

    for unit in units[:CTX_LOOKAHEAD]:
        issue_scores(unit)
    for i, unit in enumerate(units):
        if i + CTX_LOOKAHEAD < len(units):
            issue_scores(units[i + CTX_LOOKAHEAD])
        b, h = unit
        rows = slice(b * SEQ, (b + 1) * SEQ)
        lo, hi = h * LANES, (h + 1) * LANES
        s = scores.pop(unit)
        p = jnp.exp2(s - jnp.max(s, axis=-1, keepdims=True))
        l = jnp.sum(p, axis=-1, keepdims=True)
        pv = jnp.dot(p.astype(BF16), v_ref[rows, lo:hi], preferred_element_type=F32)
        o_ref[rows, lo:hi] = _head_out(pv, l, lam, gain, SEQ).astype(o_ref.dtype)


def _attn_ctx(q, k, v, lam_params, gain):
    n = q.shape[0]
    blk = lambda: pl.BlockSpec((CTX_SEQS * SEQ, QK_WIDTH), lambda b: (b, 0))
    return pl.pallas_call(
        _attn_ctx_kernel,
        grid=(n // (CTX_SEQS * SEQ),),
        in_specs=[blk(), blk(), blk(), _const_spec((4, HEAD_DIM_A)), _const_spec((1, V_DIM_A))],
        out_specs=blk(),
        out_shape=jax.ShapeDtypeStruct((n, WIDTH_A), BF16),
        compiler_params=_params("parallel"),
        name="attn_ctx",
    )(q, k, v, lam_params, gain)


def _attn_lat_kernel(q_ref, k_ref, vt_ref, ck_ref, cvt_ref, lp_ref, gain_ref, *rest, n_weights):
    w_refs, o_ref, wb_refs = rest[:n_weights], rest[n_weights], rest[n_weights + 1:]
    for w_ref, wb_ref in zip(w_refs, wb_refs):
        wb_ref[...] = w_ref[...].astype(wb_ref.dtype)
    lam = _lambda(lp_ref)
    gain = gain_ref[...]
    rows = ATT_SUB_BLOCK
    chunks = [(None, PAST_LEN)] + [(s, ATT_KEY_CHUNK) for s in range(0, DEC_SEQ, ATT_KEY_CHUNK)]
    ones = jnp.ones((BF16_SUBLANES, ATT_KEY_CHUNK), BF16)
    units = [(sub, h, c) for sub in range(q_ref.shape[0] // rows)
             for h in range(N_HEADS_A) for c in range(len(chunks))]
    qq_of = {}
    scores = {}

    def issue_scores(unit):
        sub, h, c = unit
        lo, hi = h * LANES, (h + 1) * LANES
        if c == 0:
            qq_of[sub, h] = _stack_halves(q_ref[sub * rows:(sub + 1) * rows, lo:hi])
        start, size = chunks[c]
        keys = ck_ref[:, lo:hi].astype(BF16) if start is None else k_ref[start:start + size, lo:hi]
        scores[unit] = _scores(keys, qq_of[sub, h])

    for unit in units[:ATT_LOOKAHEAD]:
        issue_scores(unit)
    m = acc = None
    for i, unit in enumerate(units):
        if i + ATT_LOOKAHEAD < len(units):
            issue_scores(units[i + ATT_LOOKAHEAD])
        sub, h, c = unit
        lo, hi = h * LANES, (h + 1) * LANES
        start, size = chunks[c]
        vals_t = cvt_ref[lo:hi, :] if start is None else vt_ref[lo:hi, start:start + size]
        st = scores.pop(unit)
        cm = jnp.max(st, axis=0, keepdims=True)
        m_new = cm if c == 0 else jnp.maximum(m, cm)
        pt = jnp.exp2(st - m_new).astype(BF16)
        part = jnp.dot(jnp.concatenate([vals_t, ones[:, :size]], axis=0), pt,
                       preferred_element_type=F32)
        acc = part if c == 0 else acc * jnp.exp2(m - m_new) + part
        m = m_new
        if c == len(chunks) - 1:
            pvt = acc[:V_DIM_A]
            l = acc[V_DIM_A:V_DIM_A + 1]
            ot = pvt[:, :rows] / l[:, :rows] - lam * (pvt[:, rows:] / l[:, rows:])
            ms = jnp.mean(ot * ot, axis=0, keepdims=True)
            on = ot * lax.rsqrt(ms + LN_EPS) * gain * (1.0 - LAMBDA_INIT_0)
            o_ref[sub * rows:(sub + 1) * rows, lo:hi] = on.T.astype(o_ref.dtype)


def _attn_lat(q, k, vt, cache_k, cache_vt, lam_params, gain_col, weights):
    n = q.shape[0]
    nb = n // DEC_SEQ
    qb = DEC_SEQ // ATT_Q_BLOCK
    steps = nb * qb
    q_spec = pl.BlockSpec((ATT_Q_BLOCK, QK_WIDTH), lambda b, j: (b * qb + j, 0))
    k_spec = pl.BlockSpec((DEC_SEQ, QK_WIDTH), lambda b, j: (b, 0))
    vt_spec = pl.BlockSpec((WIDTH_A, DEC_SEQ), lambda b, j: (b, 0))
    ck_spec = pl.BlockSpec((PAST_LEN, QK_WIDTH), lambda b, j: (b, 0))
    cvt_spec = pl.BlockSpec((WIDTH_A, PAST_LEN), lambda b, j: (b, 0))
    w_specs = [pl.BlockSpec((w.shape[0] // steps, w.shape[1]), lambda b, j: (b * qb + j, 0))
               for w in weights]
    assert all(w.shape[0] % (steps * BF16_SUBLANES) == 0 for w in weights)
    outs = pl.pallas_call(
        functools.partial(_attn_lat_kernel, n_weights=len(weights)),
        grid=(nb, qb),
        in_specs=[q_spec, k_spec, vt_spec, ck_spec, cvt_spec,
                  _const_spec((4, HEAD_DIM_A)), _const_spec((V_DIM_A, 1))] + w_specs,
        out_specs=[q_spec] + w_specs,
        out_shape=[jax.ShapeDtypeStruct((n, WIDTH_A), BF16)]
        + [jax.ShapeDtypeStruct(w.shape, BF16) for w in weights],
        compiler_params=_params("parallel", "arbitrary"),
        name="attn_lat",
    )(q, k, vt, cache_k, cache_vt, lam_params, gain_col, *weights)
    return outs[0], outs[1:]


def _conv1_kernel(xp_ref, x_ref, xn_ref, mod_ref, wi_ref, cw_ref, o_ref, z_scr, *, seq_len):
    t = x_ref.shape[0]
    halo = SUBLANES
    cc = CONV_CHUNK
    n_chunks = WIDTH_C // cc
    xin = jnp.concatenate([xp_ref[...], x_ref[...], xn_ref[...]], axis=0)
    h = (xin * (1.0 + _mod(mod_ref, 1)) + _mod(mod_ref, 0)).astype(BF16)
    pos = (pl.program_id(0) * t + lax.broadcasted_iota(jnp.int32, (t, 1), 0)) % seq_len
    not_first = pos != 0
    not_last = pos != seq_len - 1

    def project(j):
        return [jnp.dot(h, wi_ref[:, sec * WIDTH_C + j * cc:sec * WIDTH_C + (j + 1) * cc],
                        preferred_element_type=F32) for sec in range(3)]

    y_next = project(0)
    for j in range(n_chunks):
        gate, conv_gate, conv_in = y_next
        if j + 1 < n_chunks:
            y_next = project(j + 1)
        z_scr[j] = conv_gate * conv_in
        cw = cw_ref[:, j * cc:(j + 1) * cc]
        z_prev = jnp.where(not_first, z_scr[j, halo - 1:halo - 1 + t, :], 0.0)
        z_next = jnp.where(not_last, z_scr[j, halo + 1:halo + 1 + t, :], 0.0)
        conv = z_prev * cw[0:1] + z_scr[j, halo:halo + t, :] * cw[1:2] + z_next * cw[2:3]
        o_ref[:, j * cc:(j + 1) * cc] = (gate[halo:halo + t] * conv).astype(o_ref.dtype)


def _conv1(x, mods, row_fn, w_in, conv_w, seq_len):
    n = x.shape[0]
    t = TOK_TILE
    tiles_per_block = t // SUBLANES
    last = n // SUBLANES - 1
    tok = pl.BlockSpec((t, D_MODEL), lambda i: (i, 0))
    prev = pl.BlockSpec((SUBLANES, D_MODEL),
                        lambda i: (jnp.maximum(i * tiles_per_block - 1, 0), 0))
    nxt = pl.BlockSpec((SUBLANES, D_MODEL),
                       lambda i: (jnp.minimum((i + 1) * tiles_per_block, last), 0))
    return pl.pallas_call(
        functools.partial(_conv1_kernel, seq_len=seq_len),
        grid=(n // t,),
        in_specs=[prev, tok, nxt, _mod_spec(row_fn, t),
                  _const_spec((D_MODEL, 3 * WIDTH_C)), _const_spec((3, WIDTH_C))],
        out_specs=pl.BlockSpec((t, WIDTH_C), lambda i: (i, 0)),
        out_shape=jax.ShapeDtypeStruct((n, WIDTH_C), BF16),
        scratch_shapes=[pltpu.VMEM((WIDTH_C // CONV_CHUNK, t + 2 * SUBLANES, CONV_CHUNK), F32)],
        compiler_params=_params("parallel"),
        name="conv1",
    )(x, x, x, mods, w_in, conv_w)


def _post_kernel(*refs, n_pieces):
    pieces = refs[:n_pieces]
    (x_ref, mod_ref, wo_ref, g1_ref, b1_ref, w1_ref, w2_ref, g2_ref, b2_ref, o_ref) = refs[n_pieces:]
    subs = _sub_tiles(x_ref)
    mixed = []
    for rows in subs:
        out = None
        k0 = 0
        for p_ref in pieces:
            k1 = k0 + p_ref.shape[1]
            part = jnp.dot(p_ref[rows, :], wo_ref[k0:k1, :], preferred_element_type=F32)
            out = part if out is None else out + part
            k0 = k1
        mixed.append(_layer_norm(ALPHA * x_ref[rows, :] + _mod(mod_ref, 2) * out,
                                 g1_ref[...], b1_ref[...]))
    hs = [(x * (1.0 + _mod(mod_ref, 4)) + _mod(mod_ref, 3)).astype(BF16) for x in mixed]
    fs = [None] * len(subs)
    for j in range(D_FF // FF_CHUNK):
        lo, hi = j * FF_CHUNK, (j + 1) * FF_CHUNK
        hids = [jnp.dot(h, w1_ref[:, lo:hi], preferred_element_type=F32) for h in hs]
        for s, hid in enumerate(hids):
            hid = jnp.square(jnp.maximum(hid, 0.0)).astype(BF16)
            part = jnp.dot(hid, w2_ref[lo:hi, :], preferred_element_type=F32)
            fs[s] = part if fs[s] is None else fs[s] + part
    for rows, x, f in zip(subs, mixed, fs):
        o_ref[rows, :] = _layer_norm(ALPHA * x + _mod(mod_ref, 5) * f, g2_ref[...], b2_ref[...])


def _post(pieces, x, mods, row_fn, w_out, g1, b1, w1, w2, g2, b2):
    n = x.shape[0]
    t = TOK_TILE
    tok = lambda width: pl.BlockSpec((t, width), lambda i: (i, 0))
    vec = _const_spec((1, D_MODEL))
    return pl.pallas_call(
        functools.partial(_post_kernel, n_pieces=len(pieces)),
        grid=(n // t,),
        in_specs=[tok(p.shape[1]) for p in pieces] + [
            tok(D_MODEL), _mod_spec(row_fn, t), _const_spec(w_out.shape), vec, vec,
            _const_spec((D_MODEL, D_FF)), _const_spec((D_FF, D_MODEL)), vec, vec],
        out_specs=tok(D_MODEL),
        out_shape=jax.ShapeDtypeStruct((n, D_MODEL), F32),
        compiler_params=_params("parallel"),
        name="post",
    )(*pieces, x, mods, w_out, g1, b1, w1, w2, g2, b2)


def _rope_tables(n):
    f32 = np.float32
    rows = n // GRID_W
    row = np.repeat(np.arange(rows, dtype=f32), GRID_W)
    col = np.tile(np.arange(GRID_W, dtype=f32), rows)
    inv = (f32(1.0) / (f32(ROPE_BASE) ** (np.arange(ROPE_PAIRS, dtype=f32) / f32(ROPE_PAIRS)))).astype(f32)
    ang_r, ang_c = (row[:, None] * inv).astype(f32), (col[:, None] * inv).astype(f32)
    cos_r, sin_r, cos_c, sin_c = (fn(a.astype(np.float64)).astype(f32)
                                  for a in (ang_r, ang_c) for fn in (np.cos, np.sin))
    zero = np.zeros_like(ang_r)
    reps = LANES // HEAD_DIM_A
    cos = np.tile(np.concatenate([cos_r, cos_r, cos_c, cos_c], axis=1), (1, reps))
    sa = np.tile(np.concatenate([-sin_r, zero, -sin_c, zero], axis=1), (1, reps))
    sb = np.tile(np.concatenate([zero, sin_r, zero, sin_c], axis=1), (1, reps))
    return jnp.asarray(cos), jnp.asarray(sa), jnp.asarray(sb)


def kernel(x_prompt, x_sample, cache_k0, cache_v0, c, c_ctx, w_mod0, b_mod0, w_in0, lambda_q1_0, lambda_k1_0, lambda_q2_0, lambda_k2_0, subln_g0, sgu_w0, sgu_b0, w_out0, ln_mix_g0, ln_mix_b0, w_ff1_0, w_ff2_0, ln_ff_g0, ln_ff_b0, w_mod1, b_mod1, w_in1, conv_w1, w_out1, ln_mix_g1, ln_mix_b1, w_ff1_1, w_ff2_1, ln_ff_g1, ln_ff_b1):
    batch, seq, d = x_prompt.shape
    dec_batch, dec_seq, _ = x_sample.shape
    assert (seq, dec_seq, d) == (SEQ, DEC_SEQ, D_MODEL) and 1 + dec_batch <= MOD_ROWS
    row = lambda v: v.reshape(1, -1)

    cvec = jnp.concatenate([c_ctx[None, :], c, jnp.zeros((MOD_ROWS - 1 - dec_batch, d), F32)], axis=0)
    mods0, mods1 = _modulations(cvec, w_mod0, row(b_mod0), w_mod1, row(b_mod1))
    mods0 = mods0.reshape(MOD_ROWS, 1, N_MOD)
    mods1 = mods1.reshape(MOD_ROWS, 1, N_MOD)
    ctx_row = lambda tok: 0
    lat_row = lambda tok: 1 + tok // DEC_SEQ

    xp = x_prompt.reshape(batch * seq, d)
    xs = x_sample.reshape(dec_batch * dec_seq, d)

    w_in0_b = w_in0.astype(BF16)
    sgu = (sgu_w0.astype(BF16), sgu_b0.T)
    lam_params = jnp.stack([lambda_q1_0, lambda_k1_0, lambda_q2_0, lambda_k2_0])
    gain = row(subln_g0)
    qp, kp, vp, sp, new_k0, new_v0 = _in_proj0(xp, mods0, w_in0_b, *sgu, ctx_row, None)
    qs, ks, vs, ss = _in_proj0(xs, mods0, w_in0_b, *sgu, lat_row, _rope_tables(dec_seq))
    ap = _attn_ctx(qp, kp, vp, lam_params, gain)
    cache_vt = jnp.swapaxes(cache_v0.reshape(dec_batch, PAST_LEN, WIDTH_A), 1, 2)
    later = (w_ff1_0, w_ff2_0, w_in1, w_out1, w_ff1_1, w_ff2_1)
    a_s, later_b = _attn_lat(qs, ks, vs, cache_k0.reshape(dec_batch * PAST_LEN, QK_WIDTH),
                             cache_vt.reshape(dec_batch * WIDTH_A, PAST_LEN).astype(BF16),
                             lam_params, gain.reshape(V_DIM_A, 1), later)
    w_ff1_0_b, w_ff2_0_b, w_in1_b, w_out1_b, w_ff1_1_b, w_ff2_1_b = later_b
    post0_w = (w_out0.astype(BF16), row(ln_mix_g0), row(ln_mix_b0),
               w_ff1_0_b, w_ff2_0_b, row(ln_ff_g0), row(ln_ff_b0))
    xp = _post([ap, sp], xp, mods0, ctx_row, *post0_w)
    xs = _post([a_s, ss], xs, mods0, lat_row, *post0_w)

    gp = _conv1(xp, mods1, ctx_row, w_in1_b, conv_w1, seq_len=seq)
    gs = _conv1(xs, mods1, lat_row, w_in1_b, conv_w1, seq_len=dec_seq)
    post1_w = (w_out1_b, row(ln_mix_g1), row(ln_mix_b1),
               w_ff1_1_b, w_ff2_1_b, row(ln_ff_g1), row(ln_ff_b1))
    xp = _post([gp], xp, mods1, ctx_row, *post1_w)
    xs = _post([gs], xs, mods1, lat_row, *post1_w)

    return (xp.reshape(batch, seq, d), xs.reshape(dec_batch, dec_seq, d),
            jnp.transpose(new_k0, (0, 4, 1, 2, 3)),
            new_v0.reshape(batch, seq, N_HEADS_A, V_DIM_A))
```

```python
import functools
import math

import jax
import jax.numpy as jnp
import numpy as np
from jax import lax
from jax.experimental import pallas as pl
from jax.experimental.pallas import tpu as pltpu

F32 = jnp.float32
BF16 = jnp.bfloat16

D_MODEL = 1024
DEPTH = 2
SEQ = 256
DEC_SEQ = 4096
PAST_LEN = 256
GRID_W = 64
N_HEADS_A = 4
HEAD_DIM_A = 64
V_DIM_A = 2 * HEAD_DIM_A
QK_WIDTH = N_HEADS_A * 2 * HEAD_DIM_A
WIDTH_A = N_HEADS_A * V_DIM_A
N_GROUPS_B = 4
CHUNK = 128
GROUP_DIM_B = 128
WIDTH_B = N_GROUPS_B * GROUP_DIM_B
IN_WIDTH_0 = 2 * QK_WIDTH + WIDTH_A + 2 * WIDTH_B
WIDTH_C = D_MODEL
D_FF = 4 * D_MODEL
ROPE_BASE = 10000.0
ROPE_PAIRS = HEAD_DIM_A // 4
LN_EPS = 1e-5
ALPHA = (2 * DEPTH) ** 0.25
LAMBDA_INIT_0 = 0.8 - 0.6 * math.exp(-0.3 * 0)
Q_SCALE = HEAD_DIM_A ** -0.5 * math.log2(math.e)

LANES = 128
SUBLANES = 8
BF16_SUBLANES = 16
VMEM_LIMIT = 56 * 2 ** 20
N_MOD = 6 * D_MODEL
MOD_ROWS = 8
TOK_TILE = 1024
SUB_TILE = 512
CTX_SEQS = 4
CTX_LOOKAHEAD = 3
ATT_Q_BLOCK = 512
ATT_SUB_BLOCK = 128
ATT_KEY_CHUNK = 256
ATT_LOOKAHEAD = 5
FF_CHUNK = 1024
CONV_CHUNK = 256
MOD_TILE = 512


def _const_spec(shape):
    zeros = (0,) * len(shape)
    return pl.BlockSpec(shape, lambda *_: zeros, pipeline_mode=pl.Buffered(1))


def _params(*sem):
    return pltpu.CompilerParams(dimension_semantics=sem, vmem_limit_bytes=VMEM_LIMIT)


def _layer_norm(x, g, b):
    mu = jnp.mean(x, axis=-1, keepdims=True)
    xc = x - mu
    var = jnp.mean(xc * xc, axis=-1, keepdims=True)
    return xc * lax.rsqrt(var + LN_EPS) * g + b


def _mod(mod_ref, idx):
    return mod_ref[:, idx * D_MODEL:(idx + 1) * D_MODEL]


def _sub_tiles(ref):
    return [slice(i * SUB_TILE, (i + 1) * SUB_TILE) for i in range(ref.shape[0] // SUB_TILE)]


def _mod_kernel(c_ref, w0_ref, b0_ref, w1_ref, b1_ref, o0_ref, o1_ref):
    c = c_ref[...]
    s = (c * jax.nn.sigmoid(c)).astype(BF16)
    for w_ref, b_ref, o_ref in ((w0_ref, b0_ref, o0_ref), (w1_ref, b1_ref, o1_ref)):
        o_ref[...] = jnp.dot(s, w_ref[...].astype(BF16),
                             preferred_element_type=F32) + b_ref[...]


def _modulations(cvec, w0, b0, w1, b1):
    w_spec = pl.BlockSpec((D_MODEL, MOD_TILE), lambda j: (0, j))
    v_spec = pl.BlockSpec((1, MOD_TILE), lambda j: (0, j))
    o_spec = pl.BlockSpec((MOD_ROWS, MOD_TILE), lambda j: (0, j))
    out = jax.ShapeDtypeStruct((MOD_ROWS, N_MOD), F32)
    return pl.pallas_call(
        _mod_kernel,
        grid=(N_MOD // MOD_TILE,),
        in_specs=[_const_spec((MOD_ROWS, D_MODEL)), w_spec, v_spec, w_spec, v_spec],
        out_specs=[o_spec, o_spec],
        out_shape=[out, out],
        compiler_params=_params("parallel"),
        name="adaln_mod",
    )(cvec, w0, b0, w1, b1)


def _mod_spec(row_fn, tile):
    return pl.BlockSpec((None, 1, N_MOD), lambda i: (row_fn(i * tile), 0, 0))


def _in0_kernel(*refs, rope):
    if rope:
        (x_ref, mod_ref, w_ref, sw_ref, sbias_ref, cos_ref, sa_ref, sb_ref,
         q_ref, k_ref, v_ref, s_ref) = refs
    else:
        (x_ref, mod_ref, w_ref, sw_ref, sbias_ref,
         q_ref, k_ref, v_ref, s_ref, k_out_ref, v_out_ref) = refs
    subs = _sub_tiles(x_ref)
    ys = []
    for rows in subs:
        h = (x_ref[rows, :] * (1.0 + _mod(mod_ref, 1)) + _mod(mod_ref, 0)).astype(BF16)
        ys.append(jnp.dot(h, w_ref[...], preferred_element_type=F32))
    for si, (rows, y) in enumerate(zip(subs, ys)):
        if rope:
            cos, sa, sb = cos_ref[rows, :], sa_ref[rows, :], sb_ref[rows, :]
        for j in range(QK_WIDTH // LANES):
            lo, hi = j * LANES, (j + 1) * LANES
            qj = y[:, lo:hi]
            kj = y[:, QK_WIDTH + lo:QK_WIDTH + hi]
            if rope:
                qj = (qj * cos + pltpu.roll(qj, LANES - ROPE_PAIRS, 1) * sa
                      + pltpu.roll(qj, ROPE_PAIRS, 1) * sb)
                kj = (kj * cos + pltpu.roll(kj, LANES - ROPE_PAIRS, 1) * sa
                      + pltpu.roll(kj, ROPE_PAIRS, 1) * sb)
            q_ref[rows, lo:hi] = (qj * Q_SCALE).astype(q_ref.dtype)
            k_ref[rows, lo:hi] = kj.astype(k_ref.dtype)
            if not rope:
                kt = kj.T
                seqs = SUB_TILE // SEQ
                for b in range(seqs):
                    for i in range(2):
                        k_out_ref[si * seqs + b, j, i] = kt[i * HEAD_DIM_A:(i + 1) * HEAD_DIM_A,
                                                            b * SEQ:(b + 1) * SEQ]
        off = 2 * QK_WIDTH
        v = y[:, off:off + WIDTH_A]
        v_ref[:, rows] = v.T.astype(v_ref.dtype)
        if not rope:
            for j in range(N_HEADS_A):
                v_out_ref[rows, j, :] = v[:, j * V_DIM_A:(j + 1) * V_DIM_A]
        u_off = off + WIDTH_A
        g_off = u_off + WIDTH_B
        for g in range(N_GROUPS_B):
            lo, hi = g * GROUP_DIM_B, (g + 1) * GROUP_DIM_B
            gg = y[:, g_off + lo:g_off + hi]
            mu = jnp.mean(gg, axis=-1, keepdims=True)
            gc = gg - mu
            var = jnp.mean(gc * gc, axis=-1, keepdims=True)
            vc = (gc * lax.rsqrt(var + LN_EPS)).astype(BF16)
            w = sw_ref[g]
            bias = jnp.broadcast_to(sbias_ref[:, g:g + 1], (CHUNK, GROUP_DIM_B))
            for c in range(SUB_TILE // CHUNK):
                r0, r1 = c * CHUNK, (c + 1) * CHUNK
                mixed = jnp.dot(w, vc[r0:r1], preferred_element_type=F32) + bias
                s_ref[rows.start + r0:rows.start + r1, lo:hi] = (
                    y[r0:r1, u_off + lo:u_off + hi] * mixed).astype(s_ref.dtype)


def _in_proj0(x, mods, w_in, sgu_w, sgu_bt, row_fn, rope_tabs):
    n = x.shape[0]
    t = TOK_TILE
    tok = lambda width: pl.BlockSpec((t, width), lambda i: (i, 0))
    in_specs = [tok(D_MODEL), _mod_spec(row_fn, t), _const_spec((D_MODEL, IN_WIDTH_0)),
                _const_spec((N_GROUPS_B, CHUNK, CHUNK)), _const_spec((CHUNK, N_GROUPS_B))]
    args = [x, mods, w_in, sgu_w, sgu_bt]
    v_spec = pl.BlockSpec((WIDTH_A, t), lambda i: (0, i))
    v_shape = (WIDTH_A, n)
    if rope_tabs is not None:
        blocks_per_seq = DEC_SEQ // t
        tab = pl.BlockSpec((t, LANES), lambda i: (i % blocks_per_seq, 0))
        in_specs += [tab, tab, tab]
        args += list(rope_tabs)
        v_spec = pl.BlockSpec((WIDTH_A, t), lambda i: (i // blocks_per_seq, i % blocks_per_seq))
        v_shape = (n // DEC_SEQ * WIDTH_A, DEC_SEQ)
    out_shape = [jax.ShapeDtypeStruct((n, QK_WIDTH), BF16),
                 jax.ShapeDtypeStruct((n, QK_WIDTH), BF16),
                 jax.ShapeDtypeStruct(v_shape, BF16),
                 jax.ShapeDtypeStruct((n, WIDTH_B), BF16)]
    out_specs = [tok(QK_WIDTH), tok(QK_WIDTH), v_spec, tok(WIDTH_B)]
    if rope_tabs is None:
        out_shape += [jax.ShapeDtypeStruct((n // SEQ, N_HEADS_A, 2, HEAD_DIM_A, SEQ), F32),
                      jax.ShapeDtypeStruct((n, N_HEADS_A, V_DIM_A), F32)]
        out_specs += [pl.BlockSpec((t // SEQ, N_HEADS_A, 2, HEAD_DIM_A, SEQ),
                                   lambda i: (i, 0, 0, 0, 0)),
                      pl.BlockSpec((t, N_HEADS_A, V_DIM_A), lambda i: (i, 0, 0))]
    return pl.pallas_call(
        functools.partial(_in0_kernel, rope=rope_tabs is not None),
        grid=(n // t,),
        in_specs=in_specs,
        out_specs=out_specs,
        out_shape=out_shape,
        compiler_params=_params("parallel"),
        name="in_proj0",
    )(*args)


def _lambda(lp_ref):
    lp = lp_ref[...]
    a = jnp.sum(lp[0:1] * lp[1:2], axis=1, keepdims=True)
    b = jnp.sum(lp[2:3] * lp[3:4], axis=1, keepdims=True)
    return jnp.exp(a) - jnp.exp(b) + LAMBDA_INIT_0


def _stack_halves(qh):
    lane = lax.broadcasted_iota(jnp.int32, qh.shape, 1)
    zero = jnp.zeros_like(qh)
    return jnp.concatenate([jnp.where(lane < HEAD_DIM_A, qh, zero),
                            jnp.where(lane >= HEAD_DIM_A, qh, zero)], axis=0)


def _scores(qq, k):
    return lax.dot_general(qq, k, (((1,), (1,)), ((), ())), preferred_element_type=F32)


def _values_with_ones(vals_t):
    return jnp.concatenate([vals_t, jnp.ones((BF16_SUBLANES, vals_t.shape[1]), BF16)], axis=0)


def _finish_head(acc, rows, lam, gain):
    pvt = acc[:V_DIM_A]
    l = acc[V_DIM_A:V_DIM_A + 1]
    ot = pvt[:, :rows] / l[:, :rows] - lam * (pvt[:, rows:] / l[:, rows:])
    ms = jnp.mean(ot * ot, axis=0, keepdims=True)
    return (ot * lax.rsqrt(ms + LN_EPS) * gain * (1.0 - LAMBDA_INIT_0)).T


def _attn_ctx_kernel(q_ref, k_ref, vt_ref, lp_ref, gain_ref, o_ref):
    lam = _lambda(lp_ref)
    gain = gain_ref[...]
    units = [(b, h) for b in range(q_ref.shape[0] // SEQ) for h in range(N_HEADS_A)]
    scores = {}

    def issue_scores(unit):
        b, h = unit
        rows = slice(b * SEQ, (b + 1) * SEQ)
        lo, hi = h * LANES, (h + 1) * LANES
        scores[unit] = _scores(k_ref[rows, lo:hi], _stack_halves(q_ref[rows, lo:hi]))

    for unit in units[:CTX_LOOKAHEAD]:
        issue_scores(unit)
    for i, unit in enumerate(units):
        if i + CTX_LOOKAHEAD < len(units):
            issue_scores(units[i + CTX_LOOKAHEAD])
        b, h = unit
        rows = slice(b * SEQ, (b + 1) * SEQ)
        lo, hi = h * LANES, (h + 1) * LANES
        st = scores.pop(unit)
        pt = jnp.exp2(st - jnp.max(st, axis=0, keepdims=True)).astype(BF16)
        acc = jnp.dot(_values_with_ones(vt_ref[lo:hi, rows]), pt, preferred_element_type=F32)
        o_ref[rows, lo:hi] = _finish_head(acc, SEQ, lam, gain).astype(o_ref.dtype)


def _attn_ctx(q, k, vt, lam_params, gain_col):
    n = q.shape[0]
    t = CTX_SEQS * SEQ
    blk = lambda: pl.BlockSpec((t, QK_WIDTH), lambda b: (b, 0))
    return pl.pallas_call(
        _attn_ctx_kernel,
        grid=(n // t,),
        in_specs=[blk(), blk(), pl.BlockSpec((WIDTH_A, t), lambda b: (0, b)),
                  _const_spec((4, HEAD_DIM_A)), _const_spec((V_DIM_A, 1))],
        out_specs=blk(),
        out_shape=jax.ShapeDtypeStruct((n, WIDTH_A), BF16),
        compiler_params=_params("parallel"),
        name="attn_ctx",
    )(q, k, vt, lam_params, gain_col)


def _attn_lat_kernel(q_ref, k_ref, vt_ref, ck_ref, cvt_ref, lp_ref, gain_ref, *rest, n_weights):
    w_refs, o_ref, wb_refs = rest[:n_weights], rest[n_weights], rest[n_weights + 1:]
    for w_ref, wb_ref in zip(w_refs, wb_refs):
        wb_ref[...] = w_ref[...].astype(wb_ref.dtype)
    lam = _lambda(lp_ref)
    gain = gain_ref[...]
    rows = ATT_SUB_BLOCK
    chunks = [(None, PAST_LEN)] + [(s, ATT_KEY_CHUNK) for s in range(0, DEC_SEQ, ATT_KEY_CHUNK)]
    units = [(sub, h, c) for sub in range(q_ref.shape[0] // rows)
             for h in range(N_HEADS_A) for c in range(len(chunks))]
    qq_of = {}
    scores = {}

    def issue_scores(unit):
        sub, h, c = unit
        lo, hi = h * LANES, (h + 1) * LANES
        if c == 0:
            qq_of[sub, h] = _stack_halves(q_ref[sub * rows:(sub + 1) * rows, lo:hi])
        start, size = chunks[c]
        keys = ck_ref[:, lo:hi].astype(BF16) if start is None else k_ref[start:start + size, lo:hi]
        scores[unit] = _scores(keys, qq_of[sub, h])

    for unit in units[:ATT_LOOKAHEAD]:
        issue_scores(unit)
    m = acc = None
    for i, unit in enumerate(units):
        if i + ATT_LOOKAHEAD < len(units):
            issue_scores(units[i + ATT_LOOKAHEAD])
        sub, h, c = unit
        lo, hi = h * LANES, (h + 1) * LANES
        start, size = chunks[c]
        vals_t = cvt_ref[lo:hi, :] if start is None else vt_ref[lo:hi, start:start + size]
        st = scores.pop(unit)
        cm = jnp.max(st, axis=0, keepdims=True)
        m_new = cm if c == 0 else jnp.maximum(m, cm)
        pt = jnp.exp2(st - m_new).astype(BF16)
        part = jnp.dot(_values_with_ones(vals_t), pt,
                       preferred_element_type=F32)
        acc = part if c == 0 else acc * jnp.exp2(m - m_new) + part
        m = m_new
        if c == len(chunks) - 1:
            o_ref[sub * rows:(sub + 1) * rows, lo:hi] = _finish_head(
                acc, rows, lam, gain).astype(o_ref.dtype)


def _attn_lat(q, k, vt, cache_k, cache_vt, lam_params, gain_col, weights):
    n = q.shape[0]
    nb = n // DEC_SEQ
    qb = DEC_SEQ // ATT_Q_BLOCK
    steps = nb * qb
    q_spec = pl.BlockSpec((ATT_Q_BLOCK, QK_WIDTH), lambda b, j: (b * qb + j, 0))
    k_spec = pl.BlockSpec((DEC_SEQ, QK_WIDTH), lambda b, j: (b, 0))
    vt_spec = pl.BlockSpec((WIDTH_A, DEC_SEQ), lambda b, j: (b, 0))
    ck_spec = pl.BlockSpec((PAST_LEN, QK_WIDTH), lambda b, j: (b, 0))
    cvt_spec = pl.BlockSpec((WIDTH_A, PAST_LEN), lambda b, j: (b, 0))
    w_specs = [pl.BlockSpec((w.shape[0] // steps, w.shape[1]), lambda b, j: (b * qb + j, 0))
               for w in weights]
    assert all(w.shape[0] % (steps * BF16_SUBLANES) == 0 for w in weights)
    outs = pl.pallas_call(
        functools.partial(_attn_lat_kernel, n_weights=len(weights)),
        grid=(nb, qb),
        in_specs=[q_spec, k_spec, vt_spec, ck_spec, cvt_spec,
                  _const_spec((4, HEAD_DIM_A)), _const_spec((V_DIM_A, 1))] + w_specs,
        out_specs=[q_spec] + w_specs,
        out_shape=[jax.ShapeDtypeStruct((n, WIDTH_A), BF16)]
        + [jax.ShapeDtypeStruct(w.shape, BF16) for w in weights],
        compiler_params=_params("parallel", "arbitrary"),
        name="attn_lat",
    )(q, k, vt, cache_k, cache_vt, lam_params, gain_col, *weights)
    return outs[0], outs[1:]


def _conv1_kernel(xp_ref, x_ref, xn_ref, mod_ref, wi_ref, cw_ref, o_ref, z_scr, *, seq_len):
    t = x_ref.shape[0]
    halo = SUBLANES
    cc = CONV_CHUNK
    n_chunks = WIDTH_C // cc
    xin = jnp.concatenate([xp_ref[...], x_ref[...], xn_ref[...]], axis=0)
    h = (xin * (1.0 + _mod(mod_ref, 1)) + _mod(mod_ref, 0)).astype(BF16)
    pos = (pl.program_id(0) * t + lax.broadcasted_iota(jnp.int32, (t, 1), 0)) % seq_len
    not_first = pos != 0
    not_last = pos != seq_len - 1

    def project(j):
        return [jnp.dot(h, wi_ref[:, sec * WIDTH_C + j * cc:sec * WIDTH_C + (j + 1) * cc],
                        preferred_element_type=F32) for sec in range(3)]

    y_next = project(0)
    for j in range(n_chunks):
        gate, conv_gate, conv_in = y_next
        if j + 1 < n_chunks:
            y_next = project(j + 1)
        z_scr[j] = conv_gate * conv_in
        cw = cw_ref[:, j * cc:(j + 1) * cc]
        z_prev = jnp.where(not_first, z_scr[j, halo - 1:halo - 1 + t, :], 0.0)
        z_next = jnp.where(not_last, z_scr[j, halo + 1:halo + 1 + t, :], 0.0)
        conv = z_prev * cw[0:1] + z_scr[j, halo:halo + t, :] * cw[1:2] + z_next * cw[2:3]
        o_ref[:, j * cc:(j + 1) * cc] = (gate[halo:halo + t] * conv).astype(o_ref.dtype)


def _conv1(x, mods, row_fn, w_in, conv_w, seq_len):
    n = x.shape[0]
    t = TOK_TILE
    tiles_per_block = t // SUBLANES
    last = n // SUBLANES - 1
    tok = pl.BlockSpec((t, D_MODEL), lambda i: (i, 0))
    prev = pl.BlockSpec((SUBLANES, D_MODEL),
                        lambda i: (jnp.maximum(i * tiles_per_block - 1, 0), 0))
    nxt = pl.BlockSpec((SUBLANES, D_MODEL),
                       lambda i: (jnp.minimum((i + 1) * tiles_per_block, last), 0))
    return pl.pallas_call(
        functools.partial(_conv1_kernel, seq_len=seq_len),
        grid=(n // t,),
        in_specs=[prev, tok, nxt, _mod_spec(row_fn, t),
                  _const_spec((D_MODEL, 3 * WIDTH_C)), _const_spec((3, WIDTH_C))],
        out_specs=pl.BlockSpec((t, WIDTH_C), lambda i: (i, 0)),
        out_shape=jax.ShapeDtypeStruct((n, WIDTH_C), BF16),
        scratch_shapes=[pltpu.VMEM((WIDTH_C // CONV_CHUNK, t + 2 * SUBLANES, CONV_CHUNK), F32)],
        compiler_params=_params("parallel"),
        name="conv1",
    )(x, x, x, mods, w_in, conv_w)


def _post_kernel(*refs, n_pieces):
    pieces = refs[:n_pieces]
    (x_ref, mod_ref, wo_ref, g1_ref, b1_ref, w1_ref, w2_ref, g2_ref, b2_ref, o_ref) = refs[n_pieces:]
    subs = _sub_tiles(x_ref)
    mixed = []
    for rows in subs:
        out = None
        k0 = 0
        for p_ref in pieces:
            k1 = k0 + p_ref.shape[1]
            part = jnp.dot(p_ref[rows, :], wo_ref[k0:k1, :], preferred_element_type=F32)
            out = part if out is None else out + part
            k0 = k1
        mixed.append(_layer_norm(ALPHA * x_ref[rows, :] + _mod(mod_ref, 2) * out,
                                 g1_ref[...], b1_ref[...]))
    hs = [(x * (1.0 + _mod(mod_ref, 4)) + _mod(mod_ref, 3)).astype(BF16) for x in mixed]
    fs = [None] * len(subs)
    for j in range(D_FF // FF_CHUNK):
        lo, hi = j * FF_CHUNK, (j + 1) * FF_CHUNK
        hids = [jnp.dot(h, w1_ref[:, lo:hi], preferred_element_type=F32) for h in hs]
        for s, hid in enumerate(hids):
            hid = jnp.square(jnp.maximum(hid, 0.0)).astype(BF16)
            part = jnp.dot(hid, w2_ref[lo:hi, :], preferred_element_type=F32)
            fs[s] = part if fs[s] is None else fs[s] + part
    for rows, x, f in zip(subs, mixed, fs):
        o_ref[rows, :] = _layer_norm(ALPHA * x + _mod(mod_ref, 5) * f, g2_ref[...], b2_ref[...])


def _post(pieces, x, mods, row_fn, w_out, g1, b1, w1, w2, g2, b2):
    n = x.shape[0]
    t = TOK_TILE
    tok = lambda width: pl.BlockSpec((t, width), lambda i: (i, 0))
    vec = _const_spec((1, D_MODEL))
    return pl.pallas_call(
        functools.partial(_post_kernel, n_pieces=len(pieces)),
        grid=(n // t,),
        in_specs=[tok(p.shape[1]) for p in pieces] + [
            tok(D_MODEL), _mod_spec(row_fn, t), _const_spec(w_out.shape), vec, vec,
            _const_spec((D_MODEL, D_FF)), _const_spec((D_FF, D_MODEL)), vec, vec],
        out_specs=tok(D_MODEL),
        out_shape=jax.ShapeDtypeStruct((n, D_MODEL), F32),
        compiler_params=_params("parallel"),
        name="post",
    )(*pieces, x, mods, w_out, g1, b1, w1, w2, g2, b2)


def _rope_tables(n):
    f32 = np.float32
    rows = n // GRID_W
    row = np.repeat(np.arange(rows, dtype=f32), GRID_W)
    col = np.tile(np.arange(GRID_W, dtype=f32), rows)
    inv = (f32(1.0) / (f32(ROPE_BASE) ** (np.arange(ROPE_PAIRS, dtype=f32) / f32(ROPE_PAIRS)))).astype(f32)
    ang_r, ang_c = (row[:, None] * inv).astype(f32), (col[:, None] * inv).astype(f32)
    cos_r, sin_r, cos_c, sin_c = (fn(a.astype(np.float64)).astype(f32)
                                  for a in (ang_r, ang_c) for fn in (np.cos, np.sin))
    zero = np.zeros_like(ang_r)
    reps = LANES // HEAD_DIM_A
    cos = np.tile(np.concatenate([cos_r, cos_r, cos_c, cos_c], axis=1), (1, reps))
    sa = np.tile(np.concatenate([-sin_r, zero, -sin_c, zero], axis=1), (1, reps))
    sb = np.tile(np.concatenate([zero, sin_r, zero, sin_c], axis=1), (1, reps))
    return jnp.asarray(cos), jnp.asarray(sa), jnp.asarray(sb)


def kernel(x_prompt, x_sample, cache_k0, cache_v0, c, c_ctx, w_mod0, b_mod0, w_in0, lambda_q1_0, lambda_k1_0, lambda_q2_0, lambda_k2_0, subln_g0, sgu_w0, sgu_b0, w_out0, ln_mix_g0, ln_mix_b0, w_ff1_0, w_ff2_0, ln_ff_g0, ln_ff_b0, w_mod1, b_mod1, w_in1, conv_w1, w_out1, ln_mix_g1, ln_mix_b1, w_ff1_1, w_ff2_1, ln_ff_g1, ln_ff_b1):
    batch, seq, d = x_prompt.shape
    dec_batch, dec_seq, _ = x_sample.shape
    assert (seq, dec_seq, d) == (SEQ, DEC_SEQ, D_MODEL) and 1 + dec_batch <= MOD_ROWS
    row = lambda v: v.reshape(1, -1)

    cvec = jnp.concatenate([c_ctx[None, :], c, jnp.zeros((MOD_ROWS - 1 - dec_batch, d), F32)], axis=0)
    mods0, mods1 = _modulations(cvec, w_mod0, row(b_mod0), w_mod1, row(b_mod1))
    mods0 = mods0.reshape(MOD_ROWS, 1, N_MOD)
    mods1 = mods1.reshape(MOD_ROWS, 1, N_MOD)
    ctx_row = lambda tok: 0
    lat_row = lambda tok: 1 + tok // DEC_SEQ

    xp = x_prompt.reshape(batch * seq, d)
    xs = x_sample.reshape(dec_batch * dec_seq, d)

    w_in0_b = w_in0.astype(BF16)
    sgu = (sgu_w0.astype(BF16), sgu_b0.T)
    lam_params = jnp.stack([lambda_q1_0, lambda_k1_0, lambda_q2_0, lambda_k2_0])
    gain = subln_g0.reshape(V_DIM_A, 1)
    qp, kp, vp, sp, new_k0, new_v0 = _in_proj0(xp, mods0, w_in0_b, *sgu, ctx_row, None)
    qs, ks, vs, ss = _in_proj0(xs, mods0, w_in0_b, *sgu, lat_row, _rope_tables(dec_seq))
    ap = _attn_ctx(qp, kp, vp, lam_params, gain)
    cache_vt = jnp.swapaxes(cache_v0.reshape(dec_batch, PAST_LEN, WIDTH_A), 1, 2)
    later = (w_ff1_0, w_ff2_0, w_in1, w_out1, w_ff1_1, w_ff2_1)
    a_s, later_b = _attn_lat(qs, ks, vs, cache_k0.reshape(dec_batch * PAST_LEN, QK_WIDTH),
                             cache_vt.reshape(dec_batch * WIDTH_A, PAST_LEN).astype(BF16),
                             lam_params, gain, later)
    w_ff1_0_b, w_ff2_0_b, w_in1_b, w_out1_b, w_ff1_1_b, w_ff2_1_b = later_b
    post0_w = (w_out0.astype(BF16), row(ln_mix_g0), row(ln_mix_b0),
               w_ff1_0_b, w_ff2_0_b, row(ln_ff_g0), row(ln_ff_b0))
    xp = _post([ap, sp], xp, mods0, ctx_row, *post0_w)
    xs = _post([a_s, ss], xs, mods0, lat_row, *post0_w)

    gp = _conv1(xp, mods1, ctx_row, w_in1_b, conv_w1, seq_len=seq)
    gs = _conv1(xs, mods1, lat_row, w_in1_b, conv_w1, seq_len=dec_seq)
    post1_w = (w_out1_b, row(ln_mix_g1), row(ln_mix_b1),
               w_ff1_1_b, w_ff2_1_b, row(ln_ff_g1), row(ln_ff_b1))
    xp = _post([gp], xp, mods1, ctx_row, *post1_w)
    xs = _post([gs], xs, mods1, lat_row, *post1_w)

    return (xp.reshape(batch, seq, d), xs.reshape(dec_batch, dec_seq, d),
            jnp.transpose(new_k0, (0, 4, 1, 2, 3)),
            new_v0.reshape(batch, seq, N_HEADS_A, V_DIM_A))
```

```python
import functools
import math

import jax
import jax.numpy as jnp
import numpy as np
from jax import lax
from jax.experimental import pallas as pl
from jax.experimental.pallas import tpu as pltpu

F32 = jnp.float32
BF16 = jnp.bfloat16

D_MODEL = 1024
DEPTH = 2
SEQ = 256
DEC_SEQ = 4096
PAST_LEN = 256
GRID_W = 64
N_HEADS_A = 4
HEAD_DIM_A = 64
V_DIM_A = 2 * HEAD_DIM_A
QK_WIDTH = N_HEADS_A * 2 * HEAD_DIM_A
WIDTH_A = N_HEADS_A * V_DIM_A
N_GROUPS_B = 4
CHUNK = 128
GROUP_DIM_B = 128
WIDTH_B = N_GROUPS_B * GROUP_DIM_B
IN_WIDTH_0 = 2 * QK_WIDTH + WIDTH_A + 2 * WIDTH_B
WIDTH_C = D_MODEL
D_FF = 4 * D_MODEL
ROPE_BASE = 10000.0
ROPE_PAIRS = HEAD_DIM_A // 4
LN_EPS = 1e-5
ALPHA = (2 * DEPTH) ** 0.25
LAMBDA_INIT_0 = 0.8 - 0.6 * math.exp(-0.3 * 0)
N_LAMBDA_VECS = 4
Q_SCALE = HEAD_DIM_A ** -0.5 * math.log2(math.e)

LANES = 128
SUBLANES = 8
BF16_SUBLANES = 16
VMEM_LIMIT = 56 * 2 ** 20
N_MOD = 6 * D_MODEL
MOD_ROWS = 8
TOK_TILE = 1024
CONV_TILE = 2048
SUB_TILE = 512
CTX_SEQS = 4
CTX_LOOKAHEAD = 3
ATT_Q_BLOCK = 1024
ATT_SUB_BLOCK = 128
ATT_KEY_CHUNK = 256
ATT_LOOKAHEAD = 5
FF_CHUNK = 1024
CONV_CHUNK = 256
MOD_TILE = 2048


def _const_spec(shape):
    zeros = (0,) * len(shape)
    return pl.BlockSpec(shape, lambda *_: zeros, pipeline_mode=pl.Buffered(1))


def _params(*sem):
    return pltpu.CompilerParams(dimension_semantics=sem, vmem_limit_bytes=VMEM_LIMIT)


def _layer_norm(x, g, b):
    mu = jnp.mean(x, axis=-1, keepdims=True)
    xc = x - mu
    var = jnp.mean(xc * xc, axis=-1, keepdims=True)
    return xc * lax.rsqrt(var + LN_EPS) * g + b


def _mod(mod_ref, idx):
    return mod_ref[:, idx * D_MODEL:(idx + 1) * D_MODEL]


def _sub_tiles(ref):
    return [slice(i * SUB_TILE, (i + 1) * SUB_TILE) for i in range(ref.shape[0] // SUB_TILE)]


def _mod_kernel(c_ref, w0_ref, b0_ref, w1_ref, b1_ref, o0_ref, o1_ref):
    c = c_ref[...]
    s = (c * jax.nn.sigmoid(c)).astype(BF16)
    for w_ref, b_ref, o_ref in ((w0_ref, b0_ref, o0_ref), (w1_ref, b1_ref, o1_ref)):
        o_ref[...] = jnp.dot(s, w_ref[...].astype(BF16),
                             preferred_element_type=F32) + b_ref[...]


def _modulations(cvec, w0, b0, w1, b1):
    w_spec = pl.BlockSpec((D_MODEL, MOD_TILE), lambda j: (0, j))
    v_spec = pl.BlockSpec((1, MOD_TILE), lambda j: (0, j))
    o_spec = pl.BlockSpec((MOD_ROWS, MOD_TILE), lambda j: (0, j))
    out = jax.ShapeDtypeStruct((MOD_ROWS, N_MOD), F32)
    return pl.pallas_call(
        _mod_kernel,
        grid=(N_MOD // MOD_TILE,),
        in_specs=[_const_spec((MOD_ROWS, D_MODEL)), w_spec, v_spec, w_spec, v_spec],
        out_specs=[o_spec, o_spec],
        out_shape=[out, out],
        compiler_params=_params("parallel"),
        name="adaln_mod",
    )(cvec, w0, b0, w1, b1)


def _mod_spec(row_fn, tile):
    return pl.BlockSpec((None, 1, N_MOD), lambda i: (row_fn(i * tile), 0, 0))


def _in0_kernel(*refs, rope):
    if rope:
        (x_ref, mod_ref, w_ref, sw_ref, sbias_ref, cos_ref, sa_ref, sb_ref,
         q_ref, k_ref, v_ref, s_ref) = refs
    else:
        (x_ref, mod_ref, w_ref, sw_ref, sbias_ref,
         q_ref, k_ref, v_ref, s_ref, k_out_ref, v_out_ref) = refs
    subs = _sub_tiles(x_ref)
    ys = []
    for rows in subs:
        h = (x_ref[rows, :] * (1.0 + _mod(mod_ref, 1)) + _mod(mod_ref, 0)).astype(BF16)
        ys.append(jnp.dot(h, w_ref[...], preferred_element_type=F32))
    for si, (rows, y) in enumerate(zip(subs, ys)):
        if rope:
            cos, sa, sb = cos_ref[rows, :], sa_ref[rows, :], sb_ref[rows, :]
        for j in range(QK_WIDTH // LANES):
            lo, hi = j * LANES, (j + 1) * LANES
            qj = y[:, lo:hi]
            kj = y[:, QK_WIDTH + lo:QK_WIDTH + hi]
            if rope:
                qj = (qj * cos + pltpu.roll(qj, LANES - ROPE_PAIRS, 1) * sa
                      + pltpu.roll(qj, ROPE_PAIRS, 1) * sb)
                kj = (kj * cos + pltpu.roll(kj, LANES - ROPE_PAIRS, 1) * sa
                      + pltpu.roll(kj, ROPE_PAIRS, 1) * sb)
            q_ref[rows, lo:hi] = (qj * Q_SCALE).astype(q_ref.dtype)
            k_ref[rows, lo:hi] = kj.astype(k_ref.dtype)
            if not rope:
                kt = kj.T
                seqs = SUB_TILE // SEQ
                for b in range(seqs):
                    for i in range(2):
                        k_out_ref[si * seqs + b, j, i] = kt[i * HEAD_DIM_A:(i + 1) * HEAD_DIM_A,
                                                            b * SEQ:(b + 1) * SEQ]
        off = 2 * QK_WIDTH
        v = y[:, off:off + WIDTH_A]
        v_ref[:, rows] = v.T.astype(v_ref.dtype)
        if not rope:
            for j in range(N_HEADS_A):
                v_out_ref[rows, j, :] = v[:, j * V_DIM_A:(j + 1) * V_DIM_A]
        u_off = off + WIDTH_A
        g_off = u_off + WIDTH_B
        for g in range(N_GROUPS_B):
            lo, hi = g * GROUP_DIM_B, (g + 1) * GROUP_DIM_B
            gg = y[:, g_off + lo:g_off + hi]
            mu = jnp.mean(gg, axis=-1, keepdims=True)
            gc = gg - mu
            var = jnp.mean(gc * gc, axis=-1, keepdims=True)
            vc = (gc * lax.rsqrt(var + LN_EPS)).astype(BF16)
            w = sw_ref[g]
            bias = jnp.broadcast_to(sbias_ref[:, g:g + 1], (CHUNK, GROUP_DIM_B))
            for c in range(SUB_TILE // CHUNK):
                r0, r1 = c * CHUNK, (c + 1) * CHUNK
                mixed = jnp.dot(w, vc[r0:r1], preferred_element_type=F32) + bias
                s_ref[rows.start + r0:rows.start + r1, lo:hi] = (
                    y[r0:r1, u_off + lo:u_off + hi] * mixed).astype(s_ref.dtype)


def _in_proj0(x, mods, w_in, sgu_w, sgu_bt, row_fn, rope_tabs):
    n = x.shape[0]
    t = TOK_TILE
    tok = lambda width: pl.BlockSpec((t, width), lambda i: (i, 0))
    in_specs = [tok(D_MODEL), _mod_spec(row_fn, t), _const_spec((D_MODEL, IN_WIDTH_0)),
                _const_spec((N_GROUPS_B, CHUNK, CHUNK)), _const_spec((CHUNK, N_GROUPS_B))]
    args = [x, mods, w_in, sgu_w, sgu_bt]
    v_spec = pl.BlockSpec((WIDTH_A, t), lambda i: (0, i))
    v_shape = (WIDTH_A, n)
    if rope_tabs is not None:
        blocks_per_seq = DEC_SEQ // t
        tab = pl.BlockSpec((t, LANES), lambda i: (i % blocks_per_seq, 0))
        in_specs += [tab, tab, tab]
        args += list(rope_tabs)
        v_spec = pl.BlockSpec((WIDTH_A, t), lambda i: (i // blocks_per_seq, i % blocks_per_seq))
        v_shape = (n // DEC_SEQ * WIDTH_A, DEC_SEQ)
    out_shape = [jax.ShapeDtypeStruct((n, QK_WIDTH), BF16),
                 jax.ShapeDtypeStruct((n, QK_WIDTH), BF16),
                 jax.ShapeDtypeStruct(v_shape, BF16),
                 jax.ShapeDtypeStruct((n, WIDTH_B), BF16)]
    out_specs = [tok(QK_WIDTH), tok(QK_WIDTH), v_spec, tok(WIDTH_B)]
    if rope_tabs is None:
        out_shape += [jax.ShapeDtypeStruct((n // SEQ, N_HEADS_A, 2, HEAD_DIM_A, SEQ), F32),
                      jax.ShapeDtypeStruct((n, N_HEADS_A, V_DIM_A), F32)]
        out_specs += [pl.BlockSpec((t // SEQ, N_HEADS_A, 2, HEAD_DIM_A, SEQ),
                                   lambda i: (i, 0, 0, 0, 0)),
                      pl.BlockSpec((t, N_HEADS_A, V_DIM_A), lambda i: (i, 0, 0))]
    return pl.pallas_call(
        functools.partial(_in0_kernel, rope=rope_tabs is not None),
        grid=(n // t,),
        in_specs=in_specs,
        out_specs=out_specs,
        out_shape=out_shape,
        compiler_params=_params("parallel"),
        name="in_proj0",
    )(*args)


def _lambda(lp_ref):
    lp = lp_ref[...]
    a = jnp.sum(lp[0:1] * lp[1:2], axis=1, keepdims=True)
    b = jnp.sum(lp[2:3] * lp[3:4], axis=1, keepdims=True)
    return jnp.exp(a) - jnp.exp(b) + LAMBDA_INIT_0


def _stack_halves(qh):
    lane = lax.broadcasted_iota(jnp.int32, qh.shape, 1)
    zero = jnp.zeros_like(qh)
    return jnp.concatenate([jnp.where(lane < HEAD_DIM_A, qh, zero),
                            jnp.where(lane >= HEAD_DIM_A, qh, zero)], axis=0)


def _scores(qq, k):
    return lax.dot_general(qq, k, (((1,), (1,)), ((), ())), preferred_element_type=F32)


def _values_with_ones(vals_t):
    return jnp.concatenate([vals_t, jnp.ones((BF16_SUBLANES, vals_t.shape[1]), BF16)], axis=0)


def _finish_head(acc, rows, lam, gain):
    pvt = acc[:V_DIM_A]
    l = acc[V_DIM_A:V_DIM_A + 1]
    ot = pvt[:, :rows] / l[:, :rows] - lam * (pvt[:, rows:] / l[:, rows:])
    ms = jnp.mean(ot * ot, axis=0, keepdims=True)
    return (ot * lax.rsqrt(ms + LN_EPS) * gain * (1.0 - LAMBDA_INIT_0)).T


def _attn_ctx_kernel(q_ref, k_ref, vt_ref, lp_ref, gain_ref, o_ref):
    lam = _lambda(lp_ref)
    gain = gain_ref[...]
    units = [(b, h) for b in range(q_ref.shape[0] // SEQ) for h in range(N_HEADS_A)]
    scores = {}

    def issue_scores(unit):
        b, h = unit
        rows = slice(b * SEQ, (b + 1) * SEQ)
        lo, hi = h * LANES, (h + 1) * LANES
        scores[unit] = _scores(k_ref[rows, lo:hi], _stack_halves(q_ref[rows, lo:hi]))

    for unit in units[:CTX_LOOKAHEAD]:
        issue_scores(unit)
    for i, unit in enumerate(units):
        if i + CTX_LOOKAHEAD < len(units):
            issue_scores(units[i + CTX_LOOKAHEAD])
        b, h = unit
        rows = slice(b * SEQ, (b + 1) * SEQ)
        lo, hi = h * LANES, (h + 1) * LANES
        st = scores.pop(unit)
        pt = jnp.exp2(st - jnp.max(st, axis=0, keepdims=True)).astype(BF16)
        acc = jnp.dot(_values_with_ones(vt_ref[lo:hi, rows]), pt, preferred_element_type=F32)
        o_ref[rows, lo:hi] = _finish_head(acc, SEQ, lam, gain).astype(o_ref.dtype)


def _attn_ctx(q, k, vt, lam_params, gain_col):
    n = q.shape[0]
    t = CTX_SEQS * SEQ
    blk = lambda: pl.BlockSpec((t, QK_WIDTH), lambda b: (b, 0))
    return pl.pallas_call(
        _attn_ctx_kernel,
        grid=(n // t,),
        in_specs=[blk(), blk(), pl.BlockSpec((WIDTH_A, t), lambda b: (0, b)),
                  _const_spec((N_LAMBDA_VECS, HEAD_DIM_A)), _const_spec((V_DIM_A, 1))],
        out_specs=blk(),
        out_shape=jax.ShapeDtypeStruct((n, WIDTH_A), BF16),
        compiler_params=_params("parallel"),
        name="attn_ctx",
    )(q, k, vt, lam_params, gain_col)


def _attn_lat_kernel(q_ref, k_ref, vt_ref, ck_ref, cvt_ref, lp_ref, gain_ref, *rest, n_weights):
    w_refs, o_ref, wb_refs = rest[:n_weights], rest[n_weights], rest[n_weights + 1:]
    for w_ref, wb_ref in zip(w_refs, wb_refs):
        wb_ref[...] = w_ref[...].astype(wb_ref.dtype)
    lam = _lambda(lp_ref)
    gain = gain_ref[...]
    rows = ATT_SUB_BLOCK
    chunks = [(None, PAST_LEN)] + [(s, ATT_KEY_CHUNK) for s in range(0, DEC_SEQ, ATT_KEY_CHUNK)]
    units = [(sub, h, c) for sub in range(q_ref.shape[0] // rows)
             for h in range(N_HEADS_A) for c in range(len(chunks))]
    qq_of = {}
    scores = {}

    def issue_scores(unit):
        sub, h, c = unit
        lo, hi = h * LANES, (h + 1) * LANES
        if c == 0:
            qq_of[sub, h] = _stack_halves(q_ref[sub * rows:(sub + 1) * rows, lo:hi])
        start, size = chunks[c]
        keys = ck_ref[:, lo:hi].astype(BF16) if start is None else k_ref[start:start + size, lo:hi]
        scores[unit] = _scores(keys, qq_of[sub, h])

    for unit in units[:ATT_LOOKAHEAD]:
        issue_scores(unit)
    m = acc = None
    for i, unit in enumerate(units):
        if i + ATT_LOOKAHEAD < len(units):
            issue_scores(units[i + ATT_LOOKAHEAD])
        sub, h, c = unit
        lo, hi = h * LANES, (h + 1) * LANES
        start, size = chunks[c]
        vals_t = cvt_ref[lo:hi, :] if start is None else vt_ref[lo:hi, start:start + size]
        st = scores.pop(unit)
        cm = jnp.max(st, axis=0, keepdims=True)
        m_new = cm if c == 0 else jnp.maximum(m, cm)
        pt = jnp.exp2(st - m_new).astype(BF16)
        part = jnp.dot(_values_with_ones(vals_t), pt,
                       preferred_element_type=F32)
        acc = part if c == 0 else acc * jnp.exp2(m - m_new) + part
        m = m_new
        if c == len(chunks) - 1:
            o_ref[sub * rows:(sub + 1) * rows, lo:hi] = _finish_head(
                acc, rows, lam, gain).astype(o_ref.dtype)


def _attn_lat(q, k, vt, cache_k, cache_vt, lam_params, gain_col, weights):
    n = q.shape[0]
    nb = n // DEC_SEQ
    qb = DEC_SEQ // ATT_Q_BLOCK
    steps = nb * qb
    q_spec = pl.BlockSpec((ATT_Q_BLOCK, QK_WIDTH), lambda b, j: (b * qb + j, 0))
    k_spec = pl.BlockSpec((DEC_SEQ, QK_WIDTH), lambda b, j: (b, 0))
    vt_spec = pl.BlockSpec((WIDTH_A, DEC_SEQ), lambda b, j: (b, 0))
    ck_spec = pl.BlockSpec((PAST_LEN, QK_WIDTH), lambda b, j: (b, 0))
    cvt_spec = pl.BlockSpec((WIDTH_A, PAST_LEN), lambda b, j: (b, 0))
    w_specs = [pl.BlockSpec((w.shape[0] // steps, w.shape[1]), lambda b, j: (b * qb + j, 0))
               for w in weights]
    assert all(w.shape[0] % (steps * BF16_SUBLANES) == 0 for w in weights)
    outs = pl.pallas_call(
        functools.partial(_attn_lat_kernel, n_weights=len(weights)),
        grid=(nb, qb),
        in_specs=[q_spec, k_spec, vt_spec, ck_spec, cvt_spec,
                  _const_spec((N_LAMBDA_VECS, HEAD_DIM_A)), _const_spec((V_DIM_A, 1))] + w_specs,
        out_specs=[q_spec] + w_specs,
        out_shape=[jax.ShapeDtypeStruct((n, WIDTH_A), BF16)]
        + [jax.ShapeDtypeStruct(w.shape, BF16) for w in weights],
        compiler_params=_params("parallel", "arbitrary"),
        name="attn_lat",
    )(q, k, vt, cache_k, cache_vt, lam_params, gain_col, *weights)
    return outs[0], outs[1:]


def _conv1_kernel(xp_ref, x_ref, xn_ref, mod_ref, wi_ref, cw_ref, o_ref, z_scr, *, seq_len):
    t = x_ref.shape[0]
    halo = SUBLANES
    cc = CONV_CHUNK
    n_chunks = WIDTH_C // cc
    xin = jnp.concatenate([xp_ref[...], x_ref[...], xn_ref[...]], axis=0)
    h = (xin * (1.0 + _mod(mod_ref, 1)) + _mod(mod_ref, 0)).astype(BF16)
    pos = (pl.program_id(0) * t + lax.broadcasted_iota(jnp.int32, (t, 1), 0)) % seq_len
    not_first = pos != 0
    not_last = pos != seq_len - 1

    def project(j):
        return [jnp.dot(h, wi_ref[:, sec * WIDTH_C + j * cc:sec * WIDTH_C + (j + 1) * cc],
                        preferred_element_type=F32) for sec in range(3)]

    y_next = project(0)
    for j in range(n_chunks):
        gate, conv_gate, conv_in = y_next
        if j + 1 < n_chunks:
            y_next = project(j + 1)
        z_scr[j] = conv_gate * conv_in
        cw = cw_ref[:, j * cc:(j + 1) * cc]
        z_prev = jnp.where(not_first, z_scr[j, halo - 1:halo - 1 + t, :], 0.0)
        z_next = jnp.where(not_last, z_scr[j, halo + 1:halo + 1 + t, :], 0.0)
        conv = z_prev * cw[0:1] + z_scr[j, halo:halo + t, :] * cw[1:2] + z_next * cw[2:3]
        o_ref[:, j * cc:(j + 1) * cc] = (gate[halo:halo + t] * conv).astype(o_ref.dtype)


def _conv1(x, mods, row_fn, w_in, conv_w, seq_len):
    n = x.shape[0]
    t = CONV_TILE
    tiles_per_block = t // SUBLANES
    last = n // SUBLANES - 1
    tok = pl.BlockSpec((t, D_MODEL), lambda i: (i, 0))
    prev = pl.BlockSpec((SUBLANES, D_MODEL),
                        lambda i: (jnp.maximum(i * tiles_per_block - 1, 0), 0))
    nxt = pl.BlockSpec((SUBLANES, D_MODEL),
                       lambda i: (jnp.minimum((i + 1) * tiles_per_block, last), 0))
    return pl.pallas_call(
        functools.partial(_conv1_kernel, seq_len=seq_len),
        grid=(n // t,),
        in_specs=[prev, tok, nxt, _mod_spec(row_fn, t),
                  _const_spec((D_MODEL, 3 * WIDTH_C)), _const_spec((3, WIDTH_C))],
        out_specs=pl.BlockSpec((t, WIDTH_C), lambda i: (i, 0)),
        out_shape=jax.ShapeDtypeStruct((n, WIDTH_C), BF16),
        scratch_shapes=[pltpu.VMEM((WIDTH_C // CONV_CHUNK, t + 2 * SUBLANES, CONV_CHUNK), F32)],
        compiler_params=_params("parallel"),
        name="conv1",
    )(x, x, x, mods, w_in, conv_w)


def _post_kernel(*refs, n_pieces):
    pieces = refs[:n_pieces]
    (x_ref, mod_ref, wo_ref, g1_ref, b1_ref, w1_ref, w2_ref, g2_ref, b2_ref, o_ref) = refs[n_pieces:]
    subs = _sub_tiles(x_ref)
    mixed = []
    for rows in subs:
        out = None
        k0 = 0
        for p_ref in pieces:
            k1 = k0 + p_ref.shape[1]
            part = jnp.dot(p_ref[rows, :], wo_ref[k0:k1, :], preferred_element_type=F32)
            out = part if out is None else out + part
            k0 = k1
        mixed.append(_layer_norm(ALPHA * x_ref[rows, :] + _mod(mod_ref, 2) * out,
                                 g1_ref[...], b1_ref[...]))
    hs = [(x * (1.0 + _mod(mod_ref, 4)) + _mod(mod_ref, 3)).astype(BF16) for x in mixed]
    fs = [None] * len(subs)
    for j in range(D_FF // FF_CHUNK):
        lo, hi = j * FF_CHUNK, (j + 1) * FF_CHUNK
        hids = [jnp.dot(h, w1_ref[:, lo:hi], preferred_element_type=F32) for h in hs]
        for s, hid in enumerate(hids):
            hid = jnp.square(jnp.maximum(hid, 0.0)).astype(BF16)
            part = jnp.dot(hid, w2_ref[lo:hi, :], preferred_element_type=F32)
            fs[s] = part if fs[s] is None else fs[s] + part
    for rows, x, f in zip(subs, mixed, fs):
        o_ref[rows, :] = _layer_norm(ALPHA * x + _mod(mod_ref, 5) * f, g2_ref[...], b2_ref[...])


def _post(pieces, x, mods, row_fn, w_out, g1, b1, w1, w2, g2, b2):
    n = x.shape[0]
    t = TOK_TILE
    tok = lambda width: pl.BlockSpec((t, width), lambda i: (i, 0))
    vec = _const_spec((1, D_MODEL))
    return pl.pallas_call(
        functools.partial(_post_kernel, n_pieces=len(pieces)),
        grid=(n // t,),
        in_specs=[tok(p.shape[1]) for p in pieces] + [
            tok(D_MODEL), _mod_spec(row_fn, t), _const_spec(w_out.shape), vec, vec,
            _const_spec((D_MODEL, D_FF)), _const_spec((D_FF, D_MODEL)), vec, vec],
        out_specs=tok(D_MODEL),
        out_shape=jax.ShapeDtypeStruct((n, D_MODEL), F32),
        compiler_params=_params("parallel"),
        name="post",
    )(*pieces, x, mods, w_out, g1, b1, w1, w2, g2, b2)


def _rope_tables(n):
    f32 = np.float32
    rows = n // GRID_W
    row = np.repeat(np.arange(rows, dtype=f32), GRID_W)
    col = np.tile(np.arange(GRID_W, dtype=f32), rows)
    inv = (f32(1.0) / (f32(ROPE_BASE) ** (np.arange(ROPE_PAIRS, dtype=f32) / f32(ROPE_PAIRS)))).astype(f32)
    ang_r, ang_c = (row[:, None] * inv).astype(f32), (col[:, None] * inv).astype(f32)
    cos_r, sin_r, cos_c, sin_c = (fn(a.astype(np.float64)).astype(f32)
                                  for a in (ang_r, ang_c) for fn in (np.cos, np.sin))
    zero = np.zeros_like(ang_r)
    reps = LANES // HEAD_DIM_A
    cos = np.tile(np.concatenate([cos_r, cos_r, cos_c, cos_c], axis=1), (1, reps))
    sa = np.tile(np.concatenate([-sin_r, zero, -sin_c, zero], axis=1), (1, reps))
    sb = np.tile(np.concatenate([zero, sin_r, zero, sin_c], axis=1), (1, reps))
    return jnp.asarray(cos), jnp.asarray(sa), jnp.asarray(sb)


def kernel(x_prompt, x_sample, cache_k0, cache_v0, c, c_ctx, w_mod0, b_mod0, w_in0, lambda_q1_0, lambda_k1_0, lambda_q2_0, lambda_k2_0, subln_g0, sgu_w0, sgu_b0, w_out0, ln_mix_g0, ln_mix_b0, w_ff1_0, w_ff2_0, ln_ff_g0, ln_ff_b0, w_mod1, b_mod1, w_in1, conv_w1, w_out1, ln_mix_g1, ln_mix_b1, w_ff1_1, w_ff2_1, ln_ff_g1, ln_ff_b1):
    batch, seq, d = x_prompt.shape
    dec_batch, dec_seq, _ = x_sample.shape
    assert (seq, dec_seq, d) == (SEQ, DEC_SEQ, D_MODEL) and 1 + dec_batch <= MOD_ROWS
    row = lambda v: v.reshape(1, -1)

    cvec = jnp.concatenate([c_ctx[None, :], c, jnp.zeros((MOD_ROWS - 1 - dec_batch, d), F32)], axis=0)
    mods0, mods1 = _modulations(cvec, w_mod0, row(b_mod0), w_mod1, row(b_mod1))
    mods0 = mods0.reshape(MOD_ROWS, 1, N_MOD)
    mods1 = mods1.reshape(MOD_ROWS, 1, N_MOD)
    ctx_row = lambda tok: 0
    lat_row = lambda tok: 1 + tok // DEC_SEQ

    xp = x_prompt.reshape(batch * seq, d)
    xs = x_sample.reshape(dec_batch * dec_seq, d)

    w_in0_b = w_in0.astype(BF16)
    sgu = (sgu_w0.astype(BF16), sgu_b0.T)
    lam_params = jnp.stack([lambda_q1_0, lambda_k1_0, lambda_q2_0, lambda_k2_0])
    gain = subln_g0.reshape(V_DIM_A, 1)
    qp, kp, vp, sp, new_k0, new_v0 = _in_proj0(xp, mods0, w_in0_b, *sgu, ctx_row, None)
    qs, ks, vs, ss = _in_proj0(xs, mods0, w_in0_b, *sgu, lat_row, _rope_tables(dec_seq))
    ap = _attn_ctx(qp, kp, vp, lam_params, gain)
    cache_vt = jnp.swapaxes(cache_v0.reshape(dec_batch, PAST_LEN, WIDTH_A), 1, 2)
    later = (w_ff1_0, w_ff2_0, w_in1, w_out1, w_ff1_1, w_ff2_1)
    a_s, later_b = _attn_lat(qs, ks, vs, cache_k0.reshape(dec_batch * PAST_LEN, QK_WIDTH),
                             cache_vt.reshape(dec_batch * WIDTH_A, PAST_LEN).astype(BF16),
                             lam_params, gain, later)
    w_ff1_0_b, w_ff2_0_b, w_in1_b, w_out1_b, w_ff1_1_b, w_ff2_1_b = later_b
    post0_w = (w_out0.astype(BF16), row(ln_mix_g0), row(ln_mix_b0),
               w_ff1_0_b, w_ff2_0_b, row(ln_ff_g0), row(ln_ff_b0))
    xp = _post([ap, sp], xp, mods0, ctx_row, *post0_w)
    xs = _post([a_s, ss], xs, mods0, lat_row, *post0_w)

    gp = _conv1(xp, mods1, ctx_row, w_in1_b, conv_w1, seq_len=seq)
    gs = _conv1(xs, mods1, lat_row, w_in1_b, conv_w1, seq_len=dec_seq)
    post1_w = (w_out1_b, row(ln_mix_g1), row(ln_mix_b1),
               w_ff1_1_b, w_ff2_1_b, row(ln_ff_g1), row(ln_ff_b1))
    xp = _post([gp], xp, mods1, ctx_row, *post1_w)
    xs = _post([gs], xs, mods1, lat_row, *post1_w)

    return (xp.reshape(batch, seq, d), xs.reshape(dec_batch, dec_seq, d),
            jnp.transpose(new_k0, (0, 4, 1, 2, 3)),
            new_v0.reshape(batch, seq, N_HEADS_A, V_DIM_A))
```

```python
import functools
import math

import jax
import jax.numpy as jnp
import numpy as np
from jax import lax
from jax.experimental import pallas as pl
from jax.experimental.pallas import tpu as pltpu

F32 = jnp.float32
BF16 = jnp.bfloat16

D_MODEL = 1024
DEPTH = 2
SEQ = 256
DEC_SEQ = 4096
PAST_LEN = 256
GRID_W = 64
N_HEADS_A = 4
HEAD_DIM_A = 64
V_DIM_A = 2 * HEAD_DIM_A
QK_WIDTH = N_HEADS_A * 2 * HEAD_DIM_A
WIDTH_A = N_HEADS_A * V_DIM_A
N_GROUPS_B = 4
CHUNK = 128
GROUP_DIM_B = 128
WIDTH_B = N_GROUPS_B * GROUP_DIM_B
IN_WIDTH_0 = 2 * QK_WIDTH + WIDTH_A + 2 * WIDTH_B
WIDTH_C = D_MODEL
D_FF = 4 * D_MODEL
ROPE_BASE = 10000.0
ROPE_PAIRS = HEAD_DIM_A // 4
LN_EPS = 1e-5
ALPHA = (2 * DEPTH) ** 0.25
LAMBDA_INIT_0 = 0.8 - 0.6 * math.exp(-0.3 * 0)
Q_SCALE = HEAD_DIM_A ** -0.5 * math.log2(math.e)

LANES = 128
SUBLANES = 8
BF16_SUBLANES = 16
VMEM_LIMIT = 56 * 2 ** 20
N_MOD = 6 * D_MODEL
MOD_ROWS = 8
TOK_TILE = 1024
SUB_TILE = 512
CTX_SEQS = 4
CTX_LOOKAHEAD = 3
ATT_Q_BLOCK = 512
ATT_SUB_BLOCK = 128
ATT_KEY_CHUNK = 256
ATT_LOOKAHEAD = 5
FF_CHUNK = 1024
CONV_CHUNK = 256
MOD_TILE = 512
MOD_SIDE_TILE = 256


def _const_spec(shape):
    zeros = (0,) * len(shape)
    return pl.BlockSpec(shape, lambda *_: zeros, pipeline_mode=pl.Buffered(1))


def _params(*sem):
    return pltpu.CompilerParams(dimension_semantics=sem, vmem_limit_bytes=VMEM_LIMIT)


def _layer_norm(x, g, b):
    mu = jnp.mean(x, axis=-1, keepdims=True)
    xc = x - mu
    var = jnp.mean(xc * xc, axis=-1, keepdims=True)
    return xc * lax.rsqrt(var + LN_EPS) * g + b


def _mod(mod_ref, idx):
    return mod_ref[:, idx * D_MODEL:(idx + 1) * D_MODEL]


def _sub_tiles(ref):
    return [slice(i * SUB_TILE, (i + 1) * SUB_TILE) for i in range(ref.shape[0] // SUB_TILE)]


def _mod_slab(c_ref, w_ref, b_ref, o_ref):
    c = c_ref[...]
    s = (c * jax.nn.sigmoid(c)).astype(BF16)
    o_ref[...] = jnp.dot(s, w_ref[...].astype(BF16), preferred_element_type=F32) + b_ref[...]


def _modulations(cvec, w, b):
    w_spec = pl.BlockSpec((D_MODEL, MOD_TILE), lambda j: (0, j))
    v_spec = pl.BlockSpec((1, MOD_TILE), lambda j: (0, j))
    o_spec = pl.BlockSpec((MOD_ROWS, MOD_TILE), lambda j: (0, j))
    return pl.pallas_call(
        _mod_slab,
        grid=(N_MOD // MOD_TILE,),
        in_specs=[_const_spec((MOD_ROWS, D_MODEL)), w_spec, v_spec],
        out_specs=o_spec,
        out_shape=jax.ShapeDtypeStruct((MOD_ROWS, N_MOD), F32),
        compiler_params=_params("parallel"),
        name="adaln_mod",
    )(cvec, w, b)


def _mod_spec(row_fn, tile):
    return pl.BlockSpec((None, 1, N_MOD), lambda i: (row_fn(i * tile), 0, 0))


def _in0_kernel(*refs, rope):
    if rope:
        (x_ref, mod_ref, w_ref, sw_ref, sbias_ref, cos_ref, sa_ref, sb_ref,
         q_ref, k_ref, v_ref, s_ref) = refs
    else:
        (x_ref, mod_ref, w_ref, sw_ref, sbias_ref,
         q_ref, k_ref, v_ref, s_ref, k_out_ref, v_out_ref) = refs
    subs = _sub_tiles(x_ref)
    ys = []
    for rows in subs:
        h = (x_ref[rows, :] * (1.0 + _mod(mod_ref, 1)) + _mod(mod_ref, 0)).astype(BF16)
        ys.append(jnp.dot(h, w_ref[...], preferred_element_type=F32))
    for si, (rows, y) in enumerate(zip(subs, ys)):
        if rope:
            cos, sa, sb = cos_ref[rows, :], sa_ref[rows, :], sb_ref[rows, :]
        for j in range(QK_WIDTH // LANES):
            lo, hi = j * LANES, (j + 1) * LANES
            qj = y[:, lo:hi]
            kj = y[:, QK_WIDTH + lo:QK_WIDTH + hi]
            if rope:
                qj = (qj * cos + pltpu.roll(qj, LANES - ROPE_PAIRS, 1) * sa
                      + pltpu.roll(qj, ROPE_PAIRS, 1) * sb)
                kj = (kj * cos + pltpu.roll(kj, LANES - ROPE_PAIRS, 1) * sa
                      + pltpu.roll(kj, ROPE_PAIRS, 1) * sb)
            q_ref[rows, lo:hi] = (qj * Q_SCALE).astype(q_ref.dtype)
            k_ref[rows, lo:hi] = kj.astype(k_ref.dtype)
            if not rope:
                kt = kj.T
                seqs = SUB_TILE // SEQ
                for b in range(seqs):
                    for i in range(2):
                        k_out_ref[si * seqs + b, j, i] = kt[i * HEAD_DIM_A:(i + 1) * HEAD_DIM_A,
                                                            b * SEQ:(b + 1) * SEQ]
        off = 2 * QK_WIDTH
        v = y[:, off:off + WIDTH_A]
        v_ref[:, rows] = v.T.astype(v_ref.dtype)
        if not rope:
            for j in range(N_HEADS_A):
                v_out_ref[rows, j, :] = v[:, j * V_DIM_A:(j + 1) * V_DIM_A]
        u_off = off + WIDTH_A
        g_off = u_off + WIDTH_B
        for g in range(N_GROUPS_B):
            lo, hi = g * GROUP_DIM_B, (g + 1) * GROUP_DIM_B
            gg = y[:, g_off + lo:g_off + hi]
            mu = jnp.mean(gg, axis=-1, keepdims=True)
            gc = gg - mu
            var = jnp.mean(gc * gc, axis=-1, keepdims=True)
            vc = (gc * lax.rsqrt(var + LN_EPS)).astype(BF16)
            w = sw_ref[g]
            bias = jnp.broadcast_to(sbias_ref[:, g:g + 1], (CHUNK, GROUP_DIM_B))
            for c in range(SUB_TILE // CHUNK):
                r0, r1 = c * CHUNK, (c + 1) * CHUNK
                mixed = jnp.dot(w, vc[r0:r1], preferred_element_type=F32) + bias
                s_ref[rows.start + r0:rows.start + r1, lo:hi] = (
                    y[r0:r1, u_off + lo:u_off + hi] * mixed).astype(s_ref.dtype)


def _in_proj0(x, mods, w_in, sgu_w, sgu_bt, row_fn, rope_tabs):
    n = x.shape[0]
    t = TOK_TILE
    tok = lambda width: pl.BlockSpec((t, width), lambda i: (i, 0))
    in_specs = [tok(D_MODEL), _mod_spec(row_fn, t), _const_spec((D_MODEL, IN_WIDTH_0)),
                _const_spec((N_GROUPS_B, CHUNK, CHUNK)), _const_spec((CHUNK, N_GROUPS_B))]
    args = [x, mods, w_in, sgu_w, sgu_bt]
    v_spec = pl.BlockSpec((WIDTH_A, t), lambda i: (0, i))
    v_shape = (WIDTH_A, n)
    if rope_tabs is not None:
        blocks_per_seq = DEC_SEQ // t
        tab = pl.BlockSpec((t, LANES), lambda i: (i % blocks_per_seq, 0))
        in_specs += [tab, tab, tab]
        args += list(rope_tabs)
        v_spec = pl.BlockSpec((WIDTH_A, t), lambda i: (i // blocks_per_seq, i % blocks_per_seq))
        v_shape = (n // DEC_SEQ * WIDTH_A, DEC_SEQ)
    out_shape = [jax.ShapeDtypeStruct((n, QK_WIDTH), BF16),
                 jax.ShapeDtypeStruct((n, QK_WIDTH), BF16),
                 jax.ShapeDtypeStruct(v_shape, BF16),
                 jax.ShapeDtypeStruct((n, WIDTH_B), BF16)]
    out_specs = [tok(QK_WIDTH), tok(QK_WIDTH), v_spec, tok(WIDTH_B)]
    if rope_tabs is None:
        out_shape += [jax.ShapeDtypeStruct((n // SEQ, N_HEADS_A, 2, HEAD_DIM_A, SEQ), F32),
                      jax.ShapeDtypeStruct((n, N_HEADS_A, V_DIM_A), F32)]
        out_specs += [pl.BlockSpec((t // SEQ, N_HEADS_A, 2, HEAD_DIM_A, SEQ),
                                   lambda i: (i, 0, 0, 0, 0)),
                      pl.BlockSpec((t, N_HEADS_A, V_DIM_A), lambda i: (i, 0, 0))]
    return pl.pallas_call(
        functools.partial(_in0_kernel, rope=rope_tabs is not None),
        grid=(n // t,),
        in_specs=in_specs,
        out_specs=out_specs,
        out_shape=out_shape,
        compiler_params=_params("parallel"),
        name="in_proj0",
    )(*args)


def _lambda(lp_ref):
    lp = lp_ref[...]
    a = jnp.sum(lp[0:1] * lp[1:2], axis=1, keepdims=True)
    b = jnp.sum(lp[2:3] * lp[3:4], axis=1, keepdims=True)
    return jnp.exp(a) - jnp.exp(b) + LAMBDA_INIT_0


def _stack_halves(qh):
    lane = lax.broadcasted_iota(jnp.int32, qh.shape, 1)
    zero = jnp.zeros_like(qh)
    return jnp.concatenate([jnp.where(lane < HEAD_DIM_A, qh, zero),
                            jnp.where(lane >= HEAD_DIM_A, qh, zero)], axis=0)


def _scores(qq, k):
    return lax.dot_general(qq, k, (((1,), (1,)), ((), ())), preferred_element_type=F32)


def _values_with_ones(vals_t):
    return jnp.concatenate([vals_t, jnp.ones((BF16_SUBLANES, vals_t.shape[1]), BF16)], axis=0)


def _finish_head(acc, rows, lam, gain):
    pvt = acc[:V_DIM_A]
    l = acc[V_DIM_A:V_DIM_A + 1]
    ot = pvt[:, :rows] / l[:, :rows] - lam * (pvt[:, rows:] / l[:, rows:])
    ms = jnp.mean(ot * ot, axis=0, keepdims=True)
    return (ot * lax.rsqrt(ms + LN_EPS) * gain * (1.0 - LAMBDA_INIT_0)).T


def _attn_ctx_kernel(q_ref, k_ref, vt_ref, lp_ref, gain_ref, o_ref):
    lam = _lambda(lp_ref)
    gain = gain_ref[...]
    units = [(b, h) for b in range(q_ref.shape[0] // SEQ) for h in range(N_HEADS_A)]
    scores = {}

    def issue_scores(unit):
        b, h = unit
        rows = slice(b * SEQ, (b + 1) * SEQ)
        lo, hi = h * LANES, (h + 1) * LANES
        scores[unit] = _scores(k_ref[rows, lo:hi], _stack_halves(q_ref[rows, lo:hi]))

    for unit in units[:CTX_LOOKAHEAD]:
        issue_scores(unit)
    for i, unit in enumerate(units):
        if i + CTX_LOOKAHEAD < len(units):
            issue_scores(units[i + CTX_LOOKAHEAD])
        b, h = unit
        rows = slice(b * SEQ, (b + 1) * SEQ)
        lo, hi = h * LANES, (h + 1) * LANES
        st = scores.pop(unit)
        pt = jnp.exp2(st - jnp.max(st, axis=0, keepdims=True)).astype(BF16)
        acc = jnp.dot(_values_with_ones(vt_ref[lo:hi, rows]), pt, preferred_element_type=F32)
        o_ref[rows, lo:hi] = _finish_head(acc, SEQ, lam, gain).astype(o_ref.dtype)


def _attn_ctx(q, k, vt, lam_params, gain_col):
    n = q.shape[0]
    t = CTX_SEQS * SEQ
    blk = lambda: pl.BlockSpec((t, QK_WIDTH), lambda b: (b, 0))
    return pl.pallas_call(
        _attn_ctx_kernel,
        grid=(n // t,),
        in_specs=[blk(), blk(), pl.BlockSpec((WIDTH_A, t), lambda b: (0, b)),
                  _const_spec((4, HEAD_DIM_A)), _const_spec((V_DIM_A, 1))],
        out_specs=blk(),
        out_shape=jax.ShapeDtypeStruct((n, WIDTH_A), BF16),
        compiler_params=_params("parallel"),
        name="attn_ctx",
    )(q, k, vt, lam_params, gain_col)


def _attn_lat_kernel(q_ref, k_ref, vt_ref, ck_ref, cvt_ref, lp_ref, gain_ref,
                     c_ref, wm_ref, bm_ref, *rest, n_weights):
    w_refs, (o_ref, mod_ref), wb_refs = rest[:n_weights], rest[n_weights:n_weights + 2], rest[n_weights + 2:]
    for w_ref, wb_ref in zip(w_refs, wb_refs):
        wb_ref[...] = w_ref[...].astype(wb_ref.dtype)
    _mod_slab(c_ref, wm_ref, bm_ref, mod_ref)
    lam = _lambda(lp_ref)
    gain = gain_ref[...]
    rows = ATT_SUB_BLOCK
    chunks = [(None, PAST_LEN)] + [(s, ATT_KEY_CHUNK) for s in range(0, DEC_SEQ, ATT_KEY_CHUNK)]
    units = [(sub, h, c) for sub in range(q_ref.shape[0] // rows)
             for h in range(N_HEADS_A) for c in range(len(chunks))]
    qq_of = {}
    scores = {}

    def issue_scores(unit):
        sub, h, c = unit
        lo, hi = h * LANES, (h + 1) * LANES
        if c == 0:
            qq_of[sub, h] = _stack_halves(q_ref[sub * rows:(sub + 1) * rows, lo:hi])
        start, size = chunks[c]
        keys = ck_ref[:, lo:hi].astype(BF16) if start is None else k_ref[start:start + size, lo:hi]
        scores[unit] = _scores(keys, qq_of[sub, h])

    for unit in units[:ATT_LOOKAHEAD]:
        issue_scores(unit)
    m = acc = None
    for i, unit in enumerate(units):
        if i + ATT_LOOKAHEAD < len(units):
            issue_scores(units[i + ATT_LOOKAHEAD])
        sub, h, c = unit
        lo, hi = h * LANES, (h + 1) * LANES
        start, size = chunks[c]
        vals_t = cvt_ref[lo:hi, :] if start is None else vt_ref[lo:hi, start:start + size]
        st = scores.pop(unit)
        cm = jnp.max(st, axis=0, keepdims=True)
        m_new = cm if c == 0 else jnp.maximum(m, cm)
        pt = jnp.exp2(st - m_new).astype(BF16)
        part = jnp.dot(_values_with_ones(vals_t), pt,
                       preferred_element_type=F32)
        acc = part if c == 0 else acc * jnp.exp2(m - m_new) + part
        m = m_new
        if c == len(chunks) - 1:
            o_ref[sub * rows:(sub + 1) * rows, lo:hi] = _finish_head(
                acc, rows, lam, gain).astype(o_ref.dtype)


def _attn_lat(q, k, vt, cache_k, cache_vt, lam_params, gain_col, cvec, w_mod, b_mod, weights):
    n = q.shape[0]
    nb = n // DEC_SEQ
    qb = DEC_SEQ // ATT_Q_BLOCK
    steps = nb * qb
    q_spec = pl.BlockSpec((ATT_Q_BLOCK, QK_WIDTH), lambda b, j: (b * qb + j, 0))
    k_spec = pl.BlockSpec((DEC_SEQ, QK_WIDTH), lambda b, j: (b, 0))
    vt_spec = pl.BlockSpec((WIDTH_A, DEC_SEQ), lambda b, j: (b, 0))
    ck_spec = pl.BlockSpec((PAST_LEN, QK_WIDTH), lambda b, j: (b, 0))
    cvt_spec = pl.BlockSpec((WIDTH_A, PAST_LEN), lambda b, j: (b, 0))
    w_specs = [pl.BlockSpec((w.shape[0] // steps, w.shape[1]), lambda b, j: (b * qb + j, 0))
               for w in weights]
    assert all(w.shape[0] % (steps * BF16_SUBLANES) == 0 for w in weights)
    last_slab = N_MOD // MOD_SIDE_TILE - 1
    assert last_slab < steps
    slab = lambda rows: pl.BlockSpec((rows, MOD_SIDE_TILE),
                                     lambda b, j: (0, jnp.minimum(b * qb + j, last_slab)))
    outs = pl.pallas_call(
        functools.partial(_attn_lat_kernel, n_weights=len(weights)),
        grid=(nb, qb),
        in_specs=[q_spec, k_spec, vt_spec, ck_spec, cvt_spec,
                  _const_spec((4, HEAD_DIM_A)), _const_spec((V_DIM_A, 1)),
                  _const_spec((MOD_ROWS, D_MODEL)), slab(D_MODEL), slab(1)] + w_specs,
        out_specs=[q_spec, slab(MOD_ROWS)] + w_specs,
        out_shape=[jax.ShapeDtypeStruct((n, WIDTH_A), BF16),
                   jax.ShapeDtypeStruct((MOD_ROWS, N_MOD), F32)]
        + [jax.ShapeDtypeStruct(w.shape, BF16) for w in weights],
        compiler_params=_params("arbitrary", "arbitrary"),
        name="attn_lat",
    )(q, k, vt, cache_k, cache_vt, lam_params, gain_col, cvec, w_mod, b_mod, *weights)
    return outs[0], outs[1], outs[2:]


def _conv1_kernel(xp_ref, x_ref, xn_ref, mod_ref, wi_ref, cw_ref, o_ref, z_scr, *, seq_len):
    t = x_ref.shape[0]
    halo = SUBLANES
    cc = CONV_CHUNK
    n_chunks = WIDTH_C // cc
    xin = jnp.concatenate([xp_ref[...], x_ref[...], xn_ref[...]], axis=0)
    h = (xin * (1.0 + _mod(mod_ref, 1)) + _mod(mod_ref, 0)).astype(BF16)
    pos = (pl.program_id(0) * t + lax.broadcasted_iota(jnp.int32, (t, 1), 0)) % seq_len
    not_first = pos != 0
    not_last = pos != seq_len - 1

    def project(j):
        return [jnp.dot(h, wi_ref[:, sec * WIDTH_C + j * cc:sec * WIDTH_C + (j + 1) * cc],
                        preferred_element_type=F32) for sec in range(3)]

    y_next = project(0)
    for j in range(n_chunks):
        gate, conv_gate, conv_in = y_next
        if j + 1 < n_chunks:
            y_next = project(j + 1)
        z_scr[j] = conv_gate * conv_in
        cw = cw_ref[:, j * cc:(j + 1) * cc]
        z_prev = jnp.where(not_first, z_scr[j, halo - 1:halo - 1 + t, :], 0.0)
        z_next = jnp.where(not_last, z_scr[j, halo + 1:halo + 1 + t, :], 0.0)
        conv = z_prev * cw[0:1] + z_scr[j, halo:halo + t, :] * cw[1:2] + z_next * cw[2:3]
        o_ref[:, j * cc:(j + 1) * cc] = (gate[halo:halo + t] * conv).astype(o_ref.dtype)


def _conv1(x, mods, row_fn, w_in, conv_w, seq_len):
    n = x.shape[0]
    t = TOK_TILE
    tiles_per_block = t // SUBLANES
    last = n // SUBLANES - 1
    tok = pl.BlockSpec((t, D_MODEL), lambda i: (i, 0))
    prev = pl.BlockSpec((SUBLANES, D_MODEL),
                        lambda i: (jnp.maximum(i * tiles_per_block - 1, 0), 0))
    nxt = pl.BlockSpec((SUBLANES, D_MODEL),
                       lambda i: (jnp.minimum((i + 1) * tiles_per_block, last), 0))
    return pl.pallas_call(
        functools.partial(_conv1_kernel, seq_len=seq_len),
        grid=(n // t,),
        in_specs=[prev, tok, nxt, _mod_spec(row_fn, t),
                  _const_spec((D_MODEL, 3 * WIDTH_C)), _const_spec((3, WIDTH_C))],
        out_specs=pl.BlockSpec((t, WIDTH_C), lambda i: (i, 0)),
        out_shape=jax.ShapeDtypeStruct((n, WIDTH_C), BF16),
        scratch_shapes=[pltpu.VMEM((WIDTH_C // CONV_CHUNK, t + 2 * SUBLANES, CONV_CHUNK), F32)],
        compiler_params=_params("parallel"),
        name="conv1",
    )(x, x, x, mods, w_in, conv_w)


def _post_kernel(*refs, n_pieces):
    pieces = refs[:n_pieces]
    (x_ref, mod_ref, wo_ref, g1_ref, b1_ref, w1_ref, w2_ref, g2_ref, b2_ref, o_ref) = refs[n_pieces:]
    subs = _sub_tiles(x_ref)
    mixed = []
    for rows in subs:
        out = None
        k0 = 0
        for p_ref in pieces:
            k1 = k0 + p_ref.shape[1]
            part = jnp.dot(p_ref[rows, :], wo_ref[k0:k1, :], preferred_element_type=F32)
            out = part if out is None else out + part
            k0 = k1
        mixed.append(_layer_norm(ALPHA * x_ref[rows, :] + _mod(mod_ref, 2) * out,
                                 g1_ref[...], b1_ref[...]))
    hs = [(x * (1.0 + _mod(mod_ref, 4)) + _mod(mod_ref, 3)).astype(BF16) for x in mixed]
    fs = [None] * len(subs)
    for j in range(D_FF // FF_CHUNK):
        lo, hi = j * FF_CHUNK, (j + 1) * FF_CHUNK
        hids = [jnp.dot(h, w1_ref[:, lo:hi], preferred_element_type=F32) for h in hs]
        for s, hid in enumerate(hids):
            hid = jnp.square(jnp.maximum(hid, 0.0)).astype(BF16)
            part = jnp.dot(hid, w2_ref[lo:hi, :], preferred_element_type=F32)
            fs[s] = part if fs[s] is None else fs[s] + part
    for rows, x, f in zip(subs, mixed, fs):
        o_ref[rows, :] = _layer_norm(ALPHA * x + _mod(mod_ref, 5) * f, g2_ref[...], b2_ref[...])


def _post(pieces, x, mods, row_fn, w_out, g1, b1, w1, w2, g2, b2):
    n = x.shape[0]
    t = TOK_TILE
    tok = lambda width: pl.BlockSpec((t, width), lambda i: (i, 0))
    vec = _const_spec((1, D_MODEL))
    return pl.pallas_call(
        functools.partial(_post_kernel, n_pieces=len(pieces)),
        grid=(n // t,),
        in_specs=[tok(p.shape[1]) for p in pieces] + [
            tok(D_MODEL), _mod_spec(row_fn, t), _const_spec(w_out.shape), vec, vec,
            _const_spec((D_MODEL, D_FF)), _const_spec((D_FF, D_MODEL)), vec, vec],
        out_specs=tok(D_MODEL),
        out_shape=jax.ShapeDtypeStruct((n, D_MODEL), F32),
        compiler_params=_params("parallel"),
        name="post",
    )(*pieces, x, mods, w_out, g1, b1, w1, w2, g2, b2)


def _rope_tables(n):
    f32 = np.float32
    rows = n // GRID_W
    row = np.repeat(np.arange(rows, dtype=f32), GRID_W)
    col = np.tile(np.arange(GRID_W, dtype=f32), rows)
    inv = (f32(1.0) / (f32(ROPE_BASE) ** (np.arange(ROPE_PAIRS, dtype=f32) / f32(ROPE_PAIRS)))).astype(f32)
    ang_r, ang_c = (row[:, None] * inv).astype(f32), (col[:, None] * inv).astype(f32)
    cos_r, sin_r, cos_c, sin_c = (fn(a.astype(np.float64)).astype(f32)
                                  for a in (ang_r, ang_c) for fn in (np.cos, np.sin))
    zero = np.zeros_like(ang_r)
    reps = LANES // HEAD_DIM_A
    cos = np.tile(np.concatenate([cos_r, cos_r, cos_c, cos_c], axis=1), (1, reps))
    sa = np.tile(np.concatenate([-sin_r, zero, -sin_c, zero], axis=1), (1, reps))
    sb = np.tile(np.concatenate([zero, sin_r, zero, sin_c], axis=1), (1, reps))
    return jnp.asarray(cos), jnp.asarray(sa), jnp.asarray(sb)


def kernel(x_prompt, x_sample, cache_k0, cache_v0, c, c_ctx, w_mod0, b_mod0, w_in0, lambda_q1_0, lambda_k1_0, lambda_q2_0, lambda_k2_0, subln_g0, sgu_w0, sgu_b0, w_out0, ln_mix_g0, ln_mix_b0, w_ff1_0, w_ff2_0, ln_ff_g0, ln_ff_b0, w_mod1, b_mod1, w_in1, conv_w1, w_out1, ln_mix_g1, ln_mix_b1, w_ff1_1, w_ff2_1, ln_ff_g1, ln_ff_b1):
    batch, seq, d = x_prompt.shape
    dec_batch, dec_seq, _ = x_sample.shape
    assert (seq, dec_seq, d) == (SEQ, DEC_SEQ, D_MODEL) and 1 + dec_batch <= MOD_ROWS
    row = lambda v: v.reshape(1, -1)

    cvec = jnp.concatenate([c_ctx[None, :], c, jnp.zeros((MOD_ROWS - 1 - dec_batch, d), F32)], axis=0)
    mods0 = _modulations(cvec, w_mod0, row(b_mod0)).reshape(MOD_ROWS, 1, N_MOD)
    ctx_row = lambda tok: 0
    lat_row = lambda tok: 1 + tok // DEC_SEQ

    xp = x_prompt.reshape(batch * seq, d)
    xs = x_sample.reshape(dec_batch * dec_seq, d)

    w_in0_b = w_in0.astype(BF16)
    sgu = (sgu_w0.astype(BF16), sgu_b0.T)
    lam_params = jnp.stack([lambda_q1_0, lambda_k1_0, lambda_q2_0, lambda_k2_0])
    gain = subln_g0.reshape(V_DIM_A, 1)
    qp, kp, vp, sp, new_k0, new_v0 = _in_proj0(xp, mods0, w_in0_b, *sgu, ctx_row, None)
    qs, ks, vs, ss = _in_proj0(xs, mods0, w_in0_b, *sgu, lat_row, _rope_tables(dec_seq))
    ap = _attn_ctx(qp, kp, vp, lam_params, gain)
    cache_vt = jnp.swapaxes(cache_v0.reshape(dec_batch, PAST_LEN, WIDTH_A), 1, 2)
    later = (w_ff1_0, w_ff2_0, w_in1, w_out1, w_ff1_1, w_ff2_1)
    a_s, mods1, later_b = _attn_lat(
        qs, ks, vs, cache_k0.reshape(dec_batch * PAST_LEN, QK_WIDTH),
        cache_vt.reshape(dec_batch * WIDTH_A, PAST_LEN).astype(BF16),
        lam_params, gain, cvec, w_mod1, row(b_mod1), later)
    mods1 = mods1.reshape(MOD_ROWS, 1, N_MOD)
    w_ff1_0_b, w_ff2_0_b, w_in1_b, w_out1_b, w_ff1_1_b, w_ff2_1_b = later_b
    post0_w = (w_out0.astype(BF16), row(ln_mix_g0), row(ln_mix_b0),
               w_ff1_0_b, w_ff2_0_b, row(ln_ff_g0), row(ln_ff_b0))
    xp = _post([ap, sp], xp, mods0, ctx_row, *post0_w)
    xs = _post([a_s, ss], xs, mods0, lat_row, *post0_w)

    gp = _conv1(xp, mods1, ctx_row, w_in1_b, conv_w1, seq_len=seq)
    gs = _conv1(xs, mods1, lat_row, w_in1_b, conv_w1, seq_len=dec_seq)
    post1_w = (w_out1_b, row(ln_mix_g1), row(ln_mix_b1),
               w_ff1_1_b, w_ff2_1_b, row(ln_ff_g1), row(ln_ff_b1))
    xp = _post([gp], xp, mods1, ctx_row, *post1_w)
    xs = _post([gs], xs, mods1, lat_row, *post1_w)

    return (xp.reshape(batch, seq, d), xs.reshape(dec_batch, dec_seq, d),
            jnp.transpose(new_k0, (0, 4, 1, 2, 3)),
            new_v0.reshape(batch, seq, N_HEADS_A, V_DIM_A))
```

```python
import functools
import math

import jax
import jax.numpy as jnp
import numpy as np
from jax import lax
from jax.experimental import pallas as pl
from jax.experimental.pallas import tpu as pltpu

F32 = jnp.float32
BF16 = jnp.bfloat16

D_MODEL = 1024
DEPTH = 2
SEQ = 256
DEC_SEQ = 4096
PAST_LEN = 256
GRID_W = 64
N_HEADS_A = 4
HEAD_DIM_A = 64
V_DIM_A = 2 * HEAD_DIM_A
QK_WIDTH = N_HEADS_A * 2 * HEAD_DIM_A
WIDTH_A = N_HEADS_A * V_DIM_A
N_GROUPS_B = 4
CHUNK = 128
GROUP_DIM_B = 128
WIDTH_B = N_GROUPS_B * GROUP_DIM_B
IN_WIDTH_0 = 2 * QK_WIDTH + WIDTH_A + 2 * WIDTH_B
WIDTH_C = D_MODEL
D_FF = 4 * D_MODEL
ROPE_BASE = 10000.0
ROPE_PAIRS = HEAD_DIM_A // 4
LN_EPS = 1e-5
ALPHA = (2 * DEPTH) ** 0.25
LAMBDA_INIT_0 = 0.8 - 0.6 * math.exp(-0.3 * 0)
Q_SCALE = HEAD_DIM_A ** -0.5 * math.log2(math.e)

LANES = 128
SUBLANES = 8
BF16_SUBLANES = 16
VMEM_LIMIT = 56 * 2 ** 20
N_MOD = 6 * D_MODEL
MOD_ROWS = 8
TOK_TILE = 1024
SUB_TILE = 512
CTX_SEQS = 4
CTX_LOOKAHEAD = 3
ATT_Q_BLOCK = 512
ATT_SUB_BLOCK = 128
ATT_KEY_CHUNK = 256
ATT_LOOKAHEAD = 5
FF_CHUNK = 1024
CONV_CHUNK = 256
MOD_TILE = 512
MOD_SIDE_TILE = 256


def _const_spec(shape):
    zeros = (0,) * len(shape)
    return pl.BlockSpec(shape, lambda *_: zeros, pipeline_mode=pl.Buffered(1))


def _params(*sem):
    return pltpu.CompilerParams(dimension_semantics=sem, vmem_limit_bytes=VMEM_LIMIT)


def _layer_norm(x, g, b):
    mu = jnp.mean(x, axis=-1, keepdims=True)
    xc = x - mu
    var = jnp.mean(xc * xc, axis=-1, keepdims=True)
    return xc * lax.rsqrt(var + LN_EPS) * g + b


def _mod(mod_ref, idx):
    return mod_ref[:, idx * D_MODEL:(idx + 1) * D_MODEL]


def _sub_tiles(ref):
    return [slice(i * SUB_TILE, (i + 1) * SUB_TILE) for i in range(ref.shape[0] // SUB_TILE)]


def _mod_slab(c_ref, w_ref, b_ref, o_ref):
    c = c_ref[...]
    s = (c * jax.nn.sigmoid(c)).astype(BF16)
    o_ref[...] = jnp.dot(s, w_ref[...].astype(BF16), preferred_element_type=F32) + b_ref[...]


def _modulations(cvec, w, b):
    w_spec = pl.BlockSpec((D_MODEL, MOD_TILE), lambda j: (0, j))
    v_spec = pl.BlockSpec((1, MOD_TILE), lambda j: (0, j))
    o_spec = pl.BlockSpec((MOD_ROWS, MOD_TILE), lambda j: (0, j))
    return pl.pallas_call(
        _mod_slab,
        grid=(N_MOD // MOD_TILE,),
        in_specs=[_const_spec((MOD_ROWS, D_MODEL)), w_spec, v_spec],
        out_specs=o_spec,
        out_shape=jax.ShapeDtypeStruct((MOD_ROWS, N_MOD), F32),
        compiler_params=_params("parallel"),
        name="adaln_mod",
    )(cvec, w, b)


def _mod_spec(row_fn, tile):
    return pl.BlockSpec((None, 1, N_MOD), lambda i: (row_fn(i * tile), 0, 0))


def _in0_kernel(*refs, rope):
    if rope:
        (x_ref, mod_ref, w_ref, sw_ref, sbias_ref, cos_ref, sa_ref, sb_ref,
         q_ref, k_ref, v_ref, s_ref) = refs
    else:
        (x_ref, mod_ref, w_ref, sw_ref, sbias_ref,
         q_ref, k_ref, v_ref, s_ref, k_out_ref, v_out_ref) = refs
    subs = _sub_tiles(x_ref)
    ys = []
    for rows in subs:
        h = (x_ref[rows, :] * (1.0 + _mod(mod_ref, 1)) + _mod(mod_ref, 0)).astype(BF16)
        ys.append(jnp.dot(h, w_ref[...], preferred_element_type=F32))
    for si, (rows, y) in enumerate(zip(subs, ys)):
        if rope:
            cos, sa, sb = cos_ref[rows, :], sa_ref[rows, :], sb_ref[rows, :]
        for j in range(QK_WIDTH // LANES):
            lo, hi = j * LANES, (j + 1) * LANES
            qj = y[:, lo:hi]
            kj = y[:, QK_WIDTH + lo:QK_WIDTH + hi]
            if rope:
                qj = (qj * cos + pltpu.roll(qj, LANES - ROPE_PAIRS, 1) * sa
                      + pltpu.roll(qj, ROPE_PAIRS, 1) * sb)
                kj = (kj * cos + pltpu.roll(kj, LANES - ROPE_PAIRS, 1) * sa
                      + pltpu.roll(kj, ROPE_PAIRS, 1) * sb)
            q_ref[rows, lo:hi] = (qj * Q_SCALE).astype(q_ref.dtype)
            k_ref[rows, lo:hi] = kj.astype(k_ref.dtype)
            if not rope:
                kt = kj.T
                seqs = SUB_TILE // SEQ
                for b in range(seqs):
                    for i in range(2):
                        k_out_ref[si * seqs + b, j, i] = kt[i * HEAD_DIM_A:(i + 1) * HEAD_DIM_A,
                                                            b * SEQ:(b + 1) * SEQ]
        off = 2 * QK_WIDTH
        v = y[:, off:off + WIDTH_A]
        v_ref[:, rows] = v.T.astype(v_ref.dtype)
        if not rope:
            for j in range(N_HEADS_A):
                v_out_ref[rows, j, :] = v[:, j * V_DIM_A:(j + 1) * V_DIM_A]
        u_off = off + WIDTH_A
        g_off = u_off + WIDTH_B
        for g in range(N_GROUPS_B):
            lo, hi = g * GROUP_DIM_B, (g + 1) * GROUP_DIM_B
            gg = y[:, g_off + lo:g_off + hi]
            mu = jnp.mean(gg, axis=-1, keepdims=True)
            gc = gg - mu
            var = jnp.mean(gc * gc, axis=-1, keepdims=True)
            vc = (gc * lax.rsqrt(var + LN_EPS)).astype(BF16)
            w = sw_ref[g]
            bias = jnp.broadcast_to(sbias_ref[:, g:g + 1], (CHUNK, GROUP_DIM_B))
            for c in range(SUB_TILE // CHUNK):
                r0, r1 = c * CHUNK, (c + 1) * CHUNK
                mixed = jnp.dot(w, vc[r0:r1], preferred_element_type=F32) + bias
                s_ref[rows.start + r0:rows.start + r1, lo:hi] = (
                    y[r0:r1, u_off + lo:u_off + hi] * mixed).astype(s_ref.dtype)


def _in_proj0(x, mods, w_in, sgu_w, sgu_bt, row_fn, rope_tabs):
    n = x.shape[0]
    t = TOK_TILE
    tok = lambda width: pl.BlockSpec((t, width), lambda i: (i, 0))
    in_specs = [tok(D_MODEL), _mod_spec(row_fn, t), _const_spec((D_MODEL, IN_WIDTH_0)),
                _const_spec((N_GROUPS_B, CHUNK, CHUNK)), _const_spec((CHUNK, N_GROUPS_B))]
    args = [x, mods, w_in, sgu_w, sgu_bt]
    v_spec = pl.BlockSpec((WIDTH_A, t), lambda i: (0, i))
    v_shape = (WIDTH_A, n)
    if rope_tabs is not None:
        blocks_per_seq = DEC_SEQ // t
        tab = pl.BlockSpec((t, LANES), lambda i: (i % blocks_per_seq, 0))
        in_specs += [tab, tab, tab]
        args += list(rope_tabs)
        v_spec = pl.BlockSpec((WIDTH_A, t), lambda i: (i // blocks_per_seq, i % blocks_per_seq))
        v_shape = (n // DEC_SEQ * WIDTH_A, DEC_SEQ)
    out_shape = [jax.ShapeDtypeStruct((n, QK_WIDTH), BF16),
                 jax.ShapeDtypeStruct((n, QK_WIDTH), BF16),
                 jax.ShapeDtypeStruct(v_shape, BF16),
                 jax.ShapeDtypeStruct((n, WIDTH_B), BF16)]
    out_specs = [tok(QK_WIDTH), tok(QK_WIDTH), v_spec, tok(WIDTH_B)]
    if rope_tabs is None:
        out_shape += [jax.ShapeDtypeStruct((n // SEQ, N_HEADS_A, 2, HEAD_DIM_A, SEQ), F32),
                      jax.ShapeDtypeStruct((n, N_HEADS_A, V_DIM_A), F32)]
        out_specs += [pl.BlockSpec((t // SEQ, N_HEADS_A, 2, HEAD_DIM_A, SEQ),
                                   lambda i: (i, 0, 0, 0, 0)),
                      pl.BlockSpec((t, N_HEADS_A, V_DIM_A), lambda i: (i, 0, 0))]
    return pl.pallas_call(
        functools.partial(_in0_kernel, rope=rope_tabs is not None),
        grid=(n // t,),
        in_specs=in_specs,
        out_specs=out_specs,
        out_shape=out_shape,
        compiler_params=_params("parallel"),
        name="in_proj0",
    )(*args)


def _lambda(lp_ref):
    lp = lp_ref[...]
    a = jnp.sum(lp[0:1] * lp[1:2], axis=1, keepdims=True)
    b = jnp.sum(lp[2:3] * lp[3:4], axis=1, keepdims=True)
    return jnp.exp(a) - jnp.exp(b) + LAMBDA_INIT_0


def _stack_halves(qh):
    lane = lax.broadcasted_iota(jnp.int32, qh.shape, 1)
    zero = jnp.zeros_like(qh)
    return jnp.concatenate([jnp.where(lane < HEAD_DIM_A, qh, zero),
                            jnp.where(lane >= HEAD_DIM_A, qh, zero)], axis=0)


def _scores(qq, k):
    return lax.dot_general(qq, k, (((1,), (1,)), ((), ())), preferred_element_type=F32)


def _values_with_ones(vals_t):
    return jnp.concatenate([vals_t, jnp.ones((BF16_SUBLANES, vals_t.shape[1]), BF16)], axis=0)


def _finish_head(acc, rows, lam, gain):
    pvt = acc[:V_DIM_A]
    l = acc[V_DIM_A:V_DIM_A + 1]
    ot = pvt[:, :rows] / l[:, :rows] - lam * (pvt[:, rows:] / l[:, rows:])
    ms = jnp.mean(ot * ot, axis=0, keepdims=True)
    return (ot * lax.rsqrt(ms + LN_EPS) * gain * (1.0 - LAMBDA_INIT_0)).T


def _attn_ctx_kernel(q_ref, k_ref, vt_ref, lp_ref, gain_ref, o_ref):
    lam = _lambda(lp_ref)
    gain = gain_ref[...]
    units = [(b, h) for b in range(q_ref.shape[0] // SEQ) for h in range(N_HEADS_A)]
    scores = {}

    def issue_scores(unit):
        b, h = unit
        rows = slice(b * SEQ, (b + 1) * SEQ)
        lo, hi = h * LANES, (h + 1) * LANES
        scores[unit] = _scores(k_ref[rows, lo:hi], _stack_halves(q_ref[rows, lo:hi]))

    for unit in units[:CTX_LOOKAHEAD]:
        issue_scores(unit)
    for i, unit in enumerate(units):
        if i + CTX_LOOKAHEAD < len(units):
            issue_scores(units[i + CTX_LOOKAHEAD])
        b, h = unit
        rows = slice(b * SEQ, (b + 1) * SEQ)
        lo, hi = h * LANES, (h + 1) * LANES
        st = scores.pop(unit)
        pt = jnp.exp2(st - jnp.max(st, axis=0, keepdims=True)).astype(BF16)
        acc = jnp.dot(_values_with_ones(vt_ref[lo:hi, rows]), pt, preferred_element_type=F32)
        o_ref[rows, lo:hi] = _finish_head(acc, SEQ, lam, gain).astype(o_ref.dtype)


def _attn_ctx(q, k, vt, lam_params, gain_col):
    n = q.shape[0]
    t = CTX_SEQS * SEQ
    blk = lambda: pl.BlockSpec((t, QK_WIDTH), lambda b: (b, 0))
    return pl.pallas_call(
        _attn_ctx_kernel,
        grid=(n // t,),
        in_specs=[blk(), blk(), pl.BlockSpec((WIDTH_A, t), lambda b: (0, b)),
                  _const_spec((4, HEAD_DIM_A)), _const_spec((V_DIM_A, 1))],
        out_specs=blk(),
        out_shape=jax.ShapeDtypeStruct((n, WIDTH_A), BF16),
        compiler_params=_params("parallel"),
        name="attn_ctx",
    )(q, k, vt, lam_params, gain_col)


def _attn_lat_kernel(q_ref, k_ref, vt_ref, ckt_ref, cv_ref, lp_ref, gain_ref,
                     c_ref, wm_ref, bm_ref, *rest, n_weights):
    w_refs, (o_ref, mod_ref) = rest[:n_weights], rest[n_weights:n_weights + 2]
    wb_refs, (ck_scr, cvt_scr) = rest[n_weights + 2:-2], rest[-2:]

    @pl.when(pl.program_id(1) == 0)
    def _():
        for h in range(N_HEADS_A):
            lo, hi = h * LANES, (h + 1) * LANES
            ck_scr[:, lo:hi] = jnp.concatenate(
                [ckt_ref[0, h, 0].T, ckt_ref[0, h, 1].T], axis=1).astype(BF16)
            cvt_scr[lo:hi, :] = cv_ref[0, :, h, :].T.astype(BF16)

    for w_ref, wb_ref in zip(w_refs, wb_refs):
        wb_ref[...] = w_ref[...].astype(wb_ref.dtype)
    _mod_slab(c_ref, wm_ref, bm_ref, mod_ref)
    lam = _lambda(lp_ref)
    gain = gain_ref[...]
    rows = ATT_SUB_BLOCK
    chunks = [(None, PAST_LEN)] + [(s, ATT_KEY_CHUNK) for s in range(0, DEC_SEQ, ATT_KEY_CHUNK)]
    units = [(sub, h, c) for sub in range(q_ref.shape[0] // rows)
             for h in range(N_HEADS_A) for c in range(len(chunks))]
    qq_of = {}
    scores = {}

    def issue_scores(unit):
        sub, h, c = unit
        lo, hi = h * LANES, (h + 1) * LANES
        if c == 0:
            qq_of[sub, h] = _stack_halves(q_ref[sub * rows:(sub + 1) * rows, lo:hi])
        start, size = chunks[c]
        keys = ck_scr[:, lo:hi] if start is None else k_ref[start:start + size, lo:hi]
        scores[unit] = _scores(keys, qq_of[sub, h])

    for unit in units[:ATT_LOOKAHEAD]:
        issue_scores(unit)
    m = acc = None
    for i, unit in enumerate(units):
        if i + ATT_LOOKAHEAD < len(units):
            issue_scores(units[i + ATT_LOOKAHEAD])
        sub, h, c = unit
        lo, hi = h * LANES, (h + 1) * LANES
        start, size = chunks[c]
        vals_t = cvt_scr[lo:hi, :] if start is None else vt_ref[lo:hi, start:start + size]
        st = scores.pop(unit)
        cm = jnp.max(st, axis=0, keepdims=True)
        m_new = cm if c == 0 else jnp.maximum(m, cm)
        pt = jnp.exp2(st - m_new).astype(BF16)
        part = jnp.dot(_values_with_ones(vals_t), pt,
                       preferred_element_type=F32)
        acc = part if c == 0 else acc * jnp.exp2(m - m_new) + part
        m = m_new
        if c == len(chunks) - 1:
            o_ref[sub * rows:(sub + 1) * rows, lo:hi] = _finish_head(
                acc, rows, lam, gain).astype(o_ref.dtype)


def _attn_lat(q, k, vt, cache_kt, cache_v, lam_params, gain_col, cvec, w_mod, b_mod, weights):
    n = q.shape[0]
    nb = n // DEC_SEQ
    qb = DEC_SEQ // ATT_Q_BLOCK
    steps = nb * qb
    q_spec = pl.BlockSpec((ATT_Q_BLOCK, QK_WIDTH), lambda b, j: (b * qb + j, 0))
    k_spec = pl.BlockSpec((DEC_SEQ, QK_WIDTH), lambda b, j: (b, 0))
    vt_spec = pl.BlockSpec((WIDTH_A, DEC_SEQ), lambda b, j: (b, 0))
    ck_spec = pl.BlockSpec((1, N_HEADS_A, 2, HEAD_DIM_A, PAST_LEN), lambda b, j: (b, 0, 0, 0, 0))
    cvt_spec = pl.BlockSpec((1, PAST_LEN, N_HEADS_A, V_DIM_A), lambda b, j: (b, 0, 0, 0))
    w_specs = [pl.BlockSpec((w.shape[0] // steps, w.shape[1]), lambda b, j: (b * qb + j, 0))
               for w in weights]
    assert all(w.shape[0] % (steps * BF16_SUBLANES) == 0 for w in weights)
    last_slab = N_MOD // MOD_SIDE_TILE - 1
    assert last_slab < steps
    slab = lambda rows: pl.BlockSpec((rows, MOD_SIDE_TILE),
                                     lambda b, j: (0, jnp.minimum(b * qb + j, last_slab)))
    outs = pl.pallas_call(
        functools.partial(_attn_lat_kernel, n_weights=len(weights)),
        grid=(nb, qb),
        in_specs=[q_spec, k_spec, vt_spec, ck_spec, cvt_spec,
                  _const_spec((4, HEAD_DIM_A)), _const_spec((V_DIM_A, 1)),
                  _const_spec((MOD_ROWS, D_MODEL)), slab(D_MODEL), slab(1)] + w_specs,
        out_specs=[q_spec, slab(MOD_ROWS)] + w_specs,
        out_shape=[jax.ShapeDtypeStruct((n, WIDTH_A), BF16),
                   jax.ShapeDtypeStruct((MOD_ROWS, N_MOD), F32)]
        + [jax.ShapeDtypeStruct(w.shape, BF16) for w in weights],
        scratch_shapes=[pltpu.VMEM((PAST_LEN, QK_WIDTH), BF16),
                        pltpu.VMEM((WIDTH_A, PAST_LEN), BF16)],
        compiler_params=_params("arbitrary", "arbitrary"),
        name="attn_lat",
    )(q, k, vt, cache_kt, cache_v, lam_params, gain_col, cvec, w_mod, b_mod, *weights)
    return outs[0], outs[1], outs[2:]


def _conv1_kernel(xp_ref, x_ref, xn_ref, mod_ref, wi_ref, cw_ref, o_ref, z_scr, *, seq_len):
    t = x_ref.shape[0]
    halo = SUBLANES
    cc = CONV_CHUNK
    n_chunks = WIDTH_C // cc
    xin = jnp.concatenate([xp_ref[...], x_ref[...], xn_ref[...]], axis=0)
    h = (xin * (1.0 + _mod(mod_ref, 1)) + _mod(mod_ref, 0)).astype(BF16)
    pos = (pl.program_id(0) * t + lax.broadcasted_iota(jnp.int32, (t, 1), 0)) % seq_len
    not_first = pos != 0
    not_last = pos != seq_len - 1

    def project(j):
        return [jnp.dot(h, wi_ref[:, sec * WIDTH_C + j * cc:sec * WIDTH_C + (j + 1) * cc],
                        preferred_element_type=F32) for sec in range(3)]

    y_next = project(0)
    for j in range(n_chunks):
        gate, conv_gate, conv_in = y_next
        if j + 1 < n_chunks:
            y_next = project(j + 1)
        z_scr[j] = conv_gate * conv_in
        cw = cw_ref[:, j * cc:(j + 1) * cc]
        z_prev = jnp.where(not_first, z_scr[j, halo - 1:halo - 1 + t, :], 0.0)
        z_next = jnp.where(not_last, z_scr[j, halo + 1:halo + 1 + t, :], 0.0)
        conv = z_prev * cw[0:1] + z_scr[j, halo:halo + t, :] * cw[1:2] + z_next * cw[2:3]
        o_ref[:, j * cc:(j + 1) * cc] = (gate[halo:halo + t] * conv).astype(o_ref.dtype)


def _conv1(x, mods, row_fn, w_in, conv_w, seq_len):
    n = x.shape[0]
    t = TOK_TILE
    tiles_per_block = t // SUBLANES
    last = n // SUBLANES - 1
    tok = pl.BlockSpec((t, D_MODEL), lambda i: (i, 0))
    prev = pl.BlockSpec((SUBLANES, D_MODEL),
                        lambda i: (jnp.maximum(i * tiles_per_block - 1, 0), 0))
    nxt = pl.BlockSpec((SUBLANES, D_MODEL),
                       lambda i: (jnp.minimum((i + 1) * tiles_per_block, last), 0))
    return pl.pallas_call(
        functools.partial(_conv1_kernel, seq_len=seq_len),
        grid=(n // t,),
        in_specs=[prev, tok, nxt, _mod_spec(row_fn, t),
                  _const_spec((D_MODEL, 3 * WIDTH_C)), _const_spec((3, WIDTH_C))],
        out_specs=pl.BlockSpec((t, WIDTH_C), lambda i: (i, 0)),
        out_shape=jax.ShapeDtypeStruct((n, WIDTH_C), BF16),
        scratch_shapes=[pltpu.VMEM((WIDTH_C // CONV_CHUNK, t + 2 * SUBLANES, CONV_CHUNK), F32)],
        compiler_params=_params("parallel"),
        name="conv1",
    )(x, x, x, mods, w_in, conv_w)


def _post_kernel(*refs, n_pieces):
    pieces = refs[:n_pieces]
    (x_ref, mod_ref, wo_ref, g1_ref, b1_ref, w1_ref, w2_ref, g2_ref, b2_ref, o_ref) = refs[n_pieces:]
    subs = _sub_tiles(x_ref)
    mixed = []
    for rows in subs:
        out = None
        k0 = 0
        for p_ref in pieces:
            k1 = k0 + p_ref.shape[1]
            part = jnp.dot(p_ref[rows, :], wo_ref[k0:k1, :], preferred_element_type=F32)
            out = part if out is None else out + part
            k0 = k1
        mixed.append(_layer_norm(ALPHA * x_ref[rows, :] + _mod(mod_ref, 2) * out,
                                 g1_ref[...], b1_ref[...]))
    hs = [(x * (1.0 + _mod(mod_ref, 4)) + _mod(mod_ref, 3)).astype(BF16) for x in mixed]
    fs = [None] * len(subs)
    for j in range(D_FF // FF_CHUNK):
        lo, hi = j * FF_CHUNK, (j + 1) * FF_CHUNK
        hids = [jnp.dot(h, w1_ref[:, lo:hi], preferred_element_type=F32) for h in hs]
        for s, hid in enumerate(hids):
            hid = jnp.square(jnp.maximum(hid, 0.0)).astype(BF16)
            part = jnp.dot(hid, w2_ref[lo:hi, :], preferred_element_type=F32)
            fs[s] = part if fs[s] is None else fs[s] + part
    for rows, x, f in zip(subs, mixed, fs):
        o_ref[rows, :] = _layer_norm(ALPHA * x + _mod(mod_ref, 5) * f, g2_ref[...], b2_ref[...])


def _post(pieces, x, mods, row_fn, w_out, g1, b1, w1, w2, g2, b2):
    n = x.shape[0]
    t = TOK_TILE
    tok = lambda width: pl.BlockSpec((t, width), lambda i: (i, 0))
    vec = _const_spec((1, D_MODEL))
    return pl.pallas_call(
        functools.partial(_post_kernel, n_pieces=len(pieces)),
        grid=(n // t,),
        in_specs=[tok(p.shape[1]) for p in pieces] + [
            tok(D_MODEL), _mod_spec(row_fn, t), _const_spec(w_out.shape), vec, vec,
            _const_spec((D_MODEL, D_FF)), _const_spec((D_FF, D_MODEL)), vec, vec],
        out_specs=tok(D_MODEL),
        out_shape=jax.ShapeDtypeStruct((n, D_MODEL), F32),
        compiler_params=_params("parallel"),
        name="post",
    )(*pieces, x, mods, w_out, g1, b1, w1, w2, g2, b2)


def _rope_tables(n):
    f32 = np.float32
    rows = n // GRID_W
    row = np.repeat(np.arange(rows, dtype=f32), GRID_W)
    col = np.tile(np.arange(GRID_W, dtype=f32), rows)
    inv = (f32(1.0) / (f32(ROPE_BASE) ** (np.arange(ROPE_PAIRS, dtype=f32) / f32(ROPE_PAIRS)))).astype(f32)
    ang_r, ang_c = (row[:, None] * inv).astype(f32), (col[:, None] * inv).astype(f32)
    cos_r, sin_r, cos_c, sin_c = (fn(a.astype(np.float64)).astype(f32)
                                  for a in (ang_r, ang_c) for fn in (np.cos, np.sin))
    zero = np.zeros_like(ang_r)
    reps = LANES // HEAD_DIM_A
    cos = np.tile(np.concatenate([cos_r, cos_r, cos_c, cos_c], axis=1), (1, reps))
    sa = np.tile(np.concatenate([-sin_r, zero, -sin_c, zero], axis=1), (1, reps))
    sb = np.tile(np.concatenate([zero, sin_r, zero, sin_c], axis=1), (1, reps))
    return jnp.asarray(cos), jnp.asarray(sa), jnp.asarray(sb)


def kernel(x_prompt, x_sample, cache_k0, cache_v0, c, c_ctx, w_mod0, b_mod0, w_in0, lambda_q1_0, lambda_k1_0, lambda_q2_0, lambda_k2_0, subln_g0, sgu_w0, sgu_b0, w_out0, ln_mix_g0, ln_mix_b0, w_ff1_0, w_ff2_0, ln_ff_g0, ln_ff_b0, w_mod1, b_mod1, w_in1, conv_w1, w_out1, ln_mix_g1, ln_mix_b1, w_ff1_1, w_ff2_1, ln_ff_g1, ln_ff_b1):
    batch, seq, d = x_prompt.shape
    dec_batch, dec_seq, _ = x_sample.shape
    assert (seq, dec_seq, d) == (SEQ, DEC_SEQ, D_MODEL) and 1 + dec_batch <= MOD_ROWS
    row = lambda v: v.reshape(1, -1)

    cvec = jnp.concatenate([c_ctx[None, :], c, jnp.zeros((MOD_ROWS - 1 - dec_batch, d), F32)], axis=0)
    mods0 = _modulations(cvec, w_mod0, row(b_mod0)).reshape(MOD_ROWS, 1, N_MOD)
    ctx_row = lambda tok: 0
    lat_row = lambda tok: 1 + tok // DEC_SEQ

    xp = x_prompt.reshape(batch * seq, d)
    xs = x_sample.reshape(dec_batch * dec_seq, d)

    w_in0_b = w_in0.astype(BF16)
    sgu = (sgu_w0.astype(BF16), sgu_b0.T)
    lam_params = jnp.stack([lambda_q1_0, lambda_k1_0, lambda_q2_0, lambda_k2_0])
    gain = subln_g0.reshape(V_DIM_A, 1)
    qp, kp, vp, sp, new_k0, new_v0 = _in_proj0(xp, mods0, w_in0_b, *sgu, ctx_row, None)
    qs, ks, vs, ss = _in_proj0(xs, mods0, w_in0_b, *sgu, lat_row, _rope_tables(dec_seq))
    ap = _attn_ctx(qp, kp, vp, lam_params, gain)
    later = (w_ff1_0, w_ff2_0, w_in1, w_out1, w_ff1_1, w_ff2_1)
    cache_kt = jnp.transpose(cache_k0, (0, 2, 3, 4, 1))
    a_s, mods1, later_b = _attn_lat(qs, ks, vs, cache_kt, cache_v0, lam_params, gain,
                                    cvec, w_mod1, row(b_mod1), later)
    mods1 = mods1.reshape(MOD_ROWS, 1, N_MOD)
    w_ff1_0_b, w_ff2_0_b, w_in1_b, w_out1_b, w_ff1_1_b, w_ff2_1_b = later_b
    post0_w = (w_out0.astype(BF16), row(ln_mix_g0), row(ln_mix_b0),
               w_ff1_0_b, w_ff2_0_b, row(ln_ff_g0), row(ln_ff_b0))
    xp = _post([ap, sp], xp, mods0, ctx_row, *post0_w)
    xs = _post([a_s, ss], xs, mods0, lat_row, *post0_w)

    gp = _conv1(xp, mods1, ctx_row, w_in1_b, conv_w1, seq_len=seq)
    gs = _conv1(xs, mods1, lat_row, w_in1_b, conv_w1, seq_len=dec_seq)
    post1_w = (w_out1_b, row(ln_mix_g1), row(ln_mix_b1),
               w_ff1_1_b, w_ff2_1_b, row(ln_ff_g1), row(ln_ff_b1))
    xp = _post([gp], xp, mods1, ctx_row, *post1_w)
    xs = _post([gs], xs, mods1, lat_row, *post1_w)

    return (xp.reshape(batch, seq, d), xs.reshape(dec_batch, dec_seq, d),
            jnp.transpose(new_k0, (0, 4, 1, 2, 3)),
            new_v0.reshape(batch, seq, N_HEADS_A, V_DIM_A))
```

```python
import functools
import math

import jax
import jax.numpy as jnp
import numpy as np
from jax import lax
from jax.experimental import pallas as pl
from jax.experimental.pallas import tpu as pltpu

F32 = jnp.float32
BF16 = jnp.bfloat16

D_MODEL = 1024
DEPTH = 2
SEQ = 256
DEC_SEQ = 4096
PAST_LEN = 256
GRID_W = 64
N_HEADS_A = 4
HEAD_DIM_A = 64
V_DIM_A = 2 * HEAD_DIM_A
QK_WIDTH = N_HEADS_A * 2 * HEAD_DIM_A
WIDTH_A = N_HEADS_A * V_DIM_A
N_GROUPS_B = 4
CHUNK = 128
GROUP_DIM_B = 128
WIDTH_B = N_GROUPS_B * GROUP_DIM_B
IN_WIDTH_0 = 2 * QK_WIDTH + WIDTH_A + 2 * WIDTH_B
WIDTH_C = D_MODEL
D_FF = 4 * D_MODEL
ROPE_BASE = 10000.0
ROPE_PAIRS = HEAD_DIM_A // 4
LN_EPS = 1e-5
ALPHA = (2 * DEPTH) ** 0.25
LAMBDA_INIT_0 = 0.8 - 0.6 * math.exp(-0.3 * 0)
Q_SCALE = HEAD_DIM_A ** -0.5 * math.log2(math.e)

LANES = 128
SUBLANES = 8
BF16_SUBLANES = 16
VMEM_LIMIT = 56 * 2 ** 20
N_MOD = 6 * D_MODEL
MOD_ROWS = 8
TOK_TILE = 1024
SUB_TILE = 512
CTX_LOOKAHEAD = 3
ATT_Q_BLOCK = 512
ATT_SUB_BLOCK = 128
ATT_KEY_CHUNK = 256
ATT_LOOKAHEAD = 5
FF_CHUNK = 1024
CONV_CHUNK = 256
MOD_TILE = 512
MOD_SIDE_TILE = 256


def _const_spec(shape):
    zeros = (0,) * len(shape)
    return pl.BlockSpec(shape, lambda *_: zeros, pipeline_mode=pl.Buffered(1))


def _params(*sem):
    return pltpu.CompilerParams(dimension_semantics=sem, vmem_limit_bytes=VMEM_LIMIT)


def _layer_norm(x, g, b):
    mu = jnp.mean(x, axis=-1, keepdims=True)
    xc = x - mu
    var = jnp.mean(xc * xc, axis=-1, keepdims=True)
    return xc * lax.rsqrt(var + LN_EPS) * g + b


def _mod(mod_ref, idx):
    return mod_ref[:, idx * D_MODEL:(idx + 1) * D_MODEL]


def _sub_tiles(ref):
    return [slice(i * SUB_TILE, (i + 1) * SUB_TILE) for i in range(ref.shape[0] // SUB_TILE)]


def _mod_slab(c_ref, w_ref, b_ref, o_ref):
    c = c_ref[...]
    s = (c * jax.nn.sigmoid(c)).astype(BF16)
    o_ref[...] = jnp.dot(s, w_ref[...].astype(BF16), preferred_element_type=F32) + b_ref[...]


def _modulations(cvec, w, b):
    w_spec = pl.BlockSpec((D_MODEL, MOD_TILE), lambda j: (0, j))
    v_spec = pl.BlockSpec((1, MOD_TILE), lambda j: (0, j))
    o_spec = pl.BlockSpec((MOD_ROWS, MOD_TILE), lambda j: (0, j))
    return pl.pallas_call(
        _mod_slab,
        grid=(N_MOD // MOD_TILE,),
        in_specs=[_const_spec((MOD_ROWS, D_MODEL)), w_spec, v_spec],
        out_specs=o_spec,
        out_shape=jax.ShapeDtypeStruct((MOD_ROWS, N_MOD), F32),
        compiler_params=_params("parallel"),
        name="adaln_mod",
    )(cvec, w, b)


def _mod_spec(row_fn, tile):
    return pl.BlockSpec((None, 1, N_MOD), lambda i: (row_fn(i * tile), 0, 0))


def _in0_kernel(*refs, rope):
    if rope:
        (x_ref, mod_ref, w_ref, sw_ref, sbias_ref, cos_ref, sa_ref, sb_ref,
         q_ref, k_ref, v_ref, s_ref) = refs
    else:
        (x_ref, mod_ref, w_ref, sw_ref, sbias_ref, lp_ref, gain_ref,
         a_ref, s_ref, k_out_ref, v_out_ref) = refs
        lam = _lambda(lp_ref)
        gain = gain_ref[...]
    subs = _sub_tiles(x_ref)
    ys = []
    for rows in subs:
        h = (x_ref[rows, :] * (1.0 + _mod(mod_ref, 1)) + _mod(mod_ref, 0)).astype(BF16)
        ys.append(jnp.dot(h, w_ref[...], preferred_element_type=F32))
    for si, (rows, y) in enumerate(zip(subs, ys)):
        if rope:
            cos, sa, sb = cos_ref[rows, :], sa_ref[rows, :], sb_ref[rows, :]
        q_heads, k_heads = [], []
        for j in range(QK_WIDTH // LANES):
            lo, hi = j * LANES, (j + 1) * LANES
            qj = y[:, lo:hi]
            kj = y[:, QK_WIDTH + lo:QK_WIDTH + hi]
            if rope:
                qj = (qj * cos + pltpu.roll(qj, LANES - ROPE_PAIRS, 1) * sa
                      + pltpu.roll(qj, ROPE_PAIRS, 1) * sb)
                kj = (kj * cos + pltpu.roll(kj, LANES - ROPE_PAIRS, 1) * sa
                      + pltpu.roll(kj, ROPE_PAIRS, 1) * sb)
                q_ref[rows, lo:hi] = (qj * Q_SCALE).astype(q_ref.dtype)
                k_ref[rows, lo:hi] = kj.astype(k_ref.dtype)
            else:
                q_heads.append((qj * Q_SCALE).astype(BF16))
                k_heads.append(kj.astype(BF16))
                kt = kj.T
                seqs = SUB_TILE // SEQ
                for b in range(seqs):
                    for i in range(2):
                        k_out_ref[si * seqs + b, j, i] = kt[i * HEAD_DIM_A:(i + 1) * HEAD_DIM_A,
                                                            b * SEQ:(b + 1) * SEQ]
        off = 2 * QK_WIDTH
        v = y[:, off:off + WIDTH_A]
        vt = v.T.astype(BF16)
        if rope:
            v_ref[:, rows] = vt
        else:
            for j in range(N_HEADS_A):
                v_out_ref[rows, j, :] = v[:, j * V_DIM_A:(j + 1) * V_DIM_A]
            units = [(b, hd) for b in range(SUB_TILE // SEQ) for hd in range(N_HEADS_A)]
            scores = {}

            def issue_scores(unit):
                b, hd = unit
                seq = slice(b * SEQ, (b + 1) * SEQ)
                scores[unit] = _scores(k_heads[hd][seq], _stack_halves(q_heads[hd][seq]))

            for unit in units[:CTX_LOOKAHEAD]:
                issue_scores(unit)
            for i, unit in enumerate(units):
                if i + CTX_LOOKAHEAD < len(units):
                    issue_scores(units[i + CTX_LOOKAHEAD])
                b, hd = unit
                seq = slice(b * SEQ, (b + 1) * SEQ)
                st = scores.pop(unit)
                pt = jnp.exp2(st - jnp.max(st, axis=0, keepdims=True)).astype(BF16)
                acc = jnp.dot(_values_with_ones(vt[hd * V_DIM_A:(hd + 1) * V_DIM_A, seq]), pt,
                              preferred_element_type=F32)
                a_ref[rows.start + b * SEQ:rows.start + (b + 1) * SEQ,
                      hd * V_DIM_A:(hd + 1) * V_DIM_A] = _finish_head(
                          acc, SEQ, lam, gain).astype(a_ref.dtype)
        u_off = off + WIDTH_A
        g_off = u_off + WIDTH_B
        for g in range(N_GROUPS_B):
            lo, hi = g * GROUP_DIM_B, (g + 1) * GROUP_DIM_B
            gg = y[:, g_off + lo:g_off + hi]
            mu = jnp.mean(gg, axis=-1, keepdims=True)
            gc = gg - mu
            var = jnp.mean(gc * gc, axis=-1, keepdims=True)
            vc = (gc * lax.rsqrt(var + LN_EPS)).astype(BF16)
            w = sw_ref[g]
            bias = jnp.broadcast_to(sbias_ref[:, g:g + 1], (CHUNK, GROUP_DIM_B))
            for c in range(SUB_TILE // CHUNK):
                r0, r1 = c * CHUNK, (c + 1) * CHUNK
                mixed = jnp.dot(w, vc[r0:r1], preferred_element_type=F32) + bias
                s_ref[rows.start + r0:rows.start + r1, lo:hi] = (
                    y[r0:r1, u_off + lo:u_off + hi] * mixed).astype(s_ref.dtype)


def _in_proj0(x, mods, w_in, sgu_w, sgu_bt, row_fn, rope_tabs=None, attn_params=None):
    n = x.shape[0]
    t = TOK_TILE
    tok = lambda width: pl.BlockSpec((t, width), lambda i: (i, 0))
    in_specs = [tok(D_MODEL), _mod_spec(row_fn, t), _const_spec((D_MODEL, IN_WIDTH_0)),
                _const_spec((N_GROUPS_B, CHUNK, CHUNK)), _const_spec((CHUNK, N_GROUPS_B))]
    args = [x, mods, w_in, sgu_w, sgu_bt]
    if rope_tabs is not None:
        blocks_per_seq = DEC_SEQ // t
        tab = pl.BlockSpec((t, LANES), lambda i: (i % blocks_per_seq, 0))
        in_specs += [tab, tab, tab]
        args += list(rope_tabs)
        v_spec = pl.BlockSpec((WIDTH_A, t), lambda i: (i // blocks_per_seq, i % blocks_per_seq))
        out_shape = [jax.ShapeDtypeStruct((n, QK_WIDTH), BF16),
                     jax.ShapeDtypeStruct((n, QK_WIDTH), BF16),
                     jax.ShapeDtypeStruct((n // DEC_SEQ * WIDTH_A, DEC_SEQ), BF16),
                     jax.ShapeDtypeStruct((n, WIDTH_B), BF16)]
        out_specs = [tok(QK_WIDTH), tok(QK_WIDTH), v_spec, tok(WIDTH_B)]
    else:
        in_specs += [_const_spec((4, HEAD_DIM_A)), _const_spec((V_DIM_A, 1))]
        args += list(attn_params)
        out_shape = [jax.ShapeDtypeStruct((n, WIDTH_A), BF16),
                     jax.ShapeDtypeStruct((n, WIDTH_B), BF16),
                     jax.ShapeDtypeStruct((n // SEQ, N_HEADS_A, 2, HEAD_DIM_A, SEQ), F32),
                     jax.ShapeDtypeStruct((n, N_HEADS_A, V_DIM_A), F32)]
        out_specs = [tok(WIDTH_A), tok(WIDTH_B),
                     pl.BlockSpec((t // SEQ, N_HEADS_A, 2, HEAD_DIM_A, SEQ),
                                  lambda i: (i, 0, 0, 0, 0)),
                     pl.BlockSpec((t, N_HEADS_A, V_DIM_A), lambda i: (i, 0, 0))]
    return pl.pallas_call(
        functools.partial(_in0_kernel, rope=rope_tabs is not None),
        grid=(n // t,),
        in_specs=in_specs,
        out_specs=out_specs,
        out_shape=out_shape,
        compiler_params=_params("parallel"),
        name="in_proj0",
    )(*args)


def _lambda(lp_ref):
    lp = lp_ref[...]
    a = jnp.sum(lp[0:1] * lp[1:2], axis=1, keepdims=True)
    b = jnp.sum(lp[2:3] * lp[3:4], axis=1, keepdims=True)
    return jnp.exp(a) - jnp.exp(b) + LAMBDA_INIT_0


def _stack_halves(qh):
    lane = lax.broadcasted_iota(jnp.int32, qh.shape, 1)
    zero = jnp.zeros_like(qh)
    return jnp.concatenate([jnp.where(lane < HEAD_DIM_A, qh, zero),
                            jnp.where(lane >= HEAD_DIM_A, qh, zero)], axis=0)


def _scores(qq, k):
    return lax.dot_general(qq, k, (((1,), (1,)), ((), ())), preferred_element_type=F32)


def _values_with_ones(vals_t):
    return jnp.concatenate([vals_t, jnp.ones((BF16_SUBLANES, vals_t.shape[1]), BF16)], axis=0)


def _finish_head(acc, rows, lam, gain):
    pvt = acc[:V_DIM_A]
    l = acc[V_DIM_A:V_DIM_A + 1]
    ot = pvt[:, :rows] / l[:, :rows] - lam * (pvt[:, rows:] / l[:, rows:])
    ms = jnp.mean(ot * ot, axis=0, keepdims=True)
    return (ot * lax.rsqrt(ms + LN_EPS) * gain * (1.0 - LAMBDA_INIT_0)).T


def _attn_lat_kernel(q_ref, k_ref, vt_ref, ckt_ref, cv_ref, lp_ref, gain_ref,
                     c_ref, wm_ref, bm_ref, *rest, n_weights):
    w_refs, (o_ref, mod_ref) = rest[:n_weights], rest[n_weights:n_weights + 2]
    wb_refs, (ck_scr, cvt_scr) = rest[n_weights + 2:-2], rest[-2:]

    @pl.when(pl.program_id(1) == 0)
    def _():
        for h in range(N_HEADS_A):
            lo, hi = h * LANES, (h + 1) * LANES
            ck_scr[:, lo:hi] = jnp.concatenate(
                [ckt_ref[0, h, 0].T, ckt_ref[0, h, 1].T], axis=1).astype(BF16)
            cvt_scr[lo:hi, :] = cv_ref[0, :, h, :].T.astype(BF16)

    for w_ref, wb_ref in zip(w_refs, wb_refs):
        wb_ref[...] = w_ref[...].astype(wb_ref.dtype)
    _mod_slab(c_ref, wm_ref, bm_ref, mod_ref)
    lam = _lambda(lp_ref)
    gain = gain_ref[...]
    rows = ATT_SUB_BLOCK
    chunks = [(None, PAST_LEN)] + [(s, ATT_KEY_CHUNK) for s in range(0, DEC_SEQ, ATT_KEY_CHUNK)]
    units = [(sub, h, c) for sub in range(q_ref.shape[0] // rows)
             for h in range(N_HEADS_A) for c in range(len(chunks))]
    qq_of = {}
    scores = {}

    def issue_scores(unit):
        sub, h, c = unit
        lo, hi = h * LANES, (h + 1) * LANES
        if c == 0:
            qq_of[sub, h] = _stack_halves(q_ref[sub * rows:(sub + 1) * rows, lo:hi])
        start, size = chunks[c]
        keys = ck_scr[:, lo:hi] if start is None else k_ref[start:start + size, lo:hi]
        scores[unit] = _scores(keys, qq_of[sub, h])

    for unit in units[:ATT_LOOKAHEAD]:
        issue_scores(unit)
    m = acc = None
    for i, unit in enumerate(units):
        if i + ATT_LOOKAHEAD < len(units):
            issue_scores(units[i + ATT_LOOKAHEAD])
        sub, h, c = unit
        lo, hi = h * LANES, (h + 1) * LANES
        start, size = chunks[c]
        vals_t = cvt_scr[lo:hi, :] if start is None else vt_ref[lo:hi, start:start + size]
        st = scores.pop(unit)
        cm = jnp.max(st, axis=0, keepdims=True)
        m_new = cm if c == 0 else jnp.maximum(m, cm)
        pt = jnp.exp2(st - m_new).astype(BF16)
        part = jnp.dot(_values_with_ones(vals_t), pt,
                       preferred_element_type=F32)
        acc = part if c == 0 else acc * jnp.exp2(m - m_new) + part
        m = m_new
        if c == len(chunks) - 1:
            o_ref[sub * rows:(sub + 1) * rows, lo:hi] = _finish_head(
                acc, rows, lam, gain).astype(o_ref.dtype)


def _attn_lat(q, k, vt, cache_kt, cache_v, lam_params, gain_col, cvec, w_mod, b_mod, weights):
    n = q.shape[0]
    nb = n // DEC_SEQ
    qb = DEC_SEQ // ATT_Q_BLOCK
    steps = nb * qb
    q_spec = pl.BlockSpec((ATT_Q_BLOCK, QK_WIDTH), lambda b, j: (b * qb + j, 0))
    k_spec = pl.BlockSpec((DEC_SEQ, QK_WIDTH), lambda b, j: (b, 0))
    vt_spec = pl.BlockSpec((WIDTH_A, DEC_SEQ), lambda b, j: (b, 0))
    ck_spec = pl.BlockSpec((1, N_HEADS_A, 2, HEAD_DIM_A, PAST_LEN), lambda b, j: (b, 0, 0, 0, 0))
    cvt_spec = pl.BlockSpec((1, PAST_LEN, N_HEADS_A, V_DIM_A), lambda b, j: (b, 0, 0, 0))
    w_specs = [pl.BlockSpec((w.shape[0] // steps, w.shape[1]), lambda b, j: (b * qb + j, 0))
               for w in weights]
    assert all(w.shape[0] % (steps * BF16_SUBLANES) == 0 for w in weights)
    last_slab = N_MOD // MOD_SIDE_TILE - 1
    assert last_slab < steps
    slab = lambda rows: pl.BlockSpec((rows, MOD_SIDE_TILE),
                                     lambda b, j: (0, jnp.minimum(b * qb + j, last_slab)))
    outs = pl.pallas_call(
        functools.partial(_attn_lat_kernel, n_weights=len(weights)),
        grid=(nb, qb),
        in_specs=[q_spec, k_spec, vt_spec, ck_spec, cvt_spec,
                  _const_spec((4, HEAD_DIM_A)), _const_spec((V_DIM_A, 1)),
                  _const_spec((MOD_ROWS, D_MODEL)), slab(D_MODEL), slab(1)] + w_specs,
        out_specs=[q_spec, slab(MOD_ROWS)] + w_specs,
        out_shape=[jax.ShapeDtypeStruct((n, WIDTH_A), BF16),
                   jax.ShapeDtypeStruct((MOD_ROWS, N_MOD), F32)]
        + [jax.ShapeDtypeStruct(w.shape, BF16) for w in weights],
        scratch_shapes=[pltpu.VMEM((PAST_LEN, QK_WIDTH), BF16),
                        pltpu.VMEM((WIDTH_A, PAST_LEN), BF16)],
        compiler_params=_params("arbitrary", "arbitrary"),
        name="attn_lat",
    )(q, k, vt, cache_kt, cache_v, lam_params, gain_col, cvec, w_mod, b_mod, *weights)
    return outs[0], outs[1], outs[2:]


def _conv1_kernel(xp_ref, x_ref, xn_ref, mod_ref, wi_ref, cw_ref, o_ref, z_scr, *, seq_len):
    t = x_ref.shape[0]
    halo = SUBLANES
    cc = CONV_CHUNK
    n_chunks = WIDTH_C // cc
    xin = jnp.concatenate([xp_ref[...], x_ref[...], xn_ref[...]], axis=0)
    h = (xin * (1.0 + _mod(mod_ref, 1)) + _mod(mod_ref, 0)).astype(BF16)
    pos = (pl.program_id(0) * t + lax.broadcasted_iota(jnp.int32, (t, 1), 0)) % seq_len
    not_first = pos != 0
    not_last = pos != seq_len - 1

    def project(j):
        return [jnp.dot(h, wi_ref[:, sec * WIDTH_C + j * cc:sec * WIDTH_C + (j + 1) * cc],
                        preferred_element_type=F32) for sec in range(3)]

    y_next = project(0)
    for j in range(n_chunks):
        gate, conv_gate, conv_in = y_next
        if j + 1 < n_chunks:
            y_next = project(j + 1)
        z_scr[j] = conv_gate * conv_in
        cw = cw_ref[:, j * cc:(j + 1) * cc]
        z_prev = jnp.where(not_first, z_scr[j, halo - 1:halo - 1 + t, :], 0.0)
        z_next = jnp.where(not_last, z_scr[j, halo + 1:halo + 1 + t, :], 0.0)
        conv = z_prev * cw[0:1] + z_scr[j, halo:halo + t, :] * cw[1:2] + z_next * cw[2:3]
        o_ref[:, j * cc:(j + 1) * cc] = (gate[halo:halo + t] * conv).astype(o_ref.dtype)


def _conv1(x, mods, row_fn, w_in, conv_w, seq_len):
    n = x.shape[0]
    t = TOK_TILE
    tiles_per_block = t // SUBLANES
    last = n // SUBLANES - 1
    tok = pl.BlockSpec((t, D_MODEL), lambda i: (i, 0))
    prev = pl.BlockSpec((SUBLANES, D_MODEL),
                        lambda i: (jnp.maximum(i * tiles_per_block - 1, 0), 0))
    nxt = pl.BlockSpec((SUBLANES, D_MODEL),
                       lambda i: (jnp.minimum((i + 1) * tiles_per_block, last), 0))
    return pl.pallas_call(
        functools.partial(_conv1_kernel, seq_len=seq_len),
        grid=(n // t,),
        in_specs=[prev, tok, nxt, _mod_spec(row_fn, t),
                  _const_spec((D_MODEL, 3 * WIDTH_C)), _const_spec((3, WIDTH_C))],
        out_specs=pl.BlockSpec((t, WIDTH_C), lambda i: (i, 0)),
        out_shape=jax.ShapeDtypeStruct((n, WIDTH_C), BF16),
        scratch_shapes=[pltpu.VMEM((WIDTH_C // CONV_CHUNK, t + 2 * SUBLANES, CONV_CHUNK), F32)],
        compiler_params=_params("parallel"),
        name="conv1",
    )(x, x, x, mods, w_in, conv_w)


def _post_kernel(*refs, n_pieces):
    pieces = refs[:n_pieces]
    (x_ref, mod_ref, wo_ref, g1_ref, b1_ref, w1_ref, w2_ref, g2_ref, b2_ref, o_ref) = refs[n_pieces:]
    subs = _sub_tiles(x_ref)
    mixed = []
    for rows in subs:
        out = None
        k0 = 0
        for p_ref in pieces:
            k1 = k0 + p_ref.shape[1]
            part = jnp.dot(p_ref[rows, :], wo_ref[k0:k1, :], preferred_element_type=F32)
            out = part if out is None else out + part
            k0 = k1
        mixed.append(_layer_norm(ALPHA * x_ref[rows, :] + _mod(mod_ref, 2) * out,
                                 g1_ref[...], b1_ref[...]))
    hs = [(x * (1.0 + _mod(mod_ref, 4)) + _mod(mod_ref, 3)).astype(BF16) for x in mixed]
    fs = [None] * len(subs)
    for j in range(D_FF // FF_CHUNK):
        lo, hi = j * FF_CHUNK, (j + 1) * FF_CHUNK
        hids = [jnp.dot(h, w1_ref[:, lo:hi], preferred_element_type=F32) for h in hs]
        for s, hid in enumerate(hids):
            hid = jnp.square(jnp.maximum(hid, 0.0)).astype(BF16)
            part = jnp.dot(hid, w2_ref[lo:hi, :], preferred_element_type=F32)
            fs[s] = part if fs[s] is None else fs[s] + part
    for rows, x, f in zip(subs, mixed, fs):
        o_ref[rows, :] = _layer_norm(ALPHA * x + _mod(mod_ref, 5) * f, g2_ref[...], b2_ref[...])


def _post(pieces, x, mods, row_fn, w_out, g1, b1, w1, w2, g2, b2):
    n = x.shape[0]
    t = TOK_TILE
    tok = lambda width: pl.BlockSpec((t, width), lambda i: (i, 0))
    vec = _const_spec((1, D_MODEL))
    return pl.pallas_call(
        functools.partial(_post_kernel, n_pieces=len(pieces)),
        grid=(n // t,),
        in_specs=[tok(p.shape[1]) for p in pieces] + [
            tok(D_MODEL), _mod_spec(row_fn, t), _const_spec(w_out.shape), vec, vec,
            _const_spec((D_MODEL, D_FF)), _const_spec((D_FF, D_MODEL)), vec, vec],
        out_specs=tok(D_MODEL),
        out_shape=jax.ShapeDtypeStruct((n, D_MODEL), F32),
        compiler_params=_params("parallel"),
        name="post",
    )(*pieces, x, mods, w_out, g1, b1, w1, w2, g2, b2)


def _rope_tables(n):
    f32 = np.float32
    rows = n // GRID_W
    row = np.repeat(np.arange(rows, dtype=f32), GRID_W)
    col = np.tile(np.arange(GRID_W, dtype=f32), rows)
    inv = (f32(1.0) / (f32(ROPE_BASE) ** (np.arange(ROPE_PAIRS, dtype=f32) / f32(ROPE_PAIRS)))).astype(f32)
    ang_r, ang_c = (row[:, None] * inv).astype(f32), (col[:, None] * inv).astype(f32)
    cos_r, sin_r, cos_c, sin_c = (fn(a.astype(np.float64)).astype(f32)
                                  for a in (ang_r, ang_c) for fn in (np.cos, np.sin))
    zero = np.zeros_like(ang_r)
    reps = LANES // HEAD_DIM_A
    cos = np.tile(np.concatenate([cos_r, cos_r, cos_c, cos_c], axis=1), (1, reps))
    sa = np.tile(np.concatenate([-sin_r, zero, -sin_c, zero], axis=1), (1, reps))
    sb = np.tile(np.concatenate([zero, sin_r, zero, sin_c], axis=1), (1, reps))
    return jnp.asarray(cos), jnp.asarray(sa), jnp.asarray(sb)


def kernel(x_prompt, x_sample, cache_k0, cache_v0, c, c_ctx, w_mod0, b_mod0, w_in0, lambda_q1_0, lambda_k1_0, lambda_q2_0, lambda_k2_0, subln_g0, sgu_w0, sgu_b0, w_out0, ln_mix_g0, ln_mix_b0, w_ff1_0, w_ff2_0, ln_ff_g0, ln_ff_b0, w_mod1, b_mod1, w_in1, conv_w1, w_out1, ln_mix_g1, ln_mix_b1, w_ff1_1, w_ff2_1, ln_ff_g1, ln_ff_b1):
    batch, seq, d = x_prompt.shape
    dec_batch, dec_seq, _ = x_sample.shape
    assert (seq, dec_seq, d) == (SEQ, DEC_SEQ, D_MODEL) and 1 + dec_batch <= MOD_ROWS
    row = lambda v: v.reshape(1, -1)

    cvec = jnp.concatenate([c_ctx[None, :], c, jnp.zeros((MOD_ROWS - 1 - dec_batch, d), F32)], axis=0)
    mods0 = _modulations(cvec, w_mod0, row(b_mod0)).reshape(MOD_ROWS, 1, N_MOD)
    ctx_row = lambda tok: 0
    lat_row = lambda tok: 1 + tok // DEC_SEQ

    xp = x_prompt.reshape(batch * seq, d)
    xs = x_sample.reshape(dec_batch * dec_seq, d)

    w_in0_b = w_in0.astype(BF16)
    sgu = (sgu_w0.astype(BF16), sgu_b0.T)
    lam_params = jnp.stack([lambda_q1_0, lambda_k1_0, lambda_q2_0, lambda_k2_0])
    gain = subln_g0.reshape(V_DIM_A, 1)
    ap, sp, new_k0, new_v0 = _in_proj0(xp, mods0, w_in0_b, *sgu, ctx_row,
                                       attn_params=(lam_params, gain))
    qs, ks, vs, ss = _in_proj0(xs, mods0, w_in0_b, *sgu, lat_row, rope_tabs=_rope_tables(dec_seq))
    later = (w_ff1_0, w_ff2_0, w_in1, w_out1, w_ff1_1, w_ff2_1)
    cache_kt = jnp.transpose(cache_k0, (0, 2, 3, 4, 1))
    a_s, mods1, later_b = _attn_lat(qs, ks, vs, cache_kt, cache_v0, lam_params, gain,
                                    cvec, w_mod1, row(b_mod1), later)
    mods1 = mods1.reshape(MOD_ROWS, 1, N_MOD)
    w_ff1_0_b, w_ff2_0_b, w_in1_b, w_out1_b, w_ff1_1_b, w_ff2_1_b = later_b
    post0_w = (w_out0.astype(BF16), row(ln_mix_g0), row(ln_mix_b0),
               w_ff1_0_b, w_ff2_0_b, row(ln_ff_g0), row(ln_ff_b0))
    xp = _post([ap, sp], xp, mods0, ctx_row, *post0_w)
    xs = _post([a_s, ss], xs, mods0, lat_row, *post0_w)

    gp = _conv1(xp, mods1, ctx_row, w_in1_b, conv_w1, seq_len=seq)
    gs = _conv1(xs, mods1, lat_row, w_in1_b, conv_w1, seq_len=dec_seq)
    post1_w = (w_out1_b, row(ln_mix_g1), row(ln_mix_b1),
               w_ff1_1_b, w_ff2_1_b, row(ln_ff_g1), row(ln_ff_b1))
    xp = _post([gp], xp, mods1, ctx_row, *post1_w)
    xs = _post([gs], xs, mods1, lat_row, *post1_w)

    return (xp.reshape(batch, seq, d), xs.reshape(dec_batch, dec_seq, d),
            jnp.transpose(new_k0, (0, 4, 1, 2, 3)),
            new_v0.reshape(batch, seq, N_HEADS_A, V_DIM_A))
```

```python
import functools
import math

import jax
import jax.numpy as jnp
import numpy as np
from jax import lax
from jax.experimental import pallas as pl
from jax.experimental.pallas import tpu as pltpu

F32 = jnp.float32
BF16 = jnp.bfloat16

D_MODEL = 1024
DEPTH = 2
SEQ = 256
DEC_SEQ = 4096
PAST_LEN = 256
GRID_W = 64
N_HEADS_A = 4
HEAD_DIM_A = 64
V_DIM_A = 2 * HEAD_DIM_A
QK_WIDTH = N_HEADS_A * 2 * HEAD_DIM_A
WIDTH_A = N_HEADS_A * V_DIM_A
N_GROUPS_B = 4
CHUNK = 128
GROUP_DIM_B = 128
WIDTH_B = N_GROUPS_B * GROUP_DIM_B
IN_WIDTH_0 = 2 * QK_WIDTH + WIDTH_A + 2 * WIDTH_B
WIDTH_C = D_MODEL
D_FF = 4 * D_MODEL
ROPE_BASE = 10000.0
ROPE_PAIRS = HEAD_DIM_A // 4
LN_EPS = 1e-5
ALPHA = (2 * DEPTH) ** 0.25
LAMBDA_INIT_0 = 0.8 - 0.6 * math.exp(-0.3 * 0)
Q_SCALE = HEAD_DIM_A ** -0.5 * math.log2(math.e)

LANES = 128
SUBLANES = 8
BF16_SUBLANES = 16
VMEM_LIMIT = 56 * 2 ** 20
N_MOD = 6 * D_MODEL
MOD_ROWS = 8
TOK_TILE = 1024
SUB_TILE = 512
CTX_LOOKAHEAD = 3
ATT_Q_BLOCK = 512
ATT_SUB_BLOCK = 128
ATT_KEY_CHUNK = 256
ATT_LOOKAHEAD = 5
FF_CHUNK = 1024
CONV_CHUNK = 256
MOD_TILE = 512
MOD_SIDE_TILE = 256


def _const_spec(shape):
    zeros = (0,) * len(shape)
    return pl.BlockSpec(shape, lambda *_: zeros, pipeline_mode=pl.Buffered(1))


def _params(*sem):
    return pltpu.CompilerParams(dimension_semantics=sem, vmem_limit_bytes=VMEM_LIMIT)


def _layer_norm(x, g, b):
    mu = jnp.mean(x, axis=-1, keepdims=True)
    xc = x - mu
    var = jnp.mean(xc * xc, axis=-1, keepdims=True)
    return xc * lax.rsqrt(var + LN_EPS) * g + b


def _mod(mod_ref, idx):
    return mod_ref[:, idx * D_MODEL:(idx + 1) * D_MODEL]


def _sub_tiles(ref):
    return [slice(i * SUB_TILE, (i + 1) * SUB_TILE) for i in range(ref.shape[0] // SUB_TILE)]


def _mod_slab(c_ref, w_ref, b_ref, o_ref):
    c = c_ref[...]
    s = (c * jax.nn.sigmoid(c)).astype(BF16)
    o_ref[...] = jnp.dot(s, w_ref[...].astype(BF16), preferred_element_type=F32) + b_ref[...]


def _modulations(cvec, w, b, cols):
    w_spec = pl.BlockSpec((D_MODEL, MOD_TILE), lambda j: (0, j))
    v_spec = pl.BlockSpec((1, MOD_TILE), lambda j: (0, j))
    o_spec = pl.BlockSpec((MOD_ROWS, MOD_TILE), lambda j: (0, j))
    return pl.pallas_call(
        _mod_slab,
        grid=(cols // MOD_TILE,),
        in_specs=[_const_spec((MOD_ROWS, D_MODEL)), w_spec, v_spec],
        out_specs=o_spec,
        out_shape=jax.ShapeDtypeStruct((MOD_ROWS, cols), F32),
        compiler_params=_params("parallel"),
        name="adaln_mod",
    )(cvec, w, b)


def _mod_spec(row_fn, tile, width=N_MOD):
    return pl.BlockSpec((None, 1, width), lambda i: (row_fn(i * tile), 0, 0))


def _in0_kernel(*refs, rope):
    if rope:
        (x_ref, mod_ref, w_ref, sw_ref, sbias_ref, cos_ref, sa_ref, sb_ref,
         q_ref, k_ref, v_ref, s_ref) = refs
    else:
        (x_ref, mod_ref, w_ref, sw_ref, sbias_ref, lp_ref, gain_ref,
         a_ref, s_ref, k_out_ref, v_out_ref) = refs
        lam = _lambda(lp_ref)
        gain = gain_ref[...]
    subs = _sub_tiles(x_ref)
    ys = []
    for rows in subs:
        h = (x_ref[rows, :] * (1.0 + _mod(mod_ref, 1)) + _mod(mod_ref, 0)).astype(BF16)
        ys.append(jnp.dot(h, w_ref[...], preferred_element_type=F32))
    for si, (rows, y) in enumerate(zip(subs, ys)):
        if rope:
            cos, sa, sb = cos_ref[rows, :], sa_ref[rows, :], sb_ref[rows, :]
        q_heads, k_heads = [], []
        for j in range(QK_WIDTH // LANES):
            lo, hi = j * LANES, (j + 1) * LANES
            qj = y[:, lo:hi]
            kj = y[:, QK_WIDTH + lo:QK_WIDTH + hi]
            if rope:
                qj = (qj * cos + pltpu.roll(qj, LANES - ROPE_PAIRS, 1) * sa
                      + pltpu.roll(qj, ROPE_PAIRS, 1) * sb)
                kj = (kj * cos + pltpu.roll(kj, LANES - ROPE_PAIRS, 1) * sa
                      + pltpu.roll(kj, ROPE_PAIRS, 1) * sb)
                q_ref[rows, lo:hi] = (qj * Q_SCALE).astype(q_ref.dtype)
                k_ref[rows, lo:hi] = kj.astype(k_ref.dtype)
            else:
                q_heads.append((qj * Q_SCALE).astype(BF16))
                k_heads.append(kj.astype(BF16))
                kt = kj.T
                seqs = SUB_TILE // SEQ
                for b in range(seqs):
                    for i in range(2):
                        k_out_ref[si * seqs + b, j, i] = kt[i * HEAD_DIM_A:(i + 1) * HEAD_DIM_A,
                                                            b * SEQ:(b + 1) * SEQ]
        off = 2 * QK_WIDTH
        v = y[:, off:off + WIDTH_A]
        vt = v.T.astype(BF16)
        if rope:
            v_ref[:, rows] = vt
        else:
            for j in range(N_HEADS_A):
                v_out_ref[rows, j, :] = v[:, j * V_DIM_A:(j + 1) * V_DIM_A]
            units = [(b, hd) for b in range(SUB_TILE // SEQ) for hd in range(N_HEADS_A)]
            scores = {}

            def issue_scores(unit):
                b, hd = unit
                seq = slice(b * SEQ, (b + 1) * SEQ)
                scores[unit] = _scores(k_heads[hd][seq], _stack_halves(q_heads[hd][seq]))

            for unit in units[:CTX_LOOKAHEAD]:
                issue_scores(unit)
            for i, unit in enumerate(units):
                if i + CTX_LOOKAHEAD < len(units):
                    issue_scores(units[i + CTX_LOOKAHEAD])
                b, hd = unit
                seq = slice(b * SEQ, (b + 1) * SEQ)
                st = scores.pop(unit)
                pt = jnp.exp2(st - jnp.max(st, axis=0, keepdims=True)).astype(BF16)
                acc = jnp.dot(_values_with_ones(vt[hd * V_DIM_A:(hd + 1) * V_DIM_A, seq]), pt,
                              preferred_element_type=F32)
                a_ref[rows.start + b * SEQ:rows.start + (b + 1) * SEQ,
                      hd * V_DIM_A:(hd + 1) * V_DIM_A] = _finish_head(
                          acc, SEQ, lam, gain).astype(a_ref.dtype)
        u_off = off + WIDTH_A
        g_off = u_off + WIDTH_B
        for g in range(N_GROUPS_B):
            lo, hi = g * GROUP_DIM_B, (g + 1) * GROUP_DIM_B
            gg = y[:, g_off + lo:g_off + hi]
            mu = jnp.mean(gg, axis=-1, keepdims=True)
            gc = gg - mu
            var = jnp.mean(gc * gc, axis=-1, keepdims=True)
            vc = (gc * lax.rsqrt(var + LN_EPS)).astype(BF16)
            w = sw_ref[g]
            bias = jnp.broadcast_to(sbias_ref[:, g:g + 1], (CHUNK, GROUP_DIM_B))
            for c in range(SUB_TILE // CHUNK):
                r0, r1 = c * CHUNK, (c + 1) * CHUNK
                mixed = jnp.dot(w, vc[r0:r1], preferred_element_type=F32) + bias
                s_ref[rows.start + r0:rows.start + r1, lo:hi] = (
                    y[r0:r1, u_off + lo:u_off + hi] * mixed).astype(s_ref.dtype)


def _in_proj0(x, mods, w_in, sgu_w, sgu_bt, row_fn, rope_tabs=None, attn_params=None):
    n = x.shape[0]
    t = TOK_TILE
    tok = lambda width: pl.BlockSpec((t, width), lambda i: (i, 0))
    in_specs = [tok(D_MODEL), _mod_spec(row_fn, t, mods.shape[2]), _const_spec((D_MODEL, IN_WIDTH_0)),
                _const_spec((N_GROUPS_B, CHUNK, CHUNK)), _const_spec((CHUNK, N_GROUPS_B))]
    args = [x, mods, w_in, sgu_w, sgu_bt]
    if rope_tabs is not None:
        blocks_per_seq = DEC_SEQ // t
        tab = pl.BlockSpec((t, LANES), lambda i: (i % blocks_per_seq, 0))
        in_specs += [tab, tab, tab]
        args += list(rope_tabs)
        v_spec = pl.BlockSpec((WIDTH_A, t), lambda i: (i // blocks_per_seq, i % blocks_per_seq))
        out_shape = [jax.ShapeDtypeStruct((n, QK_WIDTH), BF16),
                     jax.ShapeDtypeStruct((n, QK_WIDTH), BF16),
                     jax.ShapeDtypeStruct((n // DEC_SEQ * WIDTH_A, DEC_SEQ), BF16),
                     jax.ShapeDtypeStruct((n, WIDTH_B), BF16)]
        out_specs = [tok(QK_WIDTH), tok(QK_WIDTH), v_spec, tok(WIDTH_B)]
    else:
        in_specs += [_const_spec((4, HEAD_DIM_A)), _const_spec((V_DIM_A, 1))]
        args += list(attn_params)
        out_shape = [jax.ShapeDtypeStruct((n, WIDTH_A), BF16),
                     jax.ShapeDtypeStruct((n, WIDTH_B), BF16),
                     jax.ShapeDtypeStruct((n // SEQ, N_HEADS_A, 2, HEAD_DIM_A, SEQ), F32),
                     jax.ShapeDtypeStruct((n, N_HEADS_A, V_DIM_A), F32)]
        out_specs = [tok(WIDTH_A), tok(WIDTH_B),
                     pl.BlockSpec((t // SEQ, N_HEADS_A, 2, HEAD_DIM_A, SEQ),
                                  lambda i: (i, 0, 0, 0, 0)),
                     pl.BlockSpec((t, N_HEADS_A, V_DIM_A), lambda i: (i, 0, 0))]
    return pl.pallas_call(
        functools.partial(_in0_kernel, rope=rope_tabs is not None),
        grid=(n // t,),
        in_specs=in_specs,
        out_specs=out_specs,
        out_shape=out_shape,
        compiler_params=_params("parallel"),
        name="in_proj0",
    )(*args)


def _lambda(lp_ref):
    lp = lp_ref[...]
    a = jnp.sum(lp[0:1] * lp[1:2], axis=1, keepdims=True)
    b = jnp.sum(lp[2:3] * lp[3:4], axis=1, keepdims=True)
    return jnp.exp(a) - jnp.exp(b) + LAMBDA_INIT_0


def _stack_halves(qh):
    lane = lax.broadcasted_iota(jnp.int32, qh.shape, 1)
    zero = jnp.zeros_like(qh)
    return jnp.concatenate([jnp.where(lane < HEAD_DIM_A, qh, zero),
                            jnp.where(lane >= HEAD_DIM_A, qh, zero)], axis=0)


def _scores(qq, k):
    return lax.dot_general(qq, k, (((1,), (1,)), ((), ())), preferred_element_type=F32)


def _values_with_ones(vals_t):
    return jnp.concatenate([vals_t, jnp.ones((BF16_SUBLANES, vals_t.shape[1]), BF16)], axis=0)


def _finish_head(acc, rows, lam, gain):
    pvt = acc[:V_DIM_A]
    l = acc[V_DIM_A:V_DIM_A + 1]
    ot = pvt[:, :rows] / l[:, :rows] - lam * (pvt[:, rows:] / l[:, rows:])
    ms = jnp.mean(ot * ot, axis=0, keepdims=True)
    return (ot * lax.rsqrt(ms + LN_EPS) * gain * (1.0 - LAMBDA_INIT_0)).T


def _attn_lat_kernel(q_ref, k_ref, vt_ref, ckt_ref, cv_ref, lp_ref, gain_ref,
                     c_ref, wm_ref, bm_ref, wm0_ref, bm0_ref, *rest, n_weights):
    w_refs, (o_ref, mod_ref, mod0_ref) = rest[:n_weights], rest[n_weights:n_weights + 3]
    wb_refs, (ck_scr, cvt_scr) = rest[n_weights + 3:-2], rest[-2:]

    @pl.when(pl.program_id(1) == 0)
    def _():
        for h in range(N_HEADS_A):
            lo, hi = h * LANES, (h + 1) * LANES
            ck_scr[:, lo:hi] = jnp.concatenate(
                [ckt_ref[0, h, 0].T, ckt_ref[0, h, 1].T], axis=1).astype(BF16)
            cvt_scr[lo:hi, :] = cv_ref[0, :, h, :].T.astype(BF16)

    for w_ref, wb_ref in zip(w_refs, wb_refs):
        wb_ref[...] = w_ref[...].astype(wb_ref.dtype)
    _mod_slab(c_ref, wm_ref, bm_ref, mod_ref)
    _mod_slab(c_ref, wm0_ref, bm0_ref, mod0_ref)
    lam = _lambda(lp_ref)
    gain = gain_ref[...]
    rows = ATT_SUB_BLOCK
    chunks = [(None, PAST_LEN)] + [(s, ATT_KEY_CHUNK) for s in range(0, DEC_SEQ, ATT_KEY_CHUNK)]
    units = [(sub, h, c) for sub in range(q_ref.shape[0] // rows)
             for h in range(N_HEADS_A) for c in range(len(chunks))]
    qq_of = {}
    scores = {}

    def issue_scores(unit):
        sub, h, c = unit
        lo, hi = h * LANES, (h + 1) * LANES
        if c == 0:
            qq_of[sub, h] = _stack_halves(q_ref[sub * rows:(sub + 1) * rows, lo:hi])
        start, size = chunks[c]
        keys = ck_scr[:, lo:hi] if start is None else k_ref[start:start + size, lo:hi]
        scores[unit] = _scores(keys, qq_of[sub, h])

    for unit in units[:ATT_LOOKAHEAD]:
        issue_scores(unit)
    m = acc = None
    for i, unit in enumerate(units):
        if i + ATT_LOOKAHEAD < len(units):
            issue_scores(units[i + ATT_LOOKAHEAD])
        sub, h, c = unit
        lo, hi = h * LANES, (h + 1) * LANES
        start, size = chunks[c]
        vals_t = cvt_scr[lo:hi, :] if start is None else vt_ref[lo:hi, start:start + size]
        st = scores.pop(unit)
        cm = jnp.max(st, axis=0, keepdims=True)
        m_new = cm if c == 0 else jnp.maximum(m, cm)
        pt = jnp.exp2(st - m_new).astype(BF16)
        part = jnp.dot(_values_with_ones(vals_t), pt,
                       preferred_element_type=F32)
        acc = part if c == 0 else acc * jnp.exp2(m - m_new) + part
        m = m_new
        if c == len(chunks) - 1:
            o_ref[sub * rows:(sub + 1) * rows, lo:hi] = _finish_head(
                acc, rows, lam, gain).astype(o_ref.dtype)


def _attn_lat(q, k, vt, cache_kt, cache_v, lam_params, gain_col, cvec, w_mod, b_mod,
              w_mod_this, b_mod_this, done_cols, weights):
    n = q.shape[0]
    nb = n // DEC_SEQ
    qb = DEC_SEQ // ATT_Q_BLOCK
    steps = nb * qb
    q_spec = pl.BlockSpec((ATT_Q_BLOCK, QK_WIDTH), lambda b, j: (b * qb + j, 0))
    k_spec = pl.BlockSpec((DEC_SEQ, QK_WIDTH), lambda b, j: (b, 0))
    vt_spec = pl.BlockSpec((WIDTH_A, DEC_SEQ), lambda b, j: (b, 0))
    ck_spec = pl.BlockSpec((1, N_HEADS_A, 2, HEAD_DIM_A, PAST_LEN), lambda b, j: (b, 0, 0, 0, 0))
    cvt_spec = pl.BlockSpec((1, PAST_LEN, N_HEADS_A, V_DIM_A), lambda b, j: (b, 0, 0, 0))
    w_specs = [pl.BlockSpec((w.shape[0] // steps, w.shape[1]), lambda b, j: (b * qb + j, 0))
               for w in weights]
    assert all(w.shape[0] % (steps * BF16_SUBLANES) == 0 for w in weights)
    last_slab = N_MOD // MOD_SIDE_TILE - 1
    assert last_slab < steps
    slab = lambda rows: pl.BlockSpec((rows, MOD_SIDE_TILE),
                                     lambda b, j: (0, jnp.minimum(b * qb + j, last_slab)))
    first0 = done_cols // MOD_SIDE_TILE
    last0 = (N_MOD - done_cols) // MOD_SIDE_TILE - 1
    slab0_in = lambda rows: pl.BlockSpec(
        (rows, MOD_SIDE_TILE), lambda b, j: (0, first0 + jnp.minimum(b * qb + j, last0)))
    slab0_out = pl.BlockSpec((MOD_ROWS, MOD_SIDE_TILE),
                             lambda b, j: (0, jnp.minimum(b * qb + j, last0)))
    outs = pl.pallas_call(
        functools.partial(_attn_lat_kernel, n_weights=len(weights)),
        grid=(nb, qb),
        in_specs=[q_spec, k_spec, vt_spec, ck_spec, cvt_spec,
                  _const_spec((4, HEAD_DIM_A)), _const_spec((V_DIM_A, 1)),
                  _const_spec((MOD_ROWS, D_MODEL)), slab(D_MODEL), slab(1),
                  slab0_in(D_MODEL), slab0_in(1)] + w_specs,
        out_specs=[q_spec, slab(MOD_ROWS), slab0_out] + w_specs,
        out_shape=[jax.ShapeDtypeStruct((n, WIDTH_A), BF16),
                   jax.ShapeDtypeStruct((MOD_ROWS, N_MOD), F32),
                   jax.ShapeDtypeStruct((MOD_ROWS, N_MOD - done_cols), F32)]
        + [jax.ShapeDtypeStruct(w.shape, BF16) for w in weights],
        scratch_shapes=[pltpu.VMEM((PAST_LEN, QK_WIDTH), BF16),
                        pltpu.VMEM((WIDTH_A, PAST_LEN), BF16)],
        compiler_params=_params("arbitrary", "arbitrary"),
        name="attn_lat",
    )(q, k, vt, cache_kt, cache_v, lam_params, gain_col, cvec, w_mod, b_mod,
      w_mod_this, b_mod_this, *weights)
    return outs[0], outs[1], outs[2], outs[3:]


def _conv1_kernel(xp_ref, x_ref, xn_ref, mod_ref, wi_ref, cw_ref, o_ref, z_scr, *, seq_len):
    t = x_ref.shape[0]
    halo = SUBLANES
    cc = CONV_CHUNK
    n_chunks = WIDTH_C // cc
    xin = jnp.concatenate([xp_ref[...], x_ref[...], xn_ref[...]], axis=0)
    h = (xin * (1.0 + _mod(mod_ref, 1)) + _mod(mod_ref, 0)).astype(BF16)
    pos = (pl.program_id(0) * t + lax.broadcasted_iota(jnp.int32, (t, 1), 0)) % seq_len
    not_first = pos != 0
    not_last = pos != seq_len - 1

    def project(j):
        return [jnp.dot(h, wi_ref[:, sec * WIDTH_C + j * cc:sec * WIDTH_C + (j + 1) * cc],
                        preferred_element_type=F32) for sec in range(3)]

    y_next = project(0)
    for j in range(n_chunks):
        gate, conv_gate, conv_in = y_next
        if j + 1 < n_chunks:
            y_next = project(j + 1)
        z_scr[j] = conv_gate * conv_in
        cw = cw_ref[:, j * cc:(j + 1) * cc]
        z_prev = jnp.where(not_first, z_scr[j, halo - 1:halo - 1 + t, :], 0.0)
        z_next = jnp.where(not_last, z_scr[j, halo + 1:halo + 1 + t, :], 0.0)
        conv = z_prev * cw[0:1] + z_scr[j, halo:halo + t, :] * cw[1:2] + z_next * cw[2:3]
        o_ref[:, j * cc:(j + 1) * cc] = (gate[halo:halo + t] * conv).astype(o_ref.dtype)


def _conv1(x, mods, row_fn, w_in, conv_w, seq_len):
    n = x.shape[0]
    t = TOK_TILE
    tiles_per_block = t // SUBLANES
    last = n // SUBLANES - 1
    tok = pl.BlockSpec((t, D_MODEL), lambda i: (i, 0))
    prev = pl.BlockSpec((SUBLANES, D_MODEL),
                        lambda i: (jnp.maximum(i * tiles_per_block - 1, 0), 0))
    nxt = pl.BlockSpec((SUBLANES, D_MODEL),
                       lambda i: (jnp.minimum((i + 1) * tiles_per_block, last), 0))
    return pl.pallas_call(
        functools.partial(_conv1_kernel, seq_len=seq_len),
        grid=(n // t,),
        in_specs=[prev, tok, nxt, _mod_spec(row_fn, t),
                  _const_spec((D_MODEL, 3 * WIDTH_C)), _const_spec((3, WIDTH_C))],
        out_specs=pl.BlockSpec((t, WIDTH_C), lambda i: (i, 0)),
        out_shape=jax.ShapeDtypeStruct((n, WIDTH_C), BF16),
        scratch_shapes=[pltpu.VMEM((WIDTH_C // CONV_CHUNK, t + 2 * SUBLANES, CONV_CHUNK), F32)],
        compiler_params=_params("parallel"),
        name="conv1",
    )(x, x, x, mods, w_in, conv_w)


def _post_kernel(*refs, n_pieces):
    pieces = refs[:n_pieces]
    (x_ref, mod_ref, wo_ref, g1_ref, b1_ref, w1_ref, w2_ref, g2_ref, b2_ref, o_ref) = refs[n_pieces:]
    subs = _sub_tiles(x_ref)
    mixed = []
    for rows in subs:
        out = None
        k0 = 0
        for p_ref in pieces:
            k1 = k0 + p_ref.shape[1]
            part = jnp.dot(p_ref[rows, :], wo_ref[k0:k1, :], preferred_element_type=F32)
            out = part if out is None else out + part
            k0 = k1
        mixed.append(_layer_norm(ALPHA * x_ref[rows, :] + _mod(mod_ref, 2) * out,
                                 g1_ref[...], b1_ref[...]))
    hs = [(x * (1.0 + _mod(mod_ref, 4)) + _mod(mod_ref, 3)).astype(BF16) for x in mixed]
    fs = [None] * len(subs)
    for j in range(D_FF // FF_CHUNK):
        lo, hi = j * FF_CHUNK, (j + 1) * FF_CHUNK
        hids = [jnp.dot(h, w1_ref[:, lo:hi], preferred_element_type=F32) for h in hs]
        for s, hid in enumerate(hids):
            hid = jnp.square(jnp.maximum(hid, 0.0)).astype(BF16)
            part = jnp.dot(hid, w2_ref[lo:hi, :], preferred_element_type=F32)
            fs[s] = part if fs[s] is None else fs[s] + part
    for rows, x, f in zip(subs, mixed, fs):
        o_ref[rows, :] = _layer_norm(ALPHA * x + _mod(mod_ref, 5) * f, g2_ref[...], b2_ref[...])


def _post(pieces, x, mods, row_fn, w_out, g1, b1, w1, w2, g2, b2):
    n = x.shape[0]
    t = TOK_TILE
    tok = lambda width: pl.BlockSpec((t, width), lambda i: (i, 0))
    vec = _const_spec((1, D_MODEL))
    return pl.pallas_call(
        functools.partial(_post_kernel, n_pieces=len(pieces)),
        grid=(n // t,),
        in_specs=[tok(p.shape[1]) for p in pieces] + [
            tok(D_MODEL), _mod_spec(row_fn, t), _const_spec(w_out.shape), vec, vec,
            _const_spec((D_MODEL, D_FF)), _const_spec((D_FF, D_MODEL)), vec, vec],
        out_specs=tok(D_MODEL),
        out_shape=jax.ShapeDtypeStruct((n, D_MODEL), F32),
        compiler_params=_params("parallel"),
        name="post",
    )(*pieces, x, mods, w_out, g1, b1, w1, w2, g2, b2)


def _rope_tables(n):
    f32 = np.float32
    rows = n // GRID_W
    row = np.repeat(np.arange(rows, dtype=f32), GRID_W)
    col = np.tile(np.arange(GRID_W, dtype=f32), rows)
    inv = (f32(1.0) / (f32(ROPE_BASE) ** (np.arange(ROPE_PAIRS, dtype=f32) / f32(ROPE_PAIRS)))).astype(f32)
    ang_r, ang_c = (row[:, None] * inv).astype(f32), (col[:, None] * inv).astype(f32)
    cos_r, sin_r, cos_c, sin_c = (fn(a.astype(np.float64)).astype(f32)
                                  for a in (ang_r, ang_c) for fn in (np.cos, np.sin))
    zero = np.zeros_like(ang_r)
    reps = LANES // HEAD_DIM_A
    cos = np.tile(np.concatenate([cos_r, cos_r, cos_c, cos_c], axis=1), (1, reps))
    sa = np.tile(np.concatenate([-sin_r, zero, -sin_c, zero], axis=1), (1, reps))
    sb = np.tile(np.concatenate([zero, sin_r, zero, sin_c], axis=1), (1, reps))
    return jnp.asarray(cos), jnp.asarray(sa), jnp.asarray(sb)


def kernel(x_prompt, x_sample, cache_k0, cache_v0, c, c_ctx, w_mod0, b_mod0, w_in0, lambda_q1_0, lambda_k1_0, lambda_q2_0, lambda_k2_0, subln_g0, sgu_w0, sgu_b0, w_out0, ln_mix_g0, ln_mix_b0, w_ff1_0, w_ff2_0, ln_ff_g0, ln_ff_b0, w_mod1, b_mod1, w_in1, conv_w1, w_out1, ln_mix_g1, ln_mix_b1, w_ff1_1, w_ff2_1, ln_ff_g1, ln_ff_b1):
    batch, seq, d = x_prompt.shape
    dec_batch, dec_seq, _ = x_sample.shape
    assert (seq, dec_seq, d) == (SEQ, DEC_SEQ, D_MODEL) and 1 + dec_batch <= MOD_ROWS
    row = lambda v: v.reshape(1, -1)

    cvec = jnp.concatenate([c_ctx[None, :], c, jnp.zeros((MOD_ROWS - 1 - dec_batch, d), F32)], axis=0)
    in_cols = 2 * D_MODEL
    mods0_in = _modulations(cvec, w_mod0, row(b_mod0), in_cols)
    mods0 = mods0_in.reshape(MOD_ROWS, 1, in_cols)
    ctx_row = lambda tok: 0
    lat_row = lambda tok: 1 + tok // DEC_SEQ

    xp = x_prompt.reshape(batch * seq, d)
    xs = x_sample.reshape(dec_batch * dec_seq, d)

    w_in0_b = w_in0.astype(BF16)
    sgu = (sgu_w0.astype(BF16), sgu_b0.T)
    lam_params = jnp.stack([lambda_q1_0, lambda_k1_0, lambda_q2_0, lambda_k2_0])
    gain = subln_g0.reshape(V_DIM_A, 1)
    ap, sp, new_k0, new_v0 = _in_proj0(xp, mods0, w_in0_b, *sgu, ctx_row,
                                       attn_params=(lam_params, gain))
    qs, ks, vs, ss = _in_proj0(xs, mods0, w_in0_b, *sgu, lat_row, rope_tabs=_rope_tables(dec_seq))
    later = (w_ff1_0, w_ff2_0, w_in1, w_out1, w_ff1_1, w_ff2_1)
    cache_kt = jnp.transpose(cache_k0, (0, 2, 3, 4, 1))
    a_s, mods1, mods0_rest, later_b = _attn_lat(
        qs, ks, vs, cache_kt, cache_v0, lam_params, gain, cvec, w_mod1, row(b_mod1),
        w_mod0, row(b_mod0), in_cols, later)
    mods1 = mods1.reshape(MOD_ROWS, 1, N_MOD)
    mods0 = jnp.concatenate([mods0_in, mods0_rest], axis=1).reshape(MOD_ROWS, 1, N_MOD)
    w_ff1_0_b, w_ff2_0_b, w_in1_b, w_out1_b, w_ff1_1_b, w_ff2_1_b = later_b
    post0_w = (w_out0.astype(BF16), row(ln_mix_g0), row(ln_mix_b0),
               w_ff1_0_b, w_ff2_0_b, row(ln_ff_g0), row(ln_ff_b0))
    xp = _post([ap, sp], xp, mods0, ctx_row, *post0_w)
    xs = _post([a_s, ss], xs, mods0, lat_row, *post0_w)

    gp = _conv1(xp, mods1, ctx_row, w_in1_b, conv_w1, seq_len=seq)
    gs = _conv1(xs, mods1, lat_row, w_in1_b, conv_w1, seq_len=dec_seq)
    post1_w = (w_out1_b, row(ln_mix_g1), row(ln_mix_b1),
               w_ff1_1_b, w_ff2_1_b, row(ln_ff_g1), row(ln_ff_b1))
    xp = _post([gp], xp, mods1, ctx_row, *post1_w)
    xs = _post([gs], xs, mods1, lat_row, *post1_w)

    return (xp.reshape(batch, seq, d), xs.reshape(dec_batch, dec_seq, d),
            jnp.transpose(new_k0, (0, 4, 1, 2, 3)),
            new_v0.reshape(batch, seq, N_HEADS_A, V_DIM_A))
```

```python
import functools
import math

import jax
import jax.numpy as jnp
import numpy as np
from jax import lax
from jax.experimental import pallas as pl
from jax.experimental.pallas import tpu as pltpu

F32 = jnp.float32
BF16 = jnp.bfloat16

D_MODEL = 1024
DEPTH = 2
SEQ = 256
DEC_SEQ = 4096
PAST_LEN = 256
GRID_W = 64
N_HEADS_A = 4
HEAD_DIM_A = 64
V_DIM_A = 2 * HEAD_DIM_A
QK_WIDTH = N_HEADS_A * 2 * HEAD_DIM_A
WIDTH_A = N_HEADS_A * V_DIM_A
N_GROUPS_B = 4
CHUNK = 128
GROUP_DIM_B = 128
WIDTH_B = N_GROUPS_B * GROUP_DIM_B
IN_WIDTH_0 = 2 * QK_WIDTH + WIDTH_A + 2 * WIDTH_B
WIDTH_C = D_MODEL
D_FF = 4 * D_MODEL
ROPE_BASE = 10000.0
ROPE_PAIRS = HEAD_DIM_A // 4
LN_EPS = 1e-5
ALPHA = (2 * DEPTH) ** 0.25
LAMBDA_INIT_0 = 0.8 - 0.6 * math.exp(-0.3 * 0)
Q_SCALE = HEAD_DIM_A ** -0.5 * math.log2(math.e)

LANES = 128
SUBLANES = 8
BF16_SUBLANES = 16
VMEM_LIMIT = 56 * 2 ** 20
N_MOD = 6 * D_MODEL
MOD_ROWS = 8
TOK_TILE = 1024
SUB_TILE = 512
CTX_LOOKAHEAD = 3
ATT_Q_BLOCK = 512
ATT_SUB_BLOCK = 128
ATT_KEY_CHUNK = 256
ATT_LOOKAHEAD = 5
FF_CHUNK = 1024
CONV_CHUNK = 256
MOD_TILE = 512
MOD_SIDE_TILE = 256


def _const_spec(shape):
    zeros = (0,) * len(shape)
    return pl.BlockSpec(shape, lambda *_: zeros, pipeline_mode=pl.Buffered(1))


def _params(*sem):
    return pltpu.CompilerParams(dimension_semantics=sem, vmem_limit_bytes=VMEM_LIMIT)


def _layer_norm(x, g, b):
    mu = jnp.mean(x, axis=-1, keepdims=True)
    xc = x - mu
    var = jnp.mean(xc * xc, axis=-1, keepdims=True)
    return xc * lax.rsqrt(var + LN_EPS) * g + b


def _mod(mod_ref, idx):
    return mod_ref[:, idx * D_MODEL:(idx + 1) * D_MODEL]


def _sub_tiles(ref):
    return [slice(i * SUB_TILE, (i + 1) * SUB_TILE) for i in range(ref.shape[0] // SUB_TILE)]


def _mod_slab(c_ref, w_ref, b_ref, o_ref):
    c = c_ref[...]
    s = (c * jax.nn.sigmoid(c)).astype(BF16)
    o_ref[...] = jnp.dot(s, w_ref[...].astype(BF16), preferred_element_type=F32) + b_ref[...]


def _modulations(cvec, w, b):
    w_spec = pl.BlockSpec((D_MODEL, MOD_TILE), lambda j: (0, j))
    v_spec = pl.BlockSpec((1, MOD_TILE), lambda j: (0, j))
    o_spec = pl.BlockSpec((MOD_ROWS, MOD_TILE), lambda j: (0, j))
    return pl.pallas_call(
        _mod_slab,
        grid=(N_MOD // MOD_TILE,),
        in_specs=[_const_spec((MOD_ROWS, D_MODEL)), w_spec, v_spec],
        out_specs=o_spec,
        out_shape=jax.ShapeDtypeStruct((MOD_ROWS, N_MOD), F32),
        compiler_params=_params("parallel"),
        name="adaln_mod",
    )(cvec, w, b)


def _mod_spec(row_fn, tile):
    return pl.BlockSpec((None, 1, N_MOD), lambda i: (row_fn(i * tile), 0, 0))


def _in0_kernel(*refs, rope):
    if rope:
        (x_ref, mod_ref, w_ref, sw_ref, sbias_ref, cos_ref, sa_ref, sb_ref,
         q_ref, k_ref, v_ref, s_ref) = refs
    else:
        (x_ref, mod_ref, w_ref, sw_ref, sbias_ref, lp_ref, gain_ref,
         a_ref, s_ref, k_out_ref, v_out_ref) = refs
        lam = _lambda(lp_ref)
        gain = gain_ref[...]
    subs = _sub_tiles(x_ref)
    ys = []
    for rows in subs:
        h = (x_ref[rows, :] * (1.0 + _mod(mod_ref, 1)) + _mod(mod_ref, 0)).astype(BF16)
        ys.append(jnp.dot(h, w_ref[...], preferred_element_type=F32))
    for si, (rows, y) in enumerate(zip(subs, ys)):
        if rope:
            cos, sa, sb = cos_ref[rows, :], sa_ref[rows, :], sb_ref[rows, :]
        q_heads, k_heads = [], []
        for j in range(QK_WIDTH // LANES):
            lo, hi = j * LANES, (j + 1) * LANES
            qj = y[:, lo:hi]
            kj = y[:, QK_WIDTH + lo:QK_WIDTH + hi]
            if rope:
                qj = (qj * cos + pltpu.roll(qj, LANES - ROPE_PAIRS, 1) * sa
                      + pltpu.roll(qj, ROPE_PAIRS, 1) * sb)
                kj = (kj * cos + pltpu.roll(kj, LANES - ROPE_PAIRS, 1) * sa
                      + pltpu.roll(kj, ROPE_PAIRS, 1) * sb)
                q_ref[rows, lo:hi] = (qj * Q_SCALE).astype(q_ref.dtype)
                k_ref[rows, lo:hi] = kj.astype(k_ref.dtype)
            else:
                q_heads.append((qj * Q_SCALE).astype(BF16))
                k_heads.append(kj.astype(BF16))
                kt = kj.T
                seqs = SUB_TILE // SEQ
                for b in range(seqs):
                    for i in range(2):
                        k_out_ref[si * seqs + b, j, i] = kt[i * HEAD_DIM_A:(i + 1) * HEAD_DIM_A,
                                                            b * SEQ:(b + 1) * SEQ]
        off = 2 * QK_WIDTH
        v = y[:, off:off + WIDTH_A]
        vt = v.T.astype(BF16)
        if rope:
            v_ref[:, rows] = vt
        else:
            for j in range(N_HEADS_A):
                v_out_ref[rows, j, :] = v[:, j * V_DIM_A:(j + 1) * V_DIM_A]
            units = [(b, hd) for b in range(SUB_TILE // SEQ) for hd in range(N_HEADS_A)]
            scores = {}

            def issue_scores(unit):
                b, hd = unit
                seq = slice(b * SEQ, (b + 1) * SEQ)
                scores[unit] = _scores(k_heads[hd][seq], _stack_halves(q_heads[hd][seq]))

            for unit in units[:CTX_LOOKAHEAD]:
                issue_scores(unit)
            for i, unit in enumerate(units):
                if i + CTX_LOOKAHEAD < len(units):
                    issue_scores(units[i + CTX_LOOKAHEAD])
                b, hd = unit
                seq = slice(b * SEQ, (b + 1) * SEQ)
                st = scores.pop(unit)
                pt = jnp.exp2(st - jnp.max(st, axis=0, keepdims=True)).astype(BF16)
                acc = jnp.dot(_values_with_ones(vt[hd * V_DIM_A:(hd + 1) * V_DIM_A, seq]), pt,
                              preferred_element_type=F32)
                a_ref[rows.start + b * SEQ:rows.start + (b + 1) * SEQ,
                      hd * V_DIM_A:(hd + 1) * V_DIM_A] = _finish_head(
                          acc, SEQ, lam, gain).astype(a_ref.dtype)
        u_off = off + WIDTH_A
        g_off = u_off + WIDTH_B
        for g in range(N_GROUPS_B):
            lo, hi = g * GROUP_DIM_B, (g + 1) * GROUP_DIM_B
            gg = y[:, g_off + lo:g_off + hi]
            mu = jnp.mean(gg, axis=-1, keepdims=True)
            gc = gg - mu
            var = jnp.mean(gc * gc, axis=-1, keepdims=True)
            vc = (gc * lax.rsqrt(var + LN_EPS)).astype(BF16)
            w = sw_ref[g]
            bias = jnp.broadcast_to(sbias_ref[:, g:g + 1], (CHUNK, GROUP_DIM_B))
            for c in range(SUB_TILE // CHUNK):
                r0, r1 = c * CHUNK, (c + 1) * CHUNK
                mixed = jnp.dot(w, vc[r0:r1], preferred_element_type=F32) + bias
                s_ref[rows.start + r0:rows.start + r1, lo:hi] = (
                    y[r0:r1, u_off + lo:u_off + hi] * mixed).astype(s_ref.dtype)


def _in_proj0(x, mods, w_in, sgu_w, sgu_bt, row_fn, rope_tabs=None, attn_params=None):
    n = x.shape[0]
    t = TOK_TILE
    tok = lambda width: pl.BlockSpec((t, width), lambda i: (i, 0))
    in_specs = [tok(D_MODEL), _mod_spec(row_fn, t), _const_spec((D_MODEL, IN_WIDTH_0)),
                _const_spec((N_GROUPS_B, CHUNK, CHUNK)), _const_spec((CHUNK, N_GROUPS_B))]
    args = [x, mods, w_in, sgu_w, sgu_bt]
    if rope_tabs is not None:
        blocks_per_seq = DEC_SEQ // t
        tab = pl.BlockSpec((t, LANES), lambda i: (i % blocks_per_seq, 0))
        in_specs += [tab, tab, tab]
        args += list(rope_tabs)
        v_spec = pl.BlockSpec((WIDTH_A, t), lambda i: (i // blocks_per_seq, i % blocks_per_seq))
        out_shape = [jax.ShapeDtypeStruct((n, QK_WIDTH), BF16),
                     jax.ShapeDtypeStruct((n, QK_WIDTH), BF16),
                     jax.ShapeDtypeStruct((n // DEC_SEQ * WIDTH_A, DEC_SEQ), BF16),
                     jax.ShapeDtypeStruct((n, WIDTH_B), BF16)]
        out_specs = [tok(QK_WIDTH), tok(QK_WIDTH), v_spec, tok(WIDTH_B)]
    else:
        in_specs += [_const_spec((4, HEAD_DIM_A)), _const_spec((V_DIM_A, 1))]
        args += list(attn_params)
        out_shape = [jax.ShapeDtypeStruct((n, WIDTH_A), BF16),
                     jax.ShapeDtypeStruct((n, WIDTH_B), BF16),
                     jax.ShapeDtypeStruct((n // SEQ, N_HEADS_A, 2, HEAD_DIM_A, SEQ), F32),
                     jax.ShapeDtypeStruct((n, N_HEADS_A, V_DIM_A), F32)]
        out_specs = [tok(WIDTH_A), tok(WIDTH_B),
                     pl.BlockSpec((t // SEQ, N_HEADS_A, 2, HEAD_DIM_A, SEQ),
                                  lambda i: (i, 0, 0, 0, 0)),
                     pl.BlockSpec((t, N_HEADS_A, V_DIM_A), lambda i: (i, 0, 0))]
    return pl.pallas_call(
        functools.partial(_in0_kernel, rope=rope_tabs is not None),
        grid=(n // t,),
        in_specs=in_specs,
        out_specs=out_specs,
        out_shape=out_shape,
        compiler_params=_params("parallel"),
        name="in_proj0",
    )(*args)


def _lambda(lp_ref):
    lp = lp_ref[...]
    a = jnp.sum(lp[0:1] * lp[1:2], axis=1, keepdims=True)
    b = jnp.sum(lp[2:3] * lp[3:4], axis=1, keepdims=True)
    return jnp.exp(a) - jnp.exp(b) + LAMBDA_INIT_0


def _stack_halves(qh):
    lane = lax.broadcasted_iota(jnp.int32, qh.shape, 1)
    zero = jnp.zeros_like(qh)
    return jnp.concatenate([jnp.where(lane < HEAD_DIM_A, qh, zero),
                            jnp.where(lane >= HEAD_DIM_A, qh, zero)], axis=0)


def _scores(qq, k):
    return lax.dot_general(qq, k, (((1,), (1,)), ((), ())), preferred_element_type=F32)


def _values_with_ones(vals_t):
    return jnp.concatenate([vals_t, jnp.ones((BF16_SUBLANES, vals_t.shape[1]), BF16)], axis=0)


def _finish_head(acc, rows, lam, gain):
    pvt = acc[:V_DIM_A]
    l = acc[V_DIM_A:V_DIM_A + 1]
    ot = pvt[:, :rows] / l[:, :rows] - lam * (pvt[:, rows:] / l[:, rows:])
    ms = jnp.mean(ot * ot, axis=0, keepdims=True)
    return (ot * lax.rsqrt(ms + LN_EPS) * gain * (1.0 - LAMBDA_INIT_0)).T


def _attn_lat_kernel(q_ref, k_ref, vt_ref, ckt_ref, cv_ref, lp_ref, gain_ref,
                     c_ref, wm_ref, bm_ref, *rest, n_weights):
    w_refs, (o_ref, mod_ref) = rest[:n_weights], rest[n_weights:n_weights + 2]
    wb_refs, (ck_scr, cvt_scr) = rest[n_weights + 2:-2], rest[-2:]

    @pl.when(pl.program_id(1) == 0)
    def _():
        for h in range(N_HEADS_A):
            lo, hi = h * LANES, (h + 1) * LANES
            ck_scr[:, lo:hi] = jnp.concatenate(
                [ckt_ref[0, h, 0].T, ckt_ref[0, h, 1].T], axis=1).astype(BF16)
            cvt_scr[lo:hi, :] = cv_ref[0, :, h, :].T.astype(BF16)

    for w_ref, wb_ref in zip(w_refs, wb_refs):
        wb_ref[...] = w_ref[...].astype(wb_ref.dtype)
    _mod_slab(c_ref, wm_ref, bm_ref, mod_ref)
    lam = _lambda(lp_ref)
    gain = gain_ref[...]
    rows = ATT_SUB_BLOCK
    chunks = [(None, PAST_LEN)] + [(s, ATT_KEY_CHUNK) for s in range(0, DEC_SEQ, ATT_KEY_CHUNK)]
    units = [(sub, h, c) for sub in range(q_ref.shape[0] // rows)
             for h in range(N_HEADS_A) for c in range(len(chunks))]
    qq_of = {}
    scores = {}

    def issue_scores(unit):
        sub, h, c = unit
        lo, hi = h * LANES, (h + 1) * LANES
        if c == 0:
            qq_of[sub, h] = _stack_halves(q_ref[sub * rows:(sub + 1) * rows, lo:hi])
        start, size = chunks[c]
        keys = ck_scr[:, lo:hi] if start is None else k_ref[start:start + size, lo:hi]
        scores[unit] = _scores(keys, qq_of[sub, h])

    for unit in units[:ATT_LOOKAHEAD]:
        issue_scores(unit)
    m = acc = None
    for i, unit in enumerate(units):
        if i + ATT_LOOKAHEAD < len(units):
            issue_scores(units[i + ATT_LOOKAHEAD])
        sub, h, c = unit
        lo, hi = h * LANES, (h + 1) * LANES
        start, size = chunks[c]
        vals_t = cvt_scr[lo:hi, :] if start is None else vt_ref[lo:hi, start:start + size]
        st = scores.pop(unit)
        cm = jnp.max(st, axis=0, keepdims=True)
        m_new = cm if c == 0 else jnp.maximum(m, cm)
        pt = jnp.exp2(st - m_new).astype(BF16)
        part = jnp.dot(_values_with_ones(vals_t), pt,
                       preferred_element_type=F32)
        acc = part if c == 0 else acc * jnp.exp2(m - m_new) + part
        m = m_new
        if c == len(chunks) - 1:
            o_ref[sub * rows:(sub + 1) * rows, lo:hi] = _finish_head(
                acc, rows, lam, gain).astype(o_ref.dtype)


def _attn_lat(q, k, vt, cache_kt, cache_v, lam_params, gain_col, cvec, w_mod, b_mod, weights):
    n = q.shape[0]
    nb = n // DEC_SEQ
    qb = DEC_SEQ // ATT_Q_BLOCK
    steps = nb * qb
    q_spec = pl.BlockSpec((ATT_Q_BLOCK, QK_WIDTH), lambda b, j: (b * qb + j, 0))
    k_spec = pl.BlockSpec((DEC_SEQ, QK_WIDTH), lambda b, j: (b, 0))
    vt_spec = pl.BlockSpec((WIDTH_A, DEC_SEQ), lambda b, j: (b, 0))
    ck_spec = pl.BlockSpec((1, N_HEADS_A, 2, HEAD_DIM_A, PAST_LEN), lambda b, j: (b, 0, 0, 0, 0))
    cvt_spec = pl.BlockSpec((1, PAST_LEN, N_HEADS_A, V_DIM_A), lambda b, j: (b, 0, 0, 0))
    w_specs = [pl.BlockSpec((w.shape[0] // steps, w.shape[1]), lambda b, j: (b * qb + j, 0))
               for w in weights]
    assert all(w.shape[0] % (steps * BF16_SUBLANES) == 0 for w in weights)
    last_slab = N_MOD // MOD_SIDE_TILE - 1
    assert last_slab < steps
    slab = lambda rows: pl.BlockSpec((rows, MOD_SIDE_TILE),
                                     lambda b, j: (0, jnp.minimum(b * qb + j, last_slab)))
    outs = pl.pallas_call(
        functools.partial(_attn_lat_kernel, n_weights=len(weights)),
        grid=(nb, qb),
        in_specs=[q_spec, k_spec, vt_spec, ck_spec, cvt_spec,
                  _const_spec((4, HEAD_DIM_A)), _const_spec((V_DIM_A, 1)),
                  _const_spec((MOD_ROWS, D_MODEL)), slab(D_MODEL), slab(1)] + w_specs,
        out_specs=[q_spec, slab(MOD_ROWS)] + w_specs,
        out_shape=[jax.ShapeDtypeStruct((n, WIDTH_A), BF16),
                   jax.ShapeDtypeStruct((MOD_ROWS, N_MOD), F32)]
        + [jax.ShapeDtypeStruct(w.shape, BF16) for w in weights],
        scratch_shapes=[pltpu.VMEM((PAST_LEN, QK_WIDTH), BF16),
                        pltpu.VMEM((WIDTH_A, PAST_LEN), BF16)],
        compiler_params=_params("arbitrary", "arbitrary"),
        name="attn_lat",
    )(q, k, vt, cache_kt, cache_v, lam_params, gain_col, cvec, w_mod, b_mod, *weights)
    return outs[0], outs[1], outs[2:]


def _conv1_kernel(xp_ref, x_ref, xn_ref, mod_ref, wi_ref, cw_ref, o_ref, z_scr, *, seq_len):
    t = x_ref.shape[0]
    halo = SUBLANES
    cc = CONV_CHUNK
    n_chunks = WIDTH_C // cc
    xin = jnp.concatenate([xp_ref[...], x_ref[...], xn_ref[...]], axis=0)
    h = (xin * (1.0 + _mod(mod_ref, 1)) + _mod(mod_ref, 0)).astype(BF16)
    pos = (pl.program_id(0) * t + lax.broadcasted_iota(jnp.int32, (t, 1), 0)) % seq_len
    not_first = pos != 0
    not_last = pos != seq_len - 1

    def project(j):
        return [jnp.dot(h, wi_ref[:, sec * WIDTH_C + j * cc:sec * WIDTH_C + (j + 1) * cc],
                        preferred_element_type=F32) for sec in range(3)]

    y_next = project(0)
    for j in range(n_chunks):
        gate, conv_gate, conv_in = y_next
        if j + 1 < n_chunks:
            y_next = project(j + 1)
        z_scr[j] = conv_gate * conv_in
        cw = cw_ref[:, j * cc:(j + 1) * cc]
        z_prev = jnp.where(not_first, z_scr[j, halo - 1:halo - 1 + t, :], 0.0)
        z_next = jnp.where(not_last, z_scr[j, halo + 1:halo + 1 + t, :], 0.0)
        conv = z_prev * cw[0:1] + z_scr[j, halo:halo + t, :] * cw[1:2] + z_next * cw[2:3]
        o_ref[:, j * cc:(j + 1) * cc] = (gate[halo:halo + t] * conv).astype(o_ref.dtype)


def _conv1(x, mods, row_fn, w_in, conv_w, seq_len):
    n = x.shape[0]
    t = TOK_TILE
    tiles_per_block = t // SUBLANES
    last = n // SUBLANES - 1
    tok = pl.BlockSpec((t, D_MODEL), lambda i: (i, 0))
    prev = pl.BlockSpec((SUBLANES, D_MODEL),
                        lambda i: (jnp.maximum(i * tiles_per_block - 1, 0), 0))
    nxt = pl.BlockSpec((SUBLANES, D_MODEL),
                       lambda i: (jnp.minimum((i + 1) * tiles_per_block, last), 0))
    return pl.pallas_call(
        functools.partial(_conv1_kernel, seq_len=seq_len),
        grid=(n // t,),
        in_specs=[prev, tok, nxt, _mod_spec(row_fn, t),
                  _const_spec((D_MODEL, 3 * WIDTH_C)), _const_spec((3, WIDTH_C))],
        out_specs=pl.BlockSpec((t, WIDTH_C), lambda i: (i, 0)),
        out_shape=jax.ShapeDtypeStruct((n, WIDTH_C), BF16),
        scratch_shapes=[pltpu.VMEM((WIDTH_C // CONV_CHUNK, t + 2 * SUBLANES, CONV_CHUNK), F32)],
        compiler_params=_params("parallel"),
        name="conv1",
    )(x, x, x, mods, w_in, conv_w)


def _post_kernel(*refs, n_pieces):
    pieces = refs[:n_pieces]
    (x_ref, mod_ref, wo_ref, g1_ref, b1_ref, w1_ref, w2_ref, g2_ref, b2_ref, o_ref) = refs[n_pieces:]
    subs = _sub_tiles(x_ref)
    mixed = []
    for rows in subs:
        out = None
        k0 = 0
        for p_ref in pieces:
            k1 = k0 + p_ref.shape[1]
            part = jnp.dot(p_ref[rows, :], wo_ref[k0:k1, :], preferred_element_type=F32)
            out = part if out is None else out + part
            k0 = k1
        x1 = _layer_norm(ALPHA * x_ref[rows, :] + _mod(mod_ref, 2) * out,
                         g1_ref[...], b1_ref[...])
        o_ref[rows, :] = x1
        mixed.append(x1)
    hs = [(x * (1.0 + _mod(mod_ref, 4)) + _mod(mod_ref, 3)).astype(BF16) for x in mixed]
    fs = [None] * len(subs)
    for j in range(D_FF // FF_CHUNK):
        lo, hi = j * FF_CHUNK, (j + 1) * FF_CHUNK
        hids = [jnp.dot(h, w1_ref[:, lo:hi], preferred_element_type=F32) for h in hs]
        for s, hid in enumerate(hids):
            hid = jnp.square(jnp.maximum(hid, 0.0)).astype(BF16)
            part = jnp.dot(hid, w2_ref[lo:hi, :], preferred_element_type=F32)
            fs[s] = part if fs[s] is None else fs[s] + part
    for rows, f in zip(subs, fs):
        o_ref[rows, :] = _layer_norm(ALPHA * o_ref[rows, :] + _mod(mod_ref, 5) * f,
                                     g2_ref[...], b2_ref[...])


def _post(pieces, x, mods, row_fn, w_out, g1, b1, w1, w2, g2, b2):
    n = x.shape[0]
    t = TOK_TILE
    tok = lambda width: pl.BlockSpec((t, width), lambda i: (i, 0))
    vec = _const_spec((1, D_MODEL))
    return pl.pallas_call(
        functools.partial(_post_kernel, n_pieces=len(pieces)),
        grid=(n // t,),
        in_specs=[tok(p.shape[1]) for p in pieces] + [
            tok(D_MODEL), _mod_spec(row_fn, t), _const_spec(w_out.shape), vec, vec,
            _const_spec((D_MODEL, D_FF)), _const_spec((D_FF, D_MODEL)), vec, vec],
        out_specs=tok(D_MODEL),
        out_shape=jax.ShapeDtypeStruct((n, D_MODEL), F32),
        compiler_params=_params("parallel"),
        name="post",
    )(*pieces, x, mods, w_out, g1, b1, w1, w2, g2, b2)


def _rope_tables(n):
    f32 = np.float32
    rows = n // GRID_W
    row = np.repeat(np.arange(rows, dtype=f32), GRID_W)
    col = np.tile(np.arange(GRID_W, dtype=f32), rows)
    inv = (f32(1.0) / (f32(ROPE_BASE) ** (np.arange(ROPE_PAIRS, dtype=f32) / f32(ROPE_PAIRS)))).astype(f32)
    ang_r, ang_c = (row[:, None] * inv).astype(f32), (col[:, None] * inv).astype(f32)
    cos_r, sin_r, cos_c, sin_c = (fn(a.astype(np.float64)).astype(f32)
                                  for a in (ang_r, ang_c) for fn in (np.cos, np.sin))
    zero = np.zeros_like(ang_r)
    reps = LANES // HEAD_DIM_A
    cos = np.tile(np.concatenate([cos_r, cos_r, cos_c, cos_c], axis=1), (1, reps))
    sa = np.tile(np.concatenate([-sin_r, zero, -sin_c, zero], axis=1), (1, reps))
    sb = np.tile(np.concatenate([zero, sin_r, zero, sin_c], axis=1), (1, reps))
    return jnp.asarray(cos), jnp.asarray(sa), jnp.asarray(sb)


def kernel(x_prompt, x_sample, cache_k0, cache_v0, c, c_ctx, w_mod0, b_mod0, w_in0, lambda_q1_0, lambda_k1_0, lambda_q2_0, lambda_k2_0, subln_g0, sgu_w0, sgu_b0, w_out0, ln_mix_g0, ln_mix_b0, w_ff1_0, w_ff2_0, ln_ff_g0, ln_ff_b0, w_mod1, b_mod1, w_in1, conv_w1, w_out1, ln_mix_g1, ln_mix_b1, w_ff1_1, w_ff2_1, ln_ff_g1, ln_ff_b1):
    batch, seq, d = x_prompt.shape
    dec_batch, dec_seq, _ = x_sample.shape
    assert (seq, dec_seq, d) == (SEQ, DEC_SEQ, D_MODEL) and 1 + dec_batch <= MOD_ROWS
    row = lambda v: v.reshape(1, -1)

    cvec = jnp.concatenate([c_ctx[None, :], c, jnp.zeros((MOD_ROWS - 1 - dec_batch, d), F32)], axis=0)
    mods0 = _modulations(cvec, w_mod0, row(b_mod0)).reshape(MOD_ROWS, 1, N_MOD)
    ctx_row = lambda tok: 0
    lat_row = lambda tok: 1 + tok // DEC_SEQ

    xp = x_prompt.reshape(batch * seq, d)
    xs = x_sample.reshape(dec_batch * dec_seq, d)

    w_in0_b = w_in0.astype(BF16)
    sgu = (sgu_w0.astype(BF16), sgu_b0.T)
    lam_params = jnp.stack([lambda_q1_0, lambda_k1_0, lambda_q2_0, lambda_k2_0])
    gain = subln_g0.reshape(V_DIM_A, 1)
    ap, sp, new_k0, new_v0 = _in_proj0(xp, mods0, w_in0_b, *sgu, ctx_row,
                                       attn_params=(lam_params, gain))
    qs, ks, vs, ss = _in_proj0(xs, mods0, w_in0_b, *sgu, lat_row, rope_tabs=_rope_tables(dec_seq))
    later = (w_ff1_0, w_ff2_0, w_in1, w_out1, w_ff1_1, w_ff2_1)
    cache_kt = jnp.transpose(cache_k0, (0, 2, 3, 4, 1))
    a_s, mods1, later_b = _attn_lat(qs, ks, vs, cache_kt, cache_v0, lam_params, gain,
                                    cvec, w_mod1, row(b_mod1), later)
    mods1 = mods1.reshape(MOD_ROWS, 1, N_MOD)
    w_ff1_0_b, w_ff2_0_b, w_in1_b, w_out1_b, w_ff1_1_b, w_ff2_1_b = later_b
    post0_w = (w_out0.astype(BF16), row(ln_mix_g0), row(ln_mix_b0),
               w_ff1_0_b, w_ff2_0_b, row(ln_ff_g0), row(ln_ff_b0))
    xp = _post([ap, sp], xp, mods0, ctx_row, *post0_w)
    xs = _post([a_s, ss], xs, mods0, lat_row, *post0_w)

    gp = _conv1(xp, mods1, ctx_row, w_in1_b, conv_w1, seq_len=seq)
    gs = _conv1(xs, mods1, lat_row, w_in1_b, conv_w1, seq_len=dec_seq)
    post1_w = (w_out1_b, row(ln_mix_g1), row(ln_mix_b1),
               w_ff1_1_b, w_ff2_1_b, row(ln_ff_g1), row(ln_ff_b1))
    xp = _post([gp], xp, mods1, ctx_row, *post1_w)
    xs = _post([gs], xs, mods1, lat_row, *post1_w)

    return (xp.reshape(batch, seq, d), xs.reshape(dec_batch, dec_seq, d),
            jnp.transpose(new_k0, (0, 4, 1, 2, 3)),
            new_v0.reshape(batch, seq, N_HEADS_A, V_DIM_A))
```
